```python
import math
import jax, jax.numpy as jnp
from jax import lax
import numpy as np

D_MODEL = 1024
BATCH = 32
SEQ = 2048
DEPTH = 2
DEC_BATCH = 8
DEC_SEQ = 4096
PAST_LEN = 128

GRID_W = 64
EPS = 1e-6
RET_HEADS = 4
RET_DK = 128
RET_DV = 128
RET_W = RET_HEADS * RET_DK
RET_CHUNK = 128
ROPE_BASE = 10000.0
S5_GROUP = 16
S5_CH = D_MODEL // 2
S5_GROUPS = S5_CH // S5_GROUP
S5_STATE = 64
AB_IN = 4 * RET_W + S5_CH
AB_MIX = RET_HEADS * RET_DV + S5_CH
NA_HEADS = 16
NA_DH = D_MODEL // NA_HEADS
NA_KH = 8
NA_KW = 16
NA_QB = 16
NA_BAND = 32
PEER_HEADS = 8
PEER_NKEYS = 128
PEER_N = PEER_NKEYS * PEER_NKEYS
PEER_DQ = 256
PEER_TOPK = 16
PEER_BLOCK = 128
N_EVEN = (DEPTH + 1) // 2
N_ODD = DEPTH // 2

kernel_name = "hybrid_retention_s5_natten_peer_encoder"


def rmsnorm(x, g):
    xf = x.astype(jnp.float32)
    y = xf * lax.rsqrt(jnp.mean(xf * xf, axis=-1, keepdims=True) + EPS)
    return (y * g.astype(jnp.float32)).astype(x.dtype)


def rotary(x):
    T, d = x.shape[1], x.shape[-1]
    half = d // 2
    inv = ROPE_BASE ** (-jnp.arange(half, dtype=jnp.float32) / half)
    ang = jnp.arange(T, dtype=jnp.float32)[:, None] * inv[None, :]
    cos = jnp.cos(ang)[None, :, None, :]
    sin = jnp.sin(ang)[None, :, None, :]
    x1, x2 = x[..., :half], x[..., half:]
    return jnp.concatenate([x1 * cos - x2 * sin, x1 * sin + x2 * cos], axis=-1)


def retention_dir(q, k, v, log_gamma, strict):
    B, H, T, dk = q.shape
    dv = v.shape[-1]
    C = RET_CHUNK
    NC = T // C
    qc = q.reshape(B, H, NC, C, dk)
    kc = k.reshape(B, H, NC, C, dk)
    vc = v.reshape(B, H, NC, C, dv)
    pos = jnp.arange(C, dtype=jnp.float32)
    diff = pos[:, None] - pos[None, :]
    mask = (diff > 0) if strict else (diff >= 0)
    decay = jnp.where(mask[None], jnp.exp(log_gamma[:, None, None] * jnp.maximum(diff, 0.0)[None]), 0.0)
    s = jnp.einsum('bhnid,bhnjd->bhnij', qc, kc) * decay[None, :, None]
    inner = jnp.einsum('bhnij,bhnje->bhnie', s, vc)
    k_dec = kc * jnp.exp(log_gamma[:, None] * (C - 1.0 - pos)[None])[None, :, None, :, None]
    kv = jnp.einsum('bhnjd,bhnje->nbhde', k_dec, vc)
    chunk_decay = jnp.exp(log_gamma * C)[None, :, None, None]

    def step(state, kv_n):
        return state * chunk_decay + kv_n, state

    _, s_prev = lax.scan(step, jnp.zeros((B, H, dk, dv), kv.dtype), kv)
    q_dec = qc * jnp.exp(log_gamma[:, None] * (pos + 1.0)[None])[None, :, None, :, None]
    cross = jnp.einsum('bhnid,nbhde->bhnie', q_dec, s_prev)
    return (inner + cross).reshape(B, H, T, dv)


def _ssm_combine(e1, e2):
    a1, b1 = e1
    a2, b2 = e2
    return a1 * a2, a2 * b1 + b2


def s5_mix(u, lam_re, lam_im, log_dt, b_re, b_im, c_re, c_im, d_skip, glu_w, glu_b):
    B, T, G, Cg = u.shape
    f32 = jnp.float32
    uf = u.astype(f32)
    lam = lax.complex(lam_re.astype(f32), lam_im.astype(f32))
    lam_bar = jnp.exp(lam * jnp.exp(log_dt.astype(f32))[..., None])
    bmat = lax.complex(b_re.astype(f32), b_im.astype(f32))
    b_bar = ((lam_bar - 1.0) / lam)[..., None] * bmat[None]
    cmat = lax.complex(c_re.astype(f32), c_im.astype(f32))
    uc = uf.astype(jnp.complex64)
    y = uf * d_skip.astype(f32)
    for dirn in range(2):
        bu = jnp.einsum('btgc,gpc->btgp', uc, b_bar[dirn])
        a = jnp.broadcast_to(lam_bar[dirn], bu.shape)
        _, xs = lax.associative_scan(_ssm_combine, (a, bu), axis=1, reverse=(dirn == 1))
        y = y + jnp.real(jnp.einsum('gcp,btgp->btgc', cmat[dirn], xs))
    yg = jax.nn.gelu(y, approximate=False)
    out = yg * jax.nn.sigmoid(jnp.einsum('btgc,gce->btge', yg, glu_w.astype(f32)) + glu_b.astype(f32))
    return out.reshape(B, T, G * Cg)


def mixer_ab(h, w_in, ret_decay, ret_gn, lam_re, lam_im, log_dt, b_re, b_im, c_re, c_im, d_skip, glu_w, glu_b, w_out):
    B, T, _ = h.shape
    f32 = jnp.float32
    z = h @ w_in
    q, k, v, g, u = jnp.split(z, [RET_W, 2 * RET_W, 3 * RET_W, 4 * RET_W], axis=-1)
    q = rotary(q.astype(f32).reshape(B, T, RET_HEADS, RET_DK)).transpose(0, 2, 1, 3)
    k = (rotary(k.astype(f32).reshape(B, T, RET_HEADS, RET_DK)) * RET_DK ** -0.5).transpose(0, 2, 1, 3)
    v = v.astype(f32).reshape(B, T, RET_HEADS, RET_DV).transpose(0, 2, 1, 3)
    log_gamma = -jax.nn.softplus(-ret_decay.astype(f32))
    fwd = retention_dir(q, k, v, log_gamma[0], False)
    bwd = jnp.flip(retention_dir(jnp.flip(q, 2), jnp.flip(k, 2), jnp.flip(v, 2), log_gamma[1], True), 2)
    o = (fwd + bwd).transpose(0, 2, 1, 3)
    oc = o - jnp.mean(o, axis=-1, keepdims=True)
    o = oc * lax.rsqrt(jnp.mean(oc * oc, axis=-1, keepdims=True) + EPS)
    ret = o.reshape(B, T, RET_HEADS * RET_DV) * ret_gn.astype(f32) * jax.nn.silu(g.astype(f32))
    ssm = s5_mix(u.reshape(B, T, S5_GROUPS, S5_GROUP), lam_re, lam_im, log_dt, b_re, b_im,
                 c_re, c_im, d_skip, glu_w, glu_b)
    return jnp.concatenate([ret, ssm], axis=-1).astype(h.dtype) @ w_out


def mixer_na(h, w_qkv, q_gain, k_gain, rpb, w_o):
    B, T, _ = h.shape
    rows = T // GRID_W
    kh = min(NA_KH, rows)
    ncb = GRID_W // NA_QB
    f32 = jnp.float32
    z = h @ w_qkv
    q, k, v = jnp.split(z, 3, axis=-1)
    q = rmsnorm(q.reshape(B, rows, GRID_W, NA_HEADS, NA_DH), q_gain) * NA_DH ** -0.5
    k = rmsnorm(k.reshape(B, rows, GRID_W, NA_HEADS, NA_DH), k_gain)
    v = v.reshape(B, rows, GRID_W, NA_HEADS, NA_DH)
    qb = q.reshape(B, rows, ncb, NA_QB, NA_HEADS, NA_DH)
    r = np.arange(rows)
    r_start = np.clip(r - kh // 2, 0, rows - kh)
    key_rows = r_start[:, None] + np.arange(kh)[None, :]
    dr = key_rows - r[:, None] + NA_KH - 1
    j = np.arange(GRID_W)
    c_start = np.clip(j - NA_KW // 2, 0, GRID_W - NA_KW)
    key_cols = c_start[:, None] + np.arange(NA_KW)[None, :]
    dc = key_cols - j[:, None] + NA_KW - 1
    band_start = np.clip(np.arange(ncb) * NA_QB - NA_KW // 2, 0, GRID_W - NA_BAND)
    band_cols = band_start[:, None] + np.arange(NA_BAND)[None, :]
    local = key_cols.reshape(ncb, NA_QB, NA_KW) - band_start[:, None, None]
    sel = jnp.asarray((local[..., None] == np.arange(NA_BAND)).astype(np.float32))
    bias = rpb.astype(f32)[:, dr[:, :, None, None], dc[None, None, :, :]]
    bias = bias.reshape(NA_HEADS, rows, kh, ncb, NA_QB, NA_KW).transpose(1, 3, 0, 4, 2, 5)
    logits = []
    for o in range(kh):
        k_o = k[:, key_rows[:, o]][:, :, band_cols]
        s = jnp.einsum('brnqhd,brnkhd->brnhqk', qb, k_o).astype(f32)
        logits.append(jnp.einsum('brnhqk,nqck->brnhqc', s, sel))
    logits = jnp.stack(logits, axis=-2) + bias
    lshape = logits.shape
    p = jax.nn.softmax(logits.reshape(lshape[:-2] + (kh * NA_KW,)), axis=-1).reshape(lshape)
    out = jnp.zeros((B, rows, ncb, NA_QB, NA_HEADS, NA_DH), f32)
    for o in range(kh):
        v_o = v[:, key_rows[:, o]][:, :, band_cols]
        w_band = jnp.einsum('brnhqc,nqck->brnhqk', p[..., o, :], sel)
        out = out + jnp.einsum('brnhqk,brnkhd->brnqhd', w_band, v_o.astype(f32))
    return out.reshape(B, T, NA_HEADS * NA_DH).astype(h.dtype) @ w_o


def peer(h, w_q, sub_keys, u_emb, v_emb):
    B, T, D = h.shape
    tok = h.reshape(B * T, D)
    n = tok.shape[0]
    q = (tok @ w_q).reshape(n, PEER_HEADS, 2, PEER_DQ // 2)
    s = jnp.einsum('nhpd,pkd->nhpk', q, sub_keys).astype(jnp.float32)
    top_s, top_i = lax.top_k(s, PEER_TOPK)
    cand_s = (top_s[:, :, 0, :, None] + top_s[:, :, 1, None, :]).reshape(n, PEER_HEADS, PEER_TOPK * PEER_TOPK)
    cand_i = (top_i[:, :, 0, :, None] * PEER_NKEYS + top_i[:, :, 1, None, :]).reshape(n, PEER_HEADS, PEER_TOPK * PEER_TOPK)
    best_s, best_pos = lax.top_k(cand_s, PEER_TOPK)
    idx = jnp.take_along_axis(cand_i, best_pos, axis=-1)
    gate = jax.nn.softmax(best_s, axis=-1)

    def block(args):
        xb, ib, gb = args
        act = jax.nn.gelu(jnp.einsum('pd,phkd->phk', xb, u_emb[ib]).astype(jnp.float32), approximate=False)
        return jnp.einsum('phk,phkd->pd', gb * act, v_emb[ib].astype(jnp.float32))

    nb = n // PEER_BLOCK
    out = lax.map(block, (tok.reshape(nb, PEER_BLOCK, D),
                          idx.reshape(nb, PEER_BLOCK, PEER_HEADS, PEER_TOPK),
                          gate.reshape(nb, PEER_BLOCK, PEER_HEADS, PEER_TOPK)))
    return out.reshape(B, T, D)


def setup_inputs(seed: int = 0) -> dict:
    key = jax.random.key(seed)
    ks = iter(jax.random.split(key, 40))
    f32 = jnp.float32

    def nrm(shape, scale):
        return jax.random.normal(next(ks), shape, f32) * scale

    E, O, L = N_EVEN, N_ODD, DEPTH
    G, P = S5_GROUPS, S5_STATE
    x_prompt = nrm((BATCH, SEQ, D_MODEL), 1.0)
    x_sample = nrm((DEC_BATCH, DEC_SEQ, D_MODEL), 1.0)
    g0 = 1.0 - 2.0 ** (-5.0 - jnp.arange(RET_HEADS, dtype=f32))
    ab_ret_decay = (jnp.log(g0) - jnp.log1p(-g0))[None, None, :] + nrm((E, 2, RET_HEADS), 0.01)
    return {
        "x_prompt": x_prompt,
        "x_sample": x_sample,
        "norm_mix": 1.0 + nrm((L, D_MODEL), 0.02),
        "norm_ffn": 1.0 + nrm((L, D_MODEL), 0.02),
        "ab_w_in": nrm((E, D_MODEL, AB_IN), D_MODEL ** -0.5),
        "ab_ret_decay": ab_ret_decay,
        "ab_ret_gn": 1.0 + nrm((E, RET_HEADS * RET_DV), 0.02),
        "ab_s5_lam_re": -0.5 + nrm((E, 2, G, P), 0.01),
        "ab_s5_lam_im": math.pi * jnp.arange(P, dtype=f32) + nrm((E, 2, G, P), 0.01),
        "ab_s5_log_dt": jax.random.uniform(next(ks), (E, 2, G), f32, math.log(1e-3), math.log(1e-1)),
        "ab_s5_b_re": nrm((E, G, P, S5_GROUP), (2.0 * S5_GROUP) ** -0.5),
        "ab_s5_b_im": nrm((E, G, P, S5_GROUP), (2.0 * S5_GROUP) ** -0.5),
        "ab_s5_c_re": nrm((E, 2, G, S5_GROUP, P), P ** -0.5),
        "ab_s5_c_im": nrm((E, 2, G, S5_GROUP, P), P ** -0.5),
        "ab_s5_d": nrm((E, G, S5_GROUP), 1.0),
        "ab_s5_glu_w": nrm((E, G, S5_GROUP, S5_GROUP), S5_GROUP ** -0.5),
        "ab_s5_glu_b": nrm((E, G, S5_GROUP), 0.02),
        "ab_w_out": nrm((E, AB_MIX, D_MODEL), AB_MIX ** -0.5),
        "na_w_qkv": nrm((O, D_MODEL, 3 * NA_HEADS * NA_DH), D_MODEL ** -0.5),
        "na_q_gain": 1.0 + nrm((O, NA_HEADS, NA_DH), 0.02),
        "na_k_gain": 1.0 + nrm((O, NA_HEADS, NA_DH), 0.02),
        "na_rpb": nrm((O, NA_HEADS, 2 * NA_KH - 1, 2 * NA_KW - 1), 0.02),
        "na_w_o": nrm((O, NA_HEADS * NA_DH, D_MODEL), (NA_HEADS * NA_DH) ** -0.5),
        "peer_w_q": nrm((L, D_MODEL, PEER_HEADS * PEER_DQ), D_MODEL ** -0.5),
        "peer_sub_keys": nrm((L, 2, PEER_NKEYS, PEER_DQ // 2), (PEER_DQ // 2) ** -0.5),
        "peer_u": nrm((L, PEER_N, D_MODEL), D_MODEL ** -0.5),
        "peer_v": nrm((L, PEER_N, D_MODEL), PEER_HEADS ** -0.5),
    }


def reference(x_prompt, x_sample, norm_mix, norm_ffn, ab_w_in, ab_ret_decay, ab_ret_gn,
              ab_s5_lam_re, ab_s5_lam_im, ab_s5_log_dt, ab_s5_b_re, ab_s5_b_im, ab_s5_c_re, ab_s5_c_im,
              ab_s5_d, ab_s5_glu_w, ab_s5_glu_b, ab_w_out, na_w_qkv, na_q_gain, na_k_gain, na_rpb, na_w_o,
              peer_w_q, peer_sub_keys, peer_u, peer_v):
    def trunk(x):
        for l in range(DEPTH):
            h = rmsnorm(x, norm_mix[l])
            i = l // 2
            if l % 2 == 0:
                m = mixer_ab(h, ab_w_in[i], ab_ret_decay[i], ab_ret_gn[i], ab_s5_lam_re[i], ab_s5_lam_im[i],
                             ab_s5_log_dt[i], ab_s5_b_re[i], ab_s5_b_im[i], ab_s5_c_re[i], ab_s5_c_im[i],
                             ab_s5_d[i], ab_s5_glu_w[i], ab_s5_glu_b[i], ab_w_out[i])
            else:
                m = mixer_na(h, na_w_qkv[i], na_q_gain[i], na_k_gain[i], na_rpb[i], na_w_o[i])
            x = x + m.astype(x.dtype)
            x = x + peer(rmsnorm(x, norm_ffn[l]), peer_w_q[l], peer_sub_keys[l], peer_u[l], peer_v[l]).astype(x.dtype)
        return x

    y_prompt = trunk(x_prompt)
    y_sample = trunk(x_sample)
    return (y_prompt, y_sample)
```

```python
import functools
import math
from typing import NamedTuple

import numpy as np
import jax
import jax.numpy as jnp
from jax import lax
from jax.experimental import pallas as pl
from jax.experimental.pallas import tpu as pltpu

F32 = jnp.float32
BF16 = jnp.bfloat16
I32 = jnp.int32

V7X_LANES = 128
V7X_SUBLANES = 8
V7X_VMEM_BYTES = 64 * 2**20
VMEM_LIMIT_CAP = V7X_VMEM_BYTES - 8 * 2**20

NEG_BIG = -1e30


class Cfg(NamedTuple):
    eps: float = 1e-6
    grid_w: int = 64
    ret_heads: int = 4
    ret_dk: int = 128
    ret_chunk: int = 128
    rope_base: float = 10000.0
    s5_group: int = 16
    s5_state: int = 64
    s5_chunk: int = 16
    na_heads: int = 16
    na_kh: int = 8
    na_kw: int = 16
    peer_heads: int = 8
    peer_nkeys: int = 128
    peer_topk: int = 16
    tm: int = 512
    peer_tm: int = 256
    peer_te: int = 1024
    topk_tm: int = 256


def _vmem_limit(nbytes):
    return int(min(VMEM_LIMIT_CAP, max(32 * 2**20, nbytes)))


def _gelu(x):
    return 0.5 * x * (1.0 + lax.erf(x * (1.0 / math.sqrt(2.0))))


def _sigmoid(x):
    return 1.0 / (1.0 + jnp.exp(-x))


def _norm_mm_body(x_ref, g_ref, w_ref, *out_refs, splits, emit_hn, eps):
    x = x_ref[...]
    y = x * lax.rsqrt(jnp.mean(x * x, axis=-1, keepdims=True) + eps) * g_ref[...]
    yb = y.astype(BF16)
    z = jnp.dot(yb, w_ref[...], preferred_element_type=F32)
    for r, (s, e) in zip(out_refs, splits):
        r[...] = z[:, s:e].astype(r.dtype)
    if emit_hn:
        out_refs[len(splits)][...] = yb


def _norm_matmul(x2, gain, w_bf16, splits, dtypes, cfg, emit_hn=False):
    n, d = x2.shape
    nout = w_bf16.shape[1]
    tm = min(cfg.tm, n)
    out_shape = [jax.ShapeDtypeStruct((n, e - s), dt) for (s, e), dt in zip(splits, dtypes)]
    out_specs = [pl.BlockSpec((tm, e - s), lambda i: (i, 0)) for (s, e) in splits]
    if emit_hn:
        out_shape.append(jax.ShapeDtypeStruct((n, d), BF16))
        out_specs.append(pl.BlockSpec((tm, d), lambda i: (i, 0)))
    est = 2 * (tm * d * 4 + d * nout * 2 + tm * nout * 4 + tm * d * 2) + 2 * tm * nout * 4
    return pl.pallas_call(
        functools.partial(_norm_mm_body, splits=tuple(splits), emit_hn=emit_hn, eps=cfg.eps),
        out_shape=out_shape,
        grid=(n // tm,),
        in_specs=[pl.BlockSpec((tm, d), lambda i: (i, 0)),
                  pl.BlockSpec((1, d), lambda i: (0, 0)),
                  pl.BlockSpec((d, nout), lambda i: (0, 0))],
        out_specs=out_specs,
        compiler_params=pltpu.CompilerParams(dimension_semantics=("parallel",),
                                             vmem_limit_bytes=_vmem_limit(est)),
        name="norm_matmul",
    )(x2, gain.reshape(1, d).astype(F32), w_bf16)


def _mm_res_body(*refs, n_in):
    acc = refs[2 * n_in][...]
    for a, w in zip(refs[:n_in], refs[n_in:2 * n_in]):
        acc = acc + jnp.dot(a[...], w[...], preferred_element_type=F32)
    refs[-1][...] = acc


def _matmul_residual(a_list, w_list, res, cfg):
    n, d = res.shape
    tm = min(cfg.tm, n)
    n_in = len(a_list)
    in_specs = ([pl.BlockSpec((tm, a.shape[1]), lambda i: (i, 0)) for a in a_list]
                + [pl.BlockSpec(w.shape, lambda i: (0, 0)) for w in w_list]
                + [pl.BlockSpec((tm, d), lambda i: (i, 0))])
    est = 2 * (sum(tm * a.shape[1] * 2 for a in a_list) + sum(w.size * 2 for w in w_list)
               + 2 * tm * d * 4) + 2 * tm * d * 4
    return pl.pallas_call(
        functools.partial(_mm_res_body, n_in=n_in),
        out_shape=jax.ShapeDtypeStruct((n, d), F32),
        grid=(n // tm,),
        in_specs=in_specs,
        out_specs=pl.BlockSpec((tm, d), lambda i: (i, 0)),
        compiler_params=pltpu.CompilerParams(dimension_semantics=("parallel",),
                                             vmem_limit_bytes=_vmem_limit(est)),
        name="matmul_residual",
    )(*a_list, *w_list, res)


def _ret_tables(ret_decay, chunk, width):
    lg = -jax.nn.softplus(-ret_decay.astype(F32))
    pos = jnp.arange(chunk, dtype=F32)
    diff = pos[:, None] - pos[None, :]
    d_f = jnp.where(diff >= 0, jnp.exp(lg[0][:, None, None] * jnp.maximum(diff, 0.0)[None]), 0.0)
    d_b = jnp.where(diff < 0, jnp.exp(lg[1][:, None, None] * jnp.maximum(-diff, 0.0)[None]), 0.0)
    cols = [jnp.exp(lg[0][:, None] * (chunk - 1.0 - pos)[None]),
            jnp.exp(lg[1][:, None] * pos[None]),
            jnp.exp(lg[0][:, None] * (pos + 1.0)[None]),
            jnp.exp(lg[1][:, None] * (chunk - pos)[None]),
            jnp.broadcast_to(jnp.exp(lg[0] * chunk)[:, None], (lg.shape[1], chunk)),
            jnp.broadcast_to(jnp.exp(lg[1] * chunk)[:, None], (lg.shape[1], chunk))]
    tab = jnp.stack(cols, axis=1)
    return d_f + d_b, jnp.broadcast_to(tab[..., None], tab.shape + (width,))


def _rope_tables(t, half, base):
    inv = base ** (-jnp.arange(half, dtype=F32) / half)
    ang = jnp.arange(t, dtype=F32)[:, None] * inv[None, :]
    cos, sin = jnp.cos(ang), jnp.sin(ang)
    return jnp.concatenate([cos, cos], axis=1), jnp.concatenate([-sin, sin], axis=1)


def _ret_body(q_ref, k_ref, v_ref, g_ref, cos_ref, sin_ref, d_ref, tab_ref, gn_ref, o_ref,
              qs, ks, sb, *, chunk, nc, kscale, eps):
    dk = q_ref.shape[1]
    half = dk // 2
    cos, sin = cos_ref[...], sin_ref[...]
    q = q_ref[...]
    qs[...] = q * cos + pltpu.roll(q, half, 1) * sin
    k = k_ref[...]
    ks[...] = (k * cos + pltpu.roll(k, half, 1) * sin) * kscale
    contract0 = (((0,), (0,)), ((), ()))
    contract1 = (((1,), (1,)), ((), ()))

    k_f, k_b, q_f, q_b = tab_ref[0, 0], tab_ref[0, 1], tab_ref[0, 2], tab_ref[0, 3]
    cd_f, cd_b = tab_ref[0, 4], tab_ref[0, 5]

    def bwd(i, state):
        n = nc - 1 - i
        r0 = pl.multiple_of(n * chunk, chunk)
        sb[n] = state
        kc = ks[pl.ds(r0, chunk), :]
        vc = v_ref[pl.ds(r0, chunk), :].astype(BF16)
        kv = lax.dot_general((kc * k_b).astype(BF16), vc, contract0, preferred_element_type=F32)
        return state * cd_b + kv

    lax.fori_loop(0, nc, bwd, jnp.zeros((dk, v_ref.shape[1]), F32))

    def fwd(n, state):
        r0 = pl.multiple_of(n * chunk, chunk)
        qc = qs[pl.ds(r0, chunk), :]
        kc = ks[pl.ds(r0, chunk), :]
        vc = v_ref[pl.ds(r0, chunk), :].astype(BF16)
        s = lax.dot_general(qc.astype(BF16), kc.astype(BF16), contract1, preferred_element_type=F32)
        o = jnp.dot((s * d_ref[0]).astype(BF16), vc, preferred_element_type=F32)
        o = o + jnp.dot((qc * q_f).astype(BF16), state.astype(BF16), preferred_element_type=F32)
        o = o + jnp.dot((qc * q_b).astype(BF16), sb[n].astype(BF16), preferred_element_type=F32)
        oc = o - jnp.mean(o, axis=-1, keepdims=True)
        o = oc * lax.rsqrt(jnp.mean(oc * oc, axis=-1, keepdims=True) + eps)
        g = g_ref[pl.ds(r0, chunk), :]
        o_ref[pl.ds(r0, chunk), :] = (o * gn_ref[...] * (g * _sigmoid(g))).astype(o_ref.dtype)
        kv = lax.dot_general((kc * k_f).astype(BF16), vc, contract0, preferred_element_type=F32)
        return state * cd_f + kv

    lax.fori_loop(0, nc, fwd, jnp.zeros((dk, v_ref.shape[1]), F32))


def _retention(zq, ret_decay, ret_gn, b, t, cfg):
    h, dk, c = cfg.ret_heads, cfg.ret_dk, cfg.ret_chunk
    assert t % c == 0 and zq.shape[1] == 4 * h * dk
    nc = t // c
    dmat, tab = _ret_tables(ret_decay, c, dk)
    cos2, sin2 = _rope_tables(t, dk // 2, cfg.rope_base)
    blk = lambda off: pl.BlockSpec((t, dk), lambda bi, hi, off=off: (bi, off + hi))
    est = 2 * (4 * t * dk * 4 + 2 * t * dk * 4 + c * c * 4 + 6 * c * dk * 4 + t * dk * 2) \
        + 2 * t * dk * 4 + nc * dk * dk * 4 + 8 * t * dk * 4
    return pl.pallas_call(
        functools.partial(_ret_body, chunk=c, nc=nc, kscale=dk ** -0.5, eps=cfg.eps),
        out_shape=jax.ShapeDtypeStruct((b * t, h * dk), BF16),
        grid=(b, h),
        in_specs=[blk(0), blk(h), blk(2 * h), blk(3 * h),
                  pl.BlockSpec((t, dk), lambda bi, hi: (0, 0)),
                  pl.BlockSpec((t, dk), lambda bi, hi: (0, 0)),
                  pl.BlockSpec((1, c, c), lambda bi, hi: (hi, 0, 0)),
                  pl.BlockSpec((1, 6, c, dk), lambda bi, hi: (hi, 0, 0, 0)),
                  pl.BlockSpec((1, dk), lambda bi, hi: (0, hi))],
        out_specs=pl.BlockSpec((t, dk), lambda bi, hi: (bi, hi)),
        scratch_shapes=[pltpu.VMEM((t, dk), F32), pltpu.VMEM((t, dk), F32),
                        pltpu.VMEM((nc, dk, dk), F32)],
        compiler_params=pltpu.CompilerParams(dimension_semantics=("parallel", "parallel"),
                                             vmem_limit_bytes=_vmem_limit(est)),
        name="retention",
    )(zq, zq, zq, zq, cos2, sin2, dmat, tab, ret_gn.reshape(1, h * dk).astype(F32))


def _s5_tables(lam_re, lam_im, log_dt, b_re, b_im, c_re, c_im, d_skip, glu_w, glu_b, L, nsteps):
    lam = lax.complex(lam_re.astype(F32), lam_im.astype(F32))
    ldt = lam * jnp.exp(log_dt.astype(F32))[..., None]
    lam_bar = jnp.exp(ldt)
    bmat = lax.complex(b_re.astype(F32), b_im.astype(F32))
    b_bar = ((lam_bar - 1.0) / lam)[..., None] * bmat[None]
    cmat = lax.complex(c_re.astype(F32), c_im.astype(F32))
    g, p, cg = bmat.shape
    tau = jnp.arange(L + 1, dtype=F32)
    pw = jnp.exp(ldt[:, :, None, :] * tau[None, None, :, None])
    kern = jnp.real(jnp.einsum('dgop,dgtp,dgpi->dgtoi', cmat, pw[:, :, :L], b_bar))
    li = np.arange(L)
    lag = li[None, :] - li[:, None]
    k_f = jnp.where((lag >= 0)[None, :, :, None, None], kern[0][:, np.clip(lag, 0, L - 1)], 0.0)
    k_b = jnp.where((lag <= 0)[None, :, :, None, None], kern[1][:, np.clip(-lag, 0, L - 1)], 0.0)
    m = (k_f + k_b).transpose(0, 1, 4, 2, 3).reshape(g, L * cg, L * cg)

    def cat(z):
        return jnp.concatenate([jnp.real(z), jnp.imag(z)], axis=-1)

    inc_f = cat(jnp.einsum('glp,gpi->glip', pw[0][:, L - 1 - li], b_bar[0])).reshape(g, L * cg, 2 * p)
    inc_b = cat(jnp.einsum('glp,gpi->glip', pw[1][:, li], b_bar[1])).reshape(g, L * cg, 2 * p)

    def out_mat(z):
        return jnp.concatenate([jnp.real(z), -jnp.imag(z)], axis=1).reshape(g, 2 * p, L * cg)

    out_f = out_mat(jnp.einsum('gop,glp->gplo', cmat[0], pw[0][:, li + 1]))
    out_b = out_mat(jnp.einsum('gop,glp->gplo', cmat[1], pw[1][:, L - li]))
    eye = jnp.eye(L, dtype=F32)
    glu = jnp.einsum('lm,gce->glcme', eye, glu_w.astype(F32)).reshape(g, L * cg, L * cg)
    vecs = jnp.stack([jnp.tile(d_skip.astype(F32), (1, L)), jnp.tile(glu_b.astype(F32), (1, L))], axis=1)
    steps = (2.0 ** jnp.arange(nsteps, dtype=F32)) * L
    a = jnp.exp(ldt[:, :, None, :] * steps[None, None, :, None])
    scan = jnp.stack([jnp.concatenate([jnp.real(a), jnp.real(a)], -1),
                      jnp.concatenate([-jnp.imag(a), jnp.imag(a)], -1)], axis=3)
    scan = scan.transpose(1, 0, 2, 3, 4).reshape(g, 2 * nsteps * 2, 2 * p)
    return (m.astype(BF16), inc_f.astype(BF16), inc_b.astype(BF16), out_f.astype(BF16),
            out_b.astype(BF16), glu.astype(BF16), vecs, scan)


def _s5_body(u_ref, m_ref, incf_ref, incb_ref, outf_ref, outb_ref, glu_ref, vec_ref, scan_ref, o_ref,
             *, ncs, nsteps):
    u = u_ref[0]
    rows = u.shape[0]
    ub = u.astype(BF16)
    y = jnp.dot(ub, m_ref[0], preferred_element_type=F32)
    xf = jnp.dot(ub, incf_ref[0], preferred_element_type=F32)
    xb = jnp.dot(ub, incb_ref[0], preferred_element_type=F32)
    p2 = xf.shape[1]
    cidx = lax.rem(lax.broadcasted_iota(I32, (rows, 1), 0), ncs)
    for kk in range(nsteps):
        s = 1 << kk
        a_f, b_f = scan_ref[0, pl.ds(2 * kk, 1), :], scan_ref[0, pl.ds(2 * kk + 1, 1), :]
        a_b = scan_ref[0, pl.ds(2 * nsteps + 2 * kk, 1), :]
        b_b = scan_ref[0, pl.ds(2 * nsteps + 2 * kk + 1, 1), :]
        pf = pltpu.roll(xf, s, 0)
        xf = xf + jnp.where(cidx >= s, a_f * pf + b_f * pltpu.roll(pf, p2 // 2, 1), 0.0)
        pb = pltpu.roll(xb, rows - s, 0)
        xb = xb + jnp.where(cidx < ncs - s, a_b * pb + b_b * pltpu.roll(pb, p2 // 2, 1), 0.0)
    x_prev = jnp.where(cidx >= 1, pltpu.roll(xf, 1, 0), 0.0)
    x_next = jnp.where(cidx < ncs - 1, pltpu.roll(xb, rows - 1, 0), 0.0)
    y = y + jnp.dot(x_prev.astype(BF16), outf_ref[0], preferred_element_type=F32)
    y = y + jnp.dot(x_next.astype(BF16), outb_ref[0], preferred_element_type=F32)
    y = y + u * vec_ref[0, pl.ds(0, 1), :]
    yg = _gelu(y)
    z = jnp.dot(yg.astype(BF16), glu_ref[0], preferred_element_type=F32) + vec_ref[0, pl.ds(1, 1), :]
    o_ref[0] = (yg * _sigmoid(z)).astype(o_ref.dtype)


def _s5(u, params, b, t, cfg):
    lam_re, lam_im, log_dt, b_re, b_im, c_re, c_im, d_skip, glu_w, glu_b = params
    L, cg = cfg.s5_chunk, cfg.s5_group
    n, ch = u.shape
    g = ch // cg
    assert t % L == 0
    ncs = t // L
    nsteps = max(1, (ncs - 1).bit_length())
    tabs = _s5_tables(lam_re, lam_im, log_dt, b_re, b_im, c_re, c_im, d_skip, glu_w, glu_b, L, nsteps)
    uc = u.reshape(n // L, L, g, cg).transpose(2, 0, 1, 3).reshape(g, n // L, L * cg)
    seqs = max(1, min(b, 1024 // ncs))
    while b % seqs:
        seqs -= 1
    rows = seqs * ncs
    w = L * cg
    p2 = 2 * cfg.s5_state
    per_g = lambda shape: pl.BlockSpec((1,) + shape, lambda gi, ri: (gi, 0, 0))
    est = 2 * (rows * w * 4 + rows * w * 2) + 2 * 2 * (3 * w * w + 4 * w * p2) + 12 * rows * w * 4
    out = pl.pallas_call(
        functools.partial(_s5_body, ncs=ncs, nsteps=nsteps),
        out_shape=jax.ShapeDtypeStruct((g, n // L, w), BF16),
        grid=(g, (n // L) // rows),
        in_specs=[pl.BlockSpec((1, rows, w), lambda gi, ri: (gi, ri, 0)),
                  per_g((w, w)), per_g((w, p2)), per_g((w, p2)), per_g((p2, w)), per_g((p2, w)),
                  per_g((w, w)), per_g((2, w)), per_g((4 * nsteps, p2))],
        out_specs=pl.BlockSpec((1, rows, w), lambda gi, ri: (gi, ri, 0)),
        compiler_params=pltpu.CompilerParams(dimension_semantics=("parallel", "parallel"),
                                             vmem_limit_bytes=_vmem_limit(est)),
        name="s5",
    )(uc, *tabs)
    return out.reshape(g, n // L, L, cg).transpose(1, 2, 0, 3).reshape(n, ch)


def _na_bias_tables(rpb, cfg, rows):
    kh, kw, w = cfg.na_kh, cfg.na_kw, cfg.grid_w
    j = np.arange(w)
    c_start = np.clip(j - kw // 2, 0, w - kw)
    c = np.arange(w)
    inside = (c[None, :] >= c_start[:, None]) & (c[None, :] < c_start[:, None] + kw)
    dc = np.clip(c[None, :] - j[:, None] + kw - 1, 0, 2 * kw - 2)
    delta = np.arange(kh)
    o = np.arange(kh)
    dr = o[None, :] - delta[:, None] + kh - 1
    bias = rpb.astype(F32)[:, dr[:, None, :, None], dc[None, :, None, :]]
    bias = jnp.where(inside[None, None, :, None, :], bias, NEG_BIG)
    return bias.reshape(rpb.shape[0], kh, w, kh * w)


def _na_body(q_ref, k_ref, v_ref, qg_ref, kg_ref, tab_ref, o_ref, qn, kn, vb, *, w, rows, kh, dh, eps):
    lo = lax.broadcasted_iota(I32, (1, 2 * dh), 1) < dh

    def head_norm(x, gain):
        x2 = x * x
        s_lo = jnp.sum(jnp.where(lo, x2, 0.0), axis=-1, keepdims=True)
        s_hi = jnp.sum(jnp.where(lo, 0.0, x2), axis=-1, keepdims=True)
        ms = jnp.where(lo, s_lo, s_hi) * (1.0 / dh)
        return x * lax.rsqrt(ms + eps) * gain

    qn[...] = (head_norm(q_ref[...], qg_ref[...]) * dh ** -0.5).astype(BF16)
    kn[...] = head_norm(k_ref[...], kg_ref[...]).astype(BF16)
    vb[...] = v_ref[...].astype(BF16)
    contract1 = (((1,), (1,)), ((), ()))

    def row(r, carry):
        rs = jnp.clip(r - kh // 2, 0, rows - kh)
        delta = r - rs
        q0 = pl.multiple_of(r * w, w)
        k0 = pl.multiple_of(rs * w, w)
        qr = qn[pl.ds(q0, w), :]
        kb = kn[pl.ds(k0, kh * w), :]
        vv = vb[pl.ds(k0, kh * w), :]
        outs = []
        for hh in range(2):
            sl = slice(hh * dh, (hh + 1) * dh)
            s = lax.dot_general(qr[:, sl], kb[:, sl], contract1, preferred_element_type=F32)
            s = s + tab_ref[hh, delta]
            p = jnp.exp(s - jnp.max(s, axis=-1, keepdims=True))
            p = p / jnp.sum(p, axis=-1, keepdims=True)
            outs.append(jnp.dot(p.astype(BF16), vv[:, sl], preferred_element_type=F32))
        o_ref[pl.ds(q0, w), :] = jnp.concatenate(outs, axis=1).astype(o_ref.dtype)
        return carry

    lax.fori_loop(0, rows, row, 0)


def _neighbourhood_attention(z, q_gain, k_gain, rpb, b, t, cfg):
    h, w, kh = cfg.na_heads, cfg.grid_w, cfg.na_kh
    dh = z.shape[1] // (3 * h)
    rows = t // w
    assert t % w == 0 and rows >= kh and h % 2 == 0 and 2 * dh == V7X_LANES
    tab = _na_bias_tables(rpb, cfg, rows)
    hp = h // 2
    blk = lambda off: pl.BlockSpec((t, 2 * dh), lambda bi, pi, off=off: (bi, off + pi))
    gain = pl.BlockSpec((1, 2 * dh), lambda bi, pi: (0, pi))
    est = 2 * (3 * t * 2 * dh * 4 + 2 * kh * w * kh * w * 4 + t * 2 * dh * 2) + 3 * t * 2 * dh * 2 \
        + 6 * t * 2 * dh * 4
    return pl.pallas_call(
        functools.partial(_na_body, w=w, rows=rows, kh=kh, dh=dh, eps=cfg.eps),
        out_shape=jax.ShapeDtypeStruct((b * t, h * dh), BF16),
        grid=(b, hp),
        in_specs=[blk(0), blk(hp), blk(2 * hp), gain, gain,
                  pl.BlockSpec((2, kh, w, kh * w), lambda bi, pi: (pi, 0, 0, 0))],
        out_specs=pl.BlockSpec((t, 2 * dh), lambda bi, pi: (bi, pi)),
        scratch_shapes=[pltpu.VMEM((t, 2 * dh), BF16)] * 3,
        compiler_params=pltpu.CompilerParams(dimension_semantics=("parallel", "parallel"),
                                             vmem_limit_bytes=_vmem_limit(est)),
        name="neighbourhood_attention",
    )(z, z, z, q_gain.reshape(1, h * dh).astype(F32), k_gain.reshape(1, h * dh).astype(F32), tab)


def _peer_cells(topk):
    return [(k1, k2) for k1 in range(topk) for k2 in range(topk) if (k1 + 1) * (k2 + 1) <= topk]


def _top_rows(s, k, iota):
    nrows = s.shape[0]
    vals, idxs = [], []
    for _ in range(k):
        m = jnp.max(s, axis=0, keepdims=True)
        idx = jnp.min(jnp.where(s == m, iota, nrows), axis=0, keepdims=True)
        vals.append(m)
        idxs.append(idx)
        s = jnp.where(iota == idx, -jnp.inf, s)
    return vals, idxs


def _route_body(qp_ref, keys_ref, a_ref, b_ref, g_ref, a_t, b_t, g_t, c_s, c_a, c_b,
                *, heads, nkeys, topk, cells):
    tm = qp_ref.shape[0]
    dq = keys_ref.shape[2]
    iota_k = lax.broadcasted_iota(I32, (nkeys, tm), 0)
    ncell = len(cells)
    crow = c_s.shape[0]
    iota_c = lax.broadcasted_iota(I32, (crow, tm), 0)
    contract1 = (((1,), (1,)), ((), ()))
    c_s[pl.ds(ncell, crow - ncell), :] = jnp.full((crow - ncell, tm), -jnp.inf, F32)
    c_a[pl.ds(ncell, crow - ncell), :] = jnp.zeros((crow - ncell, tm), I32)
    c_b[pl.ds(ncell, crow - ncell), :] = jnp.zeros((crow - ncell, tm), I32)
    for h in range(heads):
        tops = []
        for p in range(2):
            c0 = (2 * h + p) * dq
            q = qp_ref[:, c0:c0 + dq].astype(BF16)
            s = lax.dot_general(keys_ref[p], q, contract1, preferred_element_type=F32)
            tops.append(_top_rows(s, topk, iota_k))
        (v1, i1), (v2, i2) = tops
        for ci, (k1, k2) in enumerate(cells):
            c_s[pl.ds(ci, 1), :] = v1[k1] + v2[k2]
            c_a[pl.ds(ci, 1), :] = i1[k1]
            c_b[pl.ds(ci, 1), :] = i2[k2]
        cs, ca, cb = c_s[...], c_a[...], c_b[...]
        best = []
        for kk in range(topk):
            m = jnp.max(cs, axis=0, keepdims=True)
            pos = jnp.min(jnp.where(cs == m, iota_c, crow), axis=0, keepdims=True)
            hit = iota_c == pos
            j = h * topk + kk
            a_t[pl.ds(j, 1), :] = jnp.sum(jnp.where(hit, ca, 0), axis=0, keepdims=True)
            b_t[pl.ds(j, 1), :] = jnp.sum(jnp.where(hit, cb, 0), axis=0, keepdims=True)
            best.append(m)
            cs = jnp.where(hit, -jnp.inf, cs)
        e = [jnp.exp(v - best[0]) for v in best]
        denom = e[0]
        for v in e[1:]:
            denom = denom + v
        for kk in range(topk):
            g_t[pl.ds(h * topk + kk, 1), :] = e[kk] / denom
    a_ref[...] = a_t[...].T
    b_ref[...] = b_t[...].T
    g_ref[...] = g_t[...].T


def _peer_route(qp, sub_keys, cfg):
    n = qp.shape[0]
    heads, nkeys, topk = cfg.peer_heads, cfg.peer_nkeys, cfg.peer_topk
    dq = sub_keys.shape[2]
    nj = heads * topk
    tm = min(cfg.topk_tm, n)
    assert nj == V7X_LANES and qp.shape[1] == heads * 2 * dq
    cells = _peer_cells(topk)
    crow = -(-(len(cells) + 1) // V7X_SUBLANES) * V7X_SUBLANES
    outs = pl.pallas_call(
        functools.partial(_route_body, heads=heads, nkeys=nkeys, topk=topk, cells=tuple(cells)),
        out_shape=[jax.ShapeDtypeStruct((n, nj), I32), jax.ShapeDtypeStruct((n, nj), I32),
                   jax.ShapeDtypeStruct((n, nj), F32)],
        grid=(n // tm,),
        in_specs=[pl.BlockSpec((tm, qp.shape[1]), lambda i: (i, 0)),
                  pl.BlockSpec(sub_keys.shape, lambda i: (0, 0, 0))],
        out_specs=[pl.BlockSpec((tm, nj), lambda i: (i, 0))] * 3,
        scratch_shapes=[pltpu.VMEM((nj, tm), I32), pltpu.VMEM((nj, tm), I32), pltpu.VMEM((nj, tm), F32),
                        pltpu.VMEM((crow, tm), F32), pltpu.VMEM((crow, tm), I32), pltpu.VMEM((crow, tm), I32)],
        compiler_params=pltpu.CompilerParams(dimension_semantics=("parallel",),
                                             vmem_limit_bytes=_vmem_limit(0)),
        name="peer_route",
    )(qp, sub_keys.astype(BF16))
    return outs


W_PITCH_PAD = 8


def _expert_body(hn_ref, a_ref, b_ref, g_ref, ut_ref, v_ref, x_ref, o_ref, wmap, acc, *, nk, pitch):
    j = pl.program_id(1)
    tm = hn_ref.shape[0]
    te = ut_ref.shape[1]
    nblk = te // nk
    contract1 = (((1,), (1,)), ((), ()))

    @pl.when(j == 0)
    def _():
        acc[...] = jnp.zeros_like(acc)
        iota = lax.broadcasted_iota(I32, (nk, a_ref.shape[1]), 0)

        def build(n, carry):
            ar = a_ref[pl.ds(n, 1), :]
            br = b_ref[pl.ds(n, 1), :]
            gr = g_ref[pl.ds(n, 1), :]
            one_a = jnp.where(iota == ar, 1.0, 0.0).astype(BF16)
            gate_b = jnp.where(iota == br, gr, 0.0).astype(BF16)
            wn = lax.dot_general(one_a, gate_b, contract1, preferred_element_type=F32)
            wmap[pl.ds(pl.multiple_of(n * pitch, V7X_SUBLANES), nk), :] = wn
            return carry

        lax.fori_loop(0, tm, build, 0)

    act = _gelu(jnp.dot(hn_ref[...], ut_ref[...], preferred_element_type=F32))
    parts = []
    for i in range(nblk):
        wblk = wmap[pl.ds(j * nblk + i, tm, stride=pitch), :]
        parts.append((act[:, i * nk:(i + 1) * nk] * wblk).astype(BF16))
    pmat = jnp.concatenate(parts, axis=1)
    acc[...] += jnp.dot(pmat, v_ref[...], preferred_element_type=F32)

    @pl.when(j == pl.num_programs(1) - 1)
    def _():
        o_ref[...] = x_ref[...] + acc[...]


def _peer_experts(hn, a, b, g, ut, v, x2, cfg):
    n, d = x2.shape
    nexp = ut.shape[1]
    nk = cfg.peer_nkeys
    tm = min(cfg.peer_tm, n)
    te = min(cfg.peer_te, nexp)
    assert nk == V7X_LANES and nexp == nk * nk and te % nk == 0 and nexp % te == 0
    pitch = nk + W_PITCH_PAD
    nj = a.shape[1]
    est = 2 * (tm * d * 2 + 3 * tm * nj * 4 + 2 * d * te * 2 + 2 * tm * d * 4) \
        + tm * pitch * nk * 4 + tm * d * 4 + 4 * tm * te * 4
    tile = lambda width: pl.BlockSpec((tm, width), lambda i, j: (i, 0))
    return pl.pallas_call(
        functools.partial(_expert_body, nk=nk, pitch=pitch),
        out_shape=jax.ShapeDtypeStruct((n, d), F32),
        grid=(n // tm, nexp // te),
        in_specs=[tile(d), tile(nj), tile(nj), tile(nj),
                  pl.BlockSpec((d, te), lambda i, j: (0, j)),
                  pl.BlockSpec((te, d), lambda i, j: (j, 0)),
                  tile(d)],
        out_specs=tile(d),
        scratch_shapes=[pltpu.VMEM((tm * pitch, nk), F32), pltpu.VMEM((tm, d), F32)],
        compiler_params=pltpu.CompilerParams(dimension_semantics=("parallel", "arbitrary"),
                                             vmem_limit_bytes=_vmem_limit(est)),
        name="peer_experts",
    )(hn, a, b, g, ut, v, x2)


def _peer(x2, gain, w_q_bf16, sub_keys, ut, v, cfg):
    nq = w_q_bf16.shape[1]
    qp, hn = _norm_matmul(x2, gain, w_q_bf16, [(0, nq)], [F32], cfg, emit_hn=True)
    a, b, g = _peer_route(qp, sub_keys, cfg)
    return _peer_experts(hn, a, b, g, ut, v, x2, cfg)


def _trunk(x, p, cfg):
    b, t, d = x.shape
    x2 = x.reshape(b * t, d)
    depth = p["norm_mix"].shape[0]
    for l in range(depth):
        i = l // 2
        if l % 2 == 0:
            rw = 4 * cfg.ret_heads * cfg.ret_dk
            w_in = p["ab_w_in"][i]
            zq, u = _norm_matmul(x2, p["norm_mix"][l], w_in, [(0, rw), (rw, w_in.shape[1])], [F32, F32], cfg)
            ret = _retention(zq, p["ab_ret_decay"][i], p["ab_ret_gn"][i], b, t, cfg)
            ssm = _s5(u, tuple(p[k][i] for k in ("ab_s5_lam_re", "ab_s5_lam_im", "ab_s5_log_dt", "ab_s5_b_re",
                                                  "ab_s5_b_im", "ab_s5_c_re", "ab_s5_c_im", "ab_s5_d",
                                                  "ab_s5_glu_w", "ab_s5_glu_b")), b, t, cfg)
            w_out = p["ab_w_out"][i]
            nr = ret.shape[1]
            x2 = _matmul_residual([ret, ssm], [w_out[:nr], w_out[nr:]], x2, cfg)
        else:
            w_qkv = p["na_w_qkv"][i]
            (z,) = _norm_matmul(x2, p["norm_mix"][l], w_qkv, [(0, w_qkv.shape[1])], [F32], cfg)
            att = _neighbourhood_attention(z, p["na_q_gain"][i], p["na_k_gain"][i], p["na_rpb"][i], b, t, cfg)
            x2 = _matmul_residual([att], [p["na_w_o"][i]], x2, cfg)
        x2 = _peer(x2, p["norm_ffn"][l], p["peer_w_q"][l], p["peer_sub_keys"][l], p["peer_ut"][l],
                   p["peer_v"][l], cfg)
    return x2.reshape(b, t, d)


def _prepare(params):
    p = dict(params)
    for k in ("ab_w_in", "ab_w_out", "na_w_qkv", "na_w_o", "peer_w_q", "peer_v"):
        p[k] = params[k].astype(BF16)
    p["peer_ut"] = jnp.swapaxes(params["peer_u"], 1, 2).astype(BF16)
    return p


def _forward(x_prompt, x_sample, params, cfg=Cfg()):
    p = _prepare(params)
    return _trunk(x_prompt, p, cfg), _trunk(x_sample, p, cfg)


def kernel(x_prompt, x_sample, norm_mix, norm_ffn, ab_w_in, ab_ret_decay, ab_ret_gn, ab_s5_lam_re, ab_s5_lam_im, ab_s5_log_dt, ab_s5_b_re, ab_s5_b_im, ab_s5_c_re, ab_s5_c_im, ab_s5_d, ab_s5_glu_w, ab_s5_glu_b, ab_w_out, na_w_qkv, na_q_gain, na_k_gain, na_rpb, na_w_o, peer_w_q, peer_sub_keys, peer_u, peer_v):
    params = dict(norm_mix=norm_mix, norm_ffn=norm_ffn, ab_w_in=ab_w_in, ab_ret_decay=ab_ret_decay,
                  ab_ret_gn=ab_ret_gn, ab_s5_lam_re=ab_s5_lam_re, ab_s5_lam_im=ab_s5_lam_im,
                  ab_s5_log_dt=ab_s5_log_dt, ab_s5_b_re=ab_s5_b_re, ab_s5_b_im=ab_s5_b_im,
                  ab_s5_c_re=ab_s5_c_re, ab_s5_c_im=ab_s5_c_im, ab_s5_d=ab_s5_d, ab_s5_glu_w=ab_s5_glu_w,
                  ab_s5_glu_b=ab_s5_glu_b, ab_w_out=ab_w_out, na_w_qkv=na_w_qkv, na_q_gain=na_q_gain,
                  na_k_gain=na_k_gain, na_rpb=na_rpb, na_w_o=na_w_o, peer_w_q=peer_w_q,
                  peer_sub_keys=peer_sub_keys, peer_u=peer_u, peer_v=peer_v)
    return _forward(x_prompt, x_sample, params)
```

```python
import functools
import math
from typing import NamedTuple

import numpy as np
import jax
import jax.numpy as jnp
from jax import lax
from jax.experimental import pallas as pl
from jax.experimental.pallas import tpu as pltpu

F32 = jnp.float32
BF16 = jnp.bfloat16
I32 = jnp.int32

V7X_LANES = 128
V7X_SUBLANES = 8
V7X_VMEM_BYTES = 64 * 2**20
VMEM_LIMIT_CAP = V7X_VMEM_BYTES - 8 * 2**20

NEG_BIG = -1e30


class Cfg(NamedTuple):
    eps: float = 1e-6
    grid_w: int = 64
    ret_heads: int = 4
    ret_dk: int = 128
    ret_chunk: int = 128
    rope_base: float = 10000.0
    s5_group: int = 16
    s5_state: int = 64
    s5_chunk: int = 16
    na_heads: int = 16
    na_kh: int = 8
    na_kw: int = 16
    peer_heads: int = 8
    peer_nkeys: int = 128
    peer_topk: int = 16
    tm: int = 512
    peer_tm: int = 512
    peer_unroll: int = 8
    na_unroll: int = 4
    peer_te: int = 1024
    topk_tm: int = 256


def _vmem_limit(nbytes):
    return int(min(VMEM_LIMIT_CAP, max(32 * 2**20, nbytes)))


def _gelu(x):
    return 0.5 * x * (1.0 + lax.erf(x * (1.0 / math.sqrt(2.0))))


def _sigmoid(x):
    return 1.0 / (1.0 + jnp.exp(-x))


def _norm_mm_body(x_ref, g_ref, w_ref, *out_refs, splits, emit_hn, eps):
    x = x_ref[...]
    y = x * lax.rsqrt(jnp.mean(x * x, axis=-1, keepdims=True) + eps) * g_ref[...]
    yb = y.astype(BF16)
    z = jnp.dot(yb, w_ref[...], preferred_element_type=F32)
    for r, (s, e) in zip(out_refs, splits):
        r[...] = z[:, s:e].astype(r.dtype)
    if emit_hn:
        out_refs[len(splits)][...] = yb


def _norm_matmul(x2, gain, w_bf16, splits, dtypes, cfg, emit_hn=False):
    n, d = x2.shape
    nout = w_bf16.shape[1]
    tm = min(cfg.tm, n)
    out_shape = [jax.ShapeDtypeStruct((n, e - s), dt) for (s, e), dt in zip(splits, dtypes)]
    out_specs = [pl.BlockSpec((tm, e - s), lambda i: (i, 0)) for (s, e) in splits]
    if emit_hn:
        out_shape.append(jax.ShapeDtypeStruct((n, d), BF16))
        out_specs.append(pl.BlockSpec((tm, d), lambda i: (i, 0)))
    est = 2 * (tm * d * 4 + d * nout * 2 + tm * nout * 4 + tm * d * 2) + 2 * tm * nout * 4
    return pl.pallas_call(
        functools.partial(_norm_mm_body, splits=tuple(splits), emit_hn=emit_hn, eps=cfg.eps),
        out_shape=out_shape,
        grid=(n // tm,),
        in_specs=[pl.BlockSpec((tm, d), lambda i: (i, 0)),
                  pl.BlockSpec((1, d), lambda i: (0, 0)),
                  pl.BlockSpec((d, nout), lambda i: (0, 0))],
        out_specs=out_specs,
        compiler_params=pltpu.CompilerParams(dimension_semantics=("parallel",),
                                             vmem_limit_bytes=_vmem_limit(est)),
        name="norm_matmul",
    )(x2, gain.reshape(1, d).astype(F32), w_bf16)


def _mm_res_body(*refs, n_in):
    acc = refs[2 * n_in][...]
    for a, w in zip(refs[:n_in], refs[n_in:2 * n_in]):
        acc = acc + jnp.dot(a[...], w[...], preferred_element_type=F32)
    refs[-1][...] = acc


def _matmul_residual(a_list, w_list, res, cfg):
    n, d = res.shape
    tm = min(cfg.tm, n)
    n_in = len(a_list)
    in_specs = ([pl.BlockSpec((tm, a.shape[1]), lambda i: (i, 0)) for a in a_list]
                + [pl.BlockSpec(w.shape, lambda i: (0, 0)) for w in w_list]
                + [pl.BlockSpec((tm, d), lambda i: (i, 0))])
    est = 2 * (sum(tm * a.shape[1] * 2 for a in a_list) + sum(w.size * 2 for w in w_list)
               + 2 * tm * d * 4) + 2 * tm * d * 4
    return pl.pallas_call(
        functools.partial(_mm_res_body, n_in=n_in),
        out_shape=jax.ShapeDtypeStruct((n, d), F32),
        grid=(n // tm,),
        in_specs=in_specs,
        out_specs=pl.BlockSpec((tm, d), lambda i: (i, 0)),
        compiler_params=pltpu.CompilerParams(dimension_semantics=("parallel",),
                                             vmem_limit_bytes=_vmem_limit(est)),
        name="matmul_residual",
    )(*a_list, *w_list, res)


def _ret_tables(ret_decay, chunk, width):
    lg = -jax.nn.softplus(-ret_decay.astype(F32))
    pos = jnp.arange(chunk, dtype=F32)
    diff = pos[:, None] - pos[None, :]
    d_f = jnp.where(diff >= 0, jnp.exp(lg[0][:, None, None] * jnp.maximum(diff, 0.0)[None]), 0.0)
    d_b = jnp.where(diff < 0, jnp.exp(lg[1][:, None, None] * jnp.maximum(-diff, 0.0)[None]), 0.0)
    cols = [jnp.exp(lg[0][:, None] * (chunk - 1.0 - pos)[None]),
            jnp.exp(lg[1][:, None] * pos[None]),
            jnp.exp(lg[0][:, None] * (pos + 1.0)[None]),
            jnp.exp(lg[1][:, None] * (chunk - pos)[None]),
            jnp.broadcast_to(jnp.exp(lg[0] * chunk)[:, None], (lg.shape[1], chunk)),
            jnp.broadcast_to(jnp.exp(lg[1] * chunk)[:, None], (lg.shape[1], chunk))]
    tab = jnp.stack(cols, axis=1)
    return d_f + d_b, jnp.broadcast_to(tab[..., None], tab.shape + (width,))


def _rope_tables(t, half, base):
    inv = base ** (-jnp.arange(half, dtype=F32) / half)
    ang = jnp.arange(t, dtype=F32)[:, None] * inv[None, :]
    cos, sin = jnp.cos(ang), jnp.sin(ang)
    return jnp.concatenate([cos, cos], axis=1), jnp.concatenate([-sin, sin], axis=1)


def _ret_body(q_ref, k_ref, v_ref, g_ref, cos_ref, sin_ref, d_ref, tab_ref, gn_ref, o_ref,
              qs, ks, sb, *, chunk, nc, kscale, eps):
    dk = q_ref.shape[1]
    half = dk // 2
    cos, sin = cos_ref[...], sin_ref[...]
    q = q_ref[...]
    qs[...] = q * cos + pltpu.roll(q, half, 1) * sin
    k = k_ref[...]
    ks[...] = (k * cos + pltpu.roll(k, half, 1) * sin) * kscale
    contract0 = (((0,), (0,)), ((), ()))
    contract1 = (((1,), (1,)), ((), ()))

    k_f, k_b, q_f, q_b = tab_ref[0, 0], tab_ref[0, 1], tab_ref[0, 2], tab_ref[0, 3]
    cd_f, cd_b = tab_ref[0, 4], tab_ref[0, 5]

    def bwd(i, state):
        n = nc - 1 - i
        r0 = pl.multiple_of(n * chunk, chunk)
        sb[n] = state
        kc = ks[pl.ds(r0, chunk), :]
        vc = v_ref[pl.ds(r0, chunk), :].astype(BF16)
        kv = lax.dot_general((kc * k_b).astype(BF16), vc, contract0, preferred_element_type=F32)
        return state * cd_b + kv

    lax.fori_loop(0, nc, bwd, jnp.zeros((dk, v_ref.shape[1]), F32))

    def fwd(n, state):
        r0 = pl.multiple_of(n * chunk, chunk)
        qc = qs[pl.ds(r0, chunk), :]
        kc = ks[pl.ds(r0, chunk), :]
        vc = v_ref[pl.ds(r0, chunk), :].astype(BF16)
        s = lax.dot_general(qc.astype(BF16), kc.astype(BF16), contract1, preferred_element_type=F32)
        o = jnp.dot((s * d_ref[0]).astype(BF16), vc, preferred_element_type=F32)
        o = o + jnp.dot((qc * q_f).astype(BF16), state.astype(BF16), preferred_element_type=F32)
        o = o + jnp.dot((qc * q_b).astype(BF16), sb[n].astype(BF16), preferred_element_type=F32)
        oc = o - jnp.mean(o, axis=-1, keepdims=True)
        o = oc * lax.rsqrt(jnp.mean(oc * oc, axis=-1, keepdims=True) + eps)
        g = g_ref[pl.ds(r0, chunk), :]
        o_ref[pl.ds(r0, chunk), :] = (o * gn_ref[...] * (g * _sigmoid(g))).astype(o_ref.dtype)
        kv = lax.dot_general((kc * k_f).astype(BF16), vc, contract0, preferred_element_type=F32)
        return state * cd_f + kv

    lax.fori_loop(0, nc, fwd, jnp.zeros((dk, v_ref.shape[1]), F32))


def _retention(zq, ret_decay, ret_gn, b, t, cfg):
    h, dk, c = cfg.ret_heads, cfg.ret_dk, cfg.ret_chunk
    assert t % c == 0 and zq.shape[1] == 4 * h * dk
    nc = t // c
    dmat, tab = _ret_tables(ret_decay, c, dk)
    cos2, sin2 = _rope_tables(t, dk // 2, cfg.rope_base)
    blk = lambda off: pl.BlockSpec((t, dk), lambda bi, hi, off=off: (bi, off + hi))
    est = 2 * (4 * t * dk * 4 + 2 * t * dk * 4 + c * c * 4 + 6 * c * dk * 4 + t * dk * 2) \
        + 2 * t * dk * 4 + nc * dk * dk * 4 + 8 * t * dk * 4
    return pl.pallas_call(
        functools.partial(_ret_body, chunk=c, nc=nc, kscale=dk ** -0.5, eps=cfg.eps),
        out_shape=jax.ShapeDtypeStruct((b * t, h * dk), BF16),
        grid=(b, h),
        in_specs=[blk(0), blk(h), blk(2 * h), blk(3 * h),
                  pl.BlockSpec((t, dk), lambda bi, hi: (0, 0)),
                  pl.BlockSpec((t, dk), lambda bi, hi: (0, 0)),
                  pl.BlockSpec((1, c, c), lambda bi, hi: (hi, 0, 0)),
                  pl.BlockSpec((1, 6, c, dk), lambda bi, hi: (hi, 0, 0, 0)),
                  pl.BlockSpec((1, dk), lambda bi, hi: (0, hi))],
        out_specs=pl.BlockSpec((t, dk), lambda bi, hi: (bi, hi)),
        scratch_shapes=[pltpu.VMEM((t, dk), F32), pltpu.VMEM((t, dk), F32),
                        pltpu.VMEM((nc, dk, dk), F32)],
        compiler_params=pltpu.CompilerParams(dimension_semantics=("parallel", "parallel"),
                                             vmem_limit_bytes=_vmem_limit(est)),
        name="retention",
    )(zq, zq, zq, zq, cos2, sin2, dmat, tab, ret_gn.reshape(1, h * dk).astype(F32))


def _s5_tables(lam_re, lam_im, log_dt, b_re, b_im, c_re, c_im, d_skip, glu_w, glu_b, L, nsteps):
    lam = lax.complex(lam_re.astype(F32), lam_im.astype(F32))
    ldt = lam * jnp.exp(log_dt.astype(F32))[..., None]
    lam_bar = jnp.exp(ldt)
    bmat = lax.complex(b_re.astype(F32), b_im.astype(F32))
    b_bar = ((lam_bar - 1.0) / lam)[..., None] * bmat[None]
    cmat = lax.complex(c_re.astype(F32), c_im.astype(F32))
    g, p, cg = bmat.shape
    tau = jnp.arange(L + 1, dtype=F32)
    pw = jnp.exp(ldt[:, :, None, :] * tau[None, None, :, None])
    kern = jnp.real(jnp.einsum('dgop,dgtp,dgpi->dgtoi', cmat, pw[:, :, :L], b_bar))
    li = np.arange(L)
    lag = li[None, :] - li[:, None]
    k_f = jnp.where((lag >= 0)[None, :, :, None, None], kern[0][:, np.clip(lag, 0, L - 1)], 0.0)
    k_b = jnp.where((lag <= 0)[None, :, :, None, None], kern[1][:, np.clip(-lag, 0, L - 1)], 0.0)
    m = (k_f + k_b).transpose(0, 1, 4, 2, 3).reshape(g, L * cg, L * cg)

    def cat(z):
        return jnp.concatenate([jnp.real(z), jnp.imag(z)], axis=-1)

    inc_f = cat(jnp.einsum('glp,gpi->glip', pw[0][:, L - 1 - li], b_bar[0])).reshape(g, L * cg, 2 * p)
    inc_b = cat(jnp.einsum('glp,gpi->glip', pw[1][:, li], b_bar[1])).reshape(g, L * cg, 2 * p)

    def out_mat(z):
        return jnp.concatenate([jnp.real(z), -jnp.imag(z)], axis=1).reshape(g, 2 * p, L * cg)

    out_f = out_mat(jnp.einsum('gop,glp->gplo', cmat[0], pw[0][:, li + 1]))
    out_b = out_mat(jnp.einsum('gop,glp->gplo', cmat[1], pw[1][:, L - li]))
    eye = jnp.eye(L, dtype=F32)
    glu = jnp.einsum('lm,gce->glcme', eye, glu_w.astype(F32)).reshape(g, L * cg, L * cg)
    vecs = jnp.stack([jnp.tile(d_skip.astype(F32), (1, L)), jnp.tile(glu_b.astype(F32), (1, L))], axis=1)
    steps = (2.0 ** jnp.arange(nsteps, dtype=F32)) * L
    a = jnp.exp(ldt[:, :, None, :] * steps[None, None, :, None])
    scan = jnp.stack([jnp.concatenate([jnp.real(a), jnp.real(a)], -1),
                      jnp.concatenate([-jnp.imag(a), jnp.imag(a)], -1)], axis=3)
    scan = scan.transpose(1, 0, 2, 3, 4).reshape(g, 2 * nsteps * 2, 2 * p)
    return (m.astype(BF16), inc_f.astype(BF16), inc_b.astype(BF16), out_f.astype(BF16),
            out_b.astype(BF16), glu.astype(BF16), vecs, scan)


def _s5_body(u_ref, m_ref, incf_ref, incb_ref, outf_ref, outb_ref, glu_ref, vec_ref, scan_ref, o_ref,
             *, ncs, nsteps):
    u = u_ref[0]
    rows = u.shape[0]
    ub = u.astype(BF16)
    y = jnp.dot(ub, m_ref[0], preferred_element_type=F32)
    xf = jnp.dot(ub, incf_ref[0], preferred_element_type=F32)
    xb = jnp.dot(ub, incb_ref[0], preferred_element_type=F32)
    p2 = xf.shape[1]
    cidx = lax.rem(lax.broadcasted_iota(I32, (rows, 1), 0), ncs)
    for kk in range(nsteps):
        s = 1 << kk
        a_f, b_f = scan_ref[0, pl.ds(2 * kk, 1), :], scan_ref[0, pl.ds(2 * kk + 1, 1), :]
        a_b = scan_ref[0, pl.ds(2 * nsteps + 2 * kk, 1), :]
        b_b = scan_ref[0, pl.ds(2 * nsteps + 2 * kk + 1, 1), :]
        pf = pltpu.roll(xf, s, 0)
        xf = xf + jnp.where(cidx >= s, a_f * pf + b_f * pltpu.roll(pf, p2 // 2, 1), 0.0)
        pb = pltpu.roll(xb, rows - s, 0)
        xb = xb + jnp.where(cidx < ncs - s, a_b * pb + b_b * pltpu.roll(pb, p2 // 2, 1), 0.0)
    x_prev = jnp.where(cidx >= 1, pltpu.roll(xf, 1, 0), 0.0)
    x_next = jnp.where(cidx < ncs - 1, pltpu.roll(xb, rows - 1, 0), 0.0)
    y = y + jnp.dot(x_prev.astype(BF16), outf_ref[0], preferred_element_type=F32)
    y = y + jnp.dot(x_next.astype(BF16), outb_ref[0], preferred_element_type=F32)
    y = y + u * vec_ref[0, pl.ds(0, 1), :]
    yg = _gelu(y)
    z = jnp.dot(yg.astype(BF16), glu_ref[0], preferred_element_type=F32) + vec_ref[0, pl.ds(1, 1), :]
    o_ref[0] = (yg * _sigmoid(z)).astype(o_ref.dtype)


def _s5(u, params, b, t, cfg):
    lam_re, lam_im, log_dt, b_re, b_im, c_re, c_im, d_skip, glu_w, glu_b = params
    L, cg = cfg.s5_chunk, cfg.s5_group
    n, ch = u.shape
    g = ch // cg
    assert t % L == 0
    ncs = t // L
    nsteps = max(1, (ncs - 1).bit_length())
    tabs = _s5_tables(lam_re, lam_im, log_dt, b_re, b_im, c_re, c_im, d_skip, glu_w, glu_b, L, nsteps)
    uc = u.reshape(n // L, L, g, cg).transpose(2, 0, 1, 3).reshape(g, n // L, L * cg)
    seqs = max(1, min(b, 1024 // ncs))
    while b % seqs:
        seqs -= 1
    rows = seqs * ncs
    w = L * cg
    p2 = 2 * cfg.s5_state
    per_g = lambda shape: pl.BlockSpec((1,) + shape, lambda gi, ri: (gi, 0, 0))
    est = 2 * (rows * w * 4 + rows * w * 2) + 2 * 2 * (3 * w * w + 4 * w * p2) + 12 * rows * w * 4
    out = pl.pallas_call(
        functools.partial(_s5_body, ncs=ncs, nsteps=nsteps),
        out_shape=jax.ShapeDtypeStruct((g, n // L, w), BF16),
        grid=(g, (n // L) // rows),
        in_specs=[pl.BlockSpec((1, rows, w), lambda gi, ri: (gi, ri, 0)),
                  per_g((w, w)), per_g((w, p2)), per_g((w, p2)), per_g((p2, w)), per_g((p2, w)),
                  per_g((w, w)), per_g((2, w)), per_g((4 * nsteps, p2))],
        out_specs=pl.BlockSpec((1, rows, w), lambda gi, ri: (gi, ri, 0)),
        compiler_params=pltpu.CompilerParams(dimension_semantics=("parallel", "parallel"),
                                             vmem_limit_bytes=_vmem_limit(est)),
        name="s5",
    )(uc, *tabs)
    return out.reshape(g, n // L, L, cg).transpose(1, 2, 0, 3).reshape(n, ch)


def _na_bias_tables(rpb, cfg, rows):
    kh, kw, w = cfg.na_kh, cfg.na_kw, cfg.grid_w
    j = np.arange(w)
    c_start = np.clip(j - kw // 2, 0, w - kw)
    c = np.arange(w)
    inside = (c[None, :] >= c_start[:, None]) & (c[None, :] < c_start[:, None] + kw)
    by_row = jnp.stack([rpb.astype(F32)[:, kh - 1 - dl:2 * kh - 1 - dl] for dl in range(kh)], axis=1)
    padded = jnp.pad(by_row, ((0, 0), (0, 0), (0, 0), (w - kw, w - kw)))
    bias = jnp.stack([padded[..., w - 1 - jq:2 * w - 1 - jq] for jq in range(w)], axis=2)
    bias = jnp.where(inside[None, None, :, None, :], bias, NEG_BIG)
    return bias.reshape(rpb.shape[0], kh, w, kh * w)


def _na_body(q_ref, k_ref, v_ref, qg_ref, kg_ref, tab_ref, o_ref, qn, kn, vb, *, w, rows, kh, dh, eps,
             unroll):
    lo = lax.broadcasted_iota(I32, (1, 2 * dh), 1) < dh

    def head_norm(x, gain):
        x2 = x * x
        s_lo = jnp.sum(jnp.where(lo, x2, 0.0), axis=-1, keepdims=True)
        s_hi = jnp.sum(jnp.where(lo, 0.0, x2), axis=-1, keepdims=True)
        ms = jnp.where(lo, s_lo, s_hi) * (1.0 / dh)
        return x * lax.rsqrt(ms + eps) * gain

    qh = (head_norm(q_ref[...], qg_ref[...]) * dh ** -0.5).astype(BF16)
    kh_ = head_norm(k_ref[...], kg_ref[...]).astype(BF16)
    vh = v_ref[...].astype(BF16)
    for hh in range(2):
        sl = slice(hh * dh, (hh + 1) * dh)
        qn[hh], kn[hh], vb[hh] = qh[:, sl], kh_[:, sl], vh[:, sl]
    contract1 = (((1,), (1,)), ((), ()))

    def group(gi, carry):
        chains = []
        for u in range(unroll):
            r = gi * unroll + u
            rs = jnp.clip(r - kh // 2, 0, rows - kh)
            q0 = pl.multiple_of(r * w, w)
            k0 = pl.multiple_of(rs * w, w)
            for hh in range(2):
                s = lax.dot_general(qn[hh, pl.ds(q0, w), :], kn[hh, pl.ds(k0, kh * w), :], contract1,
                                    preferred_element_type=F32)
                chains.append((s + tab_ref[hh, r - rs], hh, k0, q0))
        outs = []
        for s, hh, k0, q0 in chains:
            p = jnp.exp(s - jnp.max(s, axis=-1, keepdims=True))
            p = p / jnp.sum(p, axis=-1, keepdims=True)
            outs.append(jnp.dot(p.astype(BF16), vb[hh, pl.ds(k0, kh * w), :], preferred_element_type=F32))
        for u in range(unroll):
            q0 = chains[2 * u][3]
            o_ref[pl.ds(q0, w), :] = jnp.concatenate(outs[2 * u:2 * u + 2], axis=1).astype(o_ref.dtype)
        return carry

    lax.fori_loop(0, rows // unroll, group, 0)


def _neighbourhood_attention(z, q_gain, k_gain, rpb, b, t, cfg):
    h, w, kh = cfg.na_heads, cfg.grid_w, cfg.na_kh
    dh = z.shape[1] // (3 * h)
    rows = t // w
    assert t % w == 0 and rows >= kh and h % 2 == 0 and 2 * dh == V7X_LANES
    tab = _na_bias_tables(rpb, cfg, rows)
    hp = h // 2
    blk = lambda off: pl.BlockSpec((t, 2 * dh), lambda bi, pi, off=off: (bi, off + pi))
    gain = pl.BlockSpec((1, 2 * dh), lambda bi, pi: (0, pi))
    est = 2 * (3 * t * 2 * dh * 4 + 2 * kh * w * kh * w * 4 + t * 2 * dh * 2) + 3 * 2 * t * V7X_LANES * 2 \
        + 6 * t * 2 * dh * 4
    return pl.pallas_call(
        functools.partial(_na_body, w=w, rows=rows, kh=kh, dh=dh, eps=cfg.eps,
                          unroll=math.gcd(cfg.na_unroll, rows)),
        out_shape=jax.ShapeDtypeStruct((b * t, h * dh), BF16),
        grid=(b, hp),
        in_specs=[blk(0), blk(hp), blk(2 * hp), gain, gain,
                  pl.BlockSpec((2, kh, w, kh * w), lambda bi, pi: (pi, 0, 0, 0))],
        out_specs=pl.BlockSpec((t, 2 * dh), lambda bi, pi: (bi, pi)),
        scratch_shapes=[pltpu.VMEM((2, t, dh), BF16)] * 3,
        compiler_params=pltpu.CompilerParams(dimension_semantics=("parallel", "parallel"),
                                             vmem_limit_bytes=_vmem_limit(est)),
        name="neighbourhood_attention",
    )(z, z, z, q_gain.reshape(1, h * dh).astype(F32), k_gain.reshape(1, h * dh).astype(F32), tab)


def _peer_cells(topk):
    return [(k1, k2) for k1 in range(topk) for k2 in range(topk) if (k1 + 1) * (k2 + 1) <= topk]


def _top_rows(s, k, iota):
    nrows = s.shape[0]
    vals, idxs = [], []
    for _ in range(k):
        m = jnp.max(s, axis=0, keepdims=True)
        idx = jnp.min(jnp.where(s == m, iota, nrows), axis=0, keepdims=True)
        vals.append(m)
        idxs.append(idx)
        s = jnp.where(iota == idx, -jnp.inf, s)
    return vals, idxs


def _route_body(qp_ref, keys_ref, a_ref, b_ref, g_ref, a_t, b_t, g_t, c_s, c_a, c_b,
                *, heads, nkeys, topk, cells):
    tm = qp_ref.shape[0]
    dq = keys_ref.shape[2]
    iota_k = lax.broadcasted_iota(I32, (nkeys, tm), 0)
    ncell = len(cells)
    crow = c_s.shape[0]
    iota_c = lax.broadcasted_iota(I32, (crow, tm), 0)
    contract1 = (((1,), (1,)), ((), ()))
    c_s[pl.ds(ncell, crow - ncell), :] = jnp.full((crow - ncell, tm), -jnp.inf, F32)
    c_a[pl.ds(ncell, crow - ncell), :] = jnp.zeros((crow - ncell, tm), I32)
    c_b[pl.ds(ncell, crow - ncell), :] = jnp.zeros((crow - ncell, tm), I32)
    for h in range(heads):
        tops = []
        for p in range(2):
            c0 = (2 * h + p) * dq
            q = qp_ref[:, c0:c0 + dq].astype(BF16)
            s = lax.dot_general(keys_ref[p], q, contract1, preferred_element_type=F32)
            tops.append(_top_rows(s, topk, iota_k))
        (v1, i1), (v2, i2) = tops
        for ci, (k1, k2) in enumerate(cells):
            c_s[pl.ds(ci, 1), :] = v1[k1] + v2[k2]
            c_a[pl.ds(ci, 1), :] = i1[k1]
            c_b[pl.ds(ci, 1), :] = i2[k2]
        cs, ca, cb = c_s[...], c_a[...], c_b[...]
        best = []
        for kk in range(topk):
            m = jnp.max(cs, axis=0, keepdims=True)
            pos = jnp.min(jnp.where(cs == m, iota_c, crow), axis=0, keepdims=True)
            hit = iota_c == pos
            j = h * topk + kk
            a_t[pl.ds(j, 1), :] = jnp.sum(jnp.where(hit, ca, 0), axis=0, keepdims=True)
            b_t[pl.ds(j, 1), :] = jnp.sum(jnp.where(hit, cb, 0), axis=0, keepdims=True)
            best.append(m)
            cs = jnp.where(hit, -jnp.inf, cs)
        e = [jnp.exp(v - best[0]) for v in best]
        denom = e[0]
        for v in e[1:]:
            denom = denom + v
        for kk in range(topk):
            g_t[pl.ds(h * topk + kk, 1), :] = e[kk] / denom
    a_ref[...] = a_t[...].T
    b_ref[...] = b_t[...].T
    g_ref[...] = g_t[...].T


def _peer_route(qp, sub_keys, cfg):
    n = qp.shape[0]
    heads, nkeys, topk = cfg.peer_heads, cfg.peer_nkeys, cfg.peer_topk
    dq = sub_keys.shape[2]
    nj = heads * topk
    tm = min(cfg.topk_tm, n)
    assert nj == V7X_LANES and qp.shape[1] == heads * 2 * dq
    cells = _peer_cells(topk)
    crow = -(-(len(cells) + 1) // V7X_SUBLANES) * V7X_SUBLANES
    outs = pl.pallas_call(
        functools.partial(_route_body, heads=heads, nkeys=nkeys, topk=topk, cells=tuple(cells)),
        out_shape=[jax.ShapeDtypeStruct((n, nj), I32), jax.ShapeDtypeStruct((n, nj), I32),
                   jax.ShapeDtypeStruct((n, nj), F32)],
        grid=(n // tm,),
        in_specs=[pl.BlockSpec((tm, qp.shape[1]), lambda i: (i, 0)),
                  pl.BlockSpec(sub_keys.shape, lambda i: (0, 0, 0))],
        out_specs=[pl.BlockSpec((tm, nj), lambda i: (i, 0))] * 3,
        scratch_shapes=[pltpu.VMEM((nj, tm), I32), pltpu.VMEM((nj, tm), I32), pltpu.VMEM((nj, tm), F32),
                        pltpu.VMEM((crow, tm), F32), pltpu.VMEM((crow, tm), I32), pltpu.VMEM((crow, tm), I32)],
        compiler_params=pltpu.CompilerParams(dimension_semantics=("parallel",),
                                             vmem_limit_bytes=_vmem_limit(0)),
        name="peer_route",
    )(qp, sub_keys.astype(BF16))
    return outs


W_PITCH_PAD = 8
HIGH_HALF = np.uint32(0xFFFF0000)
HALF_ULP_BF16 = np.uint32(0x8000)


def _expert_body(hn_ref, a_ref, b_ref, g_ref, ut_ref, v_ref, x_ref, o_ref, wmap, acc, *, nk, pitch, unroll):
    j = pl.program_id(1)
    tm = hn_ref.shape[0]
    te = ut_ref.shape[1]
    nblk = te // nk
    half = nk // 2
    contract1 = (((1,), (1,)), ((), ()))

    @pl.when(j == 0)
    def _():
        acc[...] = jnp.zeros_like(acc)
        row = lax.broadcasted_iota(I32, (nk, a_ref.shape[1]), 0)
        a_of_row = jnp.where(row < half, 2 * row, 2 * (row - half) + 1)

        def build(n, carry):
            ar = a_ref[pl.ds(n, 1), :]
            br = b_ref[pl.ds(n, 1), :]
            gr = g_ref[pl.ds(n, 1), :] * 0.5
            one_a = jnp.where(a_of_row == ar, 1.0, 0.0).astype(BF16)
            gate_b = jnp.where(row == br, gr, 0.0).astype(BF16)
            wn = lax.dot_general(one_a, gate_b, contract1, preferred_element_type=F32)
            lo = pltpu.bitcast(wn[:half], jnp.uint32)
            hi = pltpu.bitcast(wn[half:], jnp.uint32)
            packed = ((hi + HALF_ULP_BF16) & HIGH_HALF) | ((lo + HALF_ULP_BF16) >> 16)
            wmap[pl.ds(pl.multiple_of(n * pitch, V7X_SUBLANES), half), :] = packed
            return carry

        lax.fori_loop(0, tm, build, 0, unroll=unroll)

    x = jnp.dot(hn_ref[...], ut_ref[...], preferred_element_type=F32)
    act = x * (1.0 + lax.erf(x * (1.0 / math.sqrt(2.0))))
    parts = []
    for i in range(nblk // 2):
        w32 = wmap[pl.ds(j * (nblk // 2) + i, tm, stride=pitch), :]
        w_even = pltpu.bitcast(w32 << 16, F32)
        w_odd = pltpu.bitcast(w32 & HIGH_HALF, F32)
        parts.append((act[:, (2 * i) * nk:(2 * i + 1) * nk] * w_even).astype(BF16))
        parts.append((act[:, (2 * i + 1) * nk:(2 * i + 2) * nk] * w_odd).astype(BF16))
    pmat = jnp.concatenate(parts, axis=1)
    acc[...] += jnp.dot(pmat, v_ref[...], preferred_element_type=F32)

    @pl.when(j == pl.num_programs(1) - 1)
    def _():
        o_ref[...] = x_ref[...] + acc[...]


def _peer_experts(hn, a, b, g, ut, v, x2, cfg):
    n, d = x2.shape
    nexp = ut.shape[1]
    nk = cfg.peer_nkeys
    tm = min(cfg.peer_tm, n)
    te = min(cfg.peer_te, nexp)
    assert nk == V7X_LANES and nexp == nk * nk and te % (2 * nk) == 0 and nexp % te == 0
    pitch = nk // 2 + W_PITCH_PAD
    nj = a.shape[1]
    est = 2 * (tm * d * 2 + 3 * tm * nj * 4 + 2 * d * te * 2 + 2 * tm * d * 4) \
        + tm * pitch * nk * 4 + tm * d * 4 + 4 * tm * te * 4
    tile = lambda width: pl.BlockSpec((tm, width), lambda i, j: (i, 0))
    return pl.pallas_call(
        functools.partial(_expert_body, nk=nk, pitch=pitch, unroll=min(cfg.peer_unroll, tm)),
        out_shape=jax.ShapeDtypeStruct((n, d), F32),
        grid=(n // tm, nexp // te),
        in_specs=[tile(d), tile(nj), tile(nj), tile(nj),
                  pl.BlockSpec((d, te), lambda i, j: (0, j)),
                  pl.BlockSpec((te, d), lambda i, j: (j, 0)),
                  tile(d)],
        out_specs=tile(d),
        scratch_shapes=[pltpu.VMEM((tm * pitch, nk), jnp.uint32), pltpu.VMEM((tm, d), F32)],
        compiler_params=pltpu.CompilerParams(dimension_semantics=("parallel", "arbitrary"),
                                             vmem_limit_bytes=_vmem_limit(est)),
        name="peer_experts",
    )(hn, a, b, g, ut, v, x2)


def _peer(x2, gain, w_q_bf16, sub_keys, ut, v, cfg):
    nq = w_q_bf16.shape[1]
    qp, hn = _norm_matmul(x2, gain, w_q_bf16, [(0, nq)], [F32], cfg, emit_hn=True)
    a, b, g = _peer_route(qp, sub_keys, cfg)
    return _peer_experts(hn, a, b, g, ut, v, x2, cfg)


def _trunk(x, p, cfg):
    b, t, d = x.shape
    x2 = x.reshape(b * t, d)
    depth = p["norm_mix"].shape[0]
    for l in range(depth):
        i = l // 2
        if l % 2 == 0:
            rw = 4 * cfg.ret_heads * cfg.ret_dk
            w_in = p["ab_w_in"][i]
            zq, u = _norm_matmul(x2, p["norm_mix"][l], w_in, [(0, rw), (rw, w_in.shape[1])], [F32, F32], cfg)
            ret = _retention(zq, p["ab_ret_decay"][i], p["ab_ret_gn"][i], b, t, cfg)
            ssm = _s5(u, tuple(p[k][i] for k in ("ab_s5_lam_re", "ab_s5_lam_im", "ab_s5_log_dt", "ab_s5_b_re",
                                                  "ab_s5_b_im", "ab_s5_c_re", "ab_s5_c_im", "ab_s5_d",
                                                  "ab_s5_glu_w", "ab_s5_glu_b")), b, t, cfg)
            w_out = p["ab_w_out"][i]
            nr = ret.shape[1]
            x2 = _matmul_residual([ret, ssm], [w_out[:nr], w_out[nr:]], x2, cfg)
        else:
            w_qkv = p["na_w_qkv"][i]
            (z,) = _norm_matmul(x2, p["norm_mix"][l], w_qkv, [(0, w_qkv.shape[1])], [F32], cfg)
            att = _neighbourhood_attention(z, p["na_q_gain"][i], p["na_k_gain"][i], p["na_rpb"][i], b, t, cfg)
            x2 = _matmul_residual([att], [p["na_w_o"][i]], x2, cfg)
        x2 = _peer(x2, p["norm_ffn"][l], p["peer_w_q"][l], p["peer_sub_keys"][l], p["peer_ut"][l],
                   p["peer_v"][l], cfg)
    return x2.reshape(b, t, d)


def _prepare(params):
    p = dict(params)
    for k in ("ab_w_in", "ab_w_out", "na_w_qkv", "na_w_o", "peer_w_q", "peer_v"):
        p[k] = params[k].astype(BF16)
    p["peer_ut"] = jnp.swapaxes(params["peer_u"], 1, 2).astype(BF16)
    return p


def _forward(x_prompt, x_sample, params, cfg=Cfg()):
    p = _prepare(params)
    return _trunk(x_prompt, p, cfg), _trunk(x_sample, p, cfg)


def kernel(x_prompt, x_sample, norm_mix, norm_ffn, ab_w_in, ab_ret_decay, ab_ret_gn, ab_s5_lam_re, ab_s5_lam_im, ab_s5_log_dt, ab_s5_b_re, ab_s5_b_im, ab_s5_c_re, ab_s5_c_im, ab_s5_d, ab_s5_glu_w, ab_s5_glu_b, ab_w_out, na_w_qkv, na_q_gain, na_k_gain, na_rpb, na_w_o, peer_w_q, peer_sub_keys, peer_u, peer_v):
    params = dict(norm_mix=norm_mix, norm_ffn=norm_ffn, ab_w_in=ab_w_in, ab_ret_decay=ab_ret_decay,
                  ab_ret_gn=ab_ret_gn, ab_s5_lam_re=ab_s5_lam_re, ab_s5_lam_im=ab_s5_lam_im,
                  ab_s5_log_dt=ab_s5_log_dt, ab_s5_b_re=ab_s5_b_re, ab_s5_b_im=ab_s5_b_im,
                  ab_s5_c_re=ab_s5_c_re, ab_s5_c_im=ab_s5_c_im, ab_s5_d=ab_s5_d, ab_s5_glu_w=ab_s5_glu_w,
                  ab_s5_glu_b=ab_s5_glu_b, ab_w_out=ab_w_out, na_w_qkv=na_w_qkv, na_q_gain=na_q_gain,
                  na_k_gain=na_k_gain, na_rpb=na_rpb, na_w_o=na_w_o, peer_w_q=peer_w_q,
                  peer_sub_keys=peer_sub_keys, peer_u=peer_u, peer_v=peer_v)
    return _forward(x_prompt, x_sample, params)
```

```python
import functools
import math
from typing import NamedTuple

import numpy as np
import jax
import jax.numpy as jnp
from jax import lax
from jax.experimental import pallas as pl
from jax.experimental.pallas import tpu as pltpu

F32 = jnp.float32
BF16 = jnp.bfloat16
I32 = jnp.int32

V7X_LANES = 128
V7X_SUBLANES = 8
V7X_VMEM_BYTES = 64 * 2**20
VMEM_LIMIT_CAP = V7X_VMEM_BYTES - 8 * 2**20

NEG_BIG = -1e30


class Cfg(NamedTuple):
    eps: float = 1e-6
    grid_w: int = 64
    ret_heads: int = 4
    ret_dk: int = 128
    ret_chunk: int = 128
    rope_base: float = 10000.0
    s5_group: int = 16
    s5_state: int = 64
    s5_chunk: int = 16
    na_heads: int = 16
    na_kh: int = 8
    na_kw: int = 16
    peer_heads: int = 8
    peer_nkeys: int = 128
    peer_topk: int = 16
    tm: int = 512
    peer_tm: int = 512
    peer_unroll: int = 32
    na_unroll: int = 4
    peer_te: int = 1024
    topk_tm: int = 256


def _vmem_limit(nbytes):
    return int(min(VMEM_LIMIT_CAP, max(32 * 2**20, nbytes)))


def _gelu(x):
    return 0.5 * x * (1.0 + lax.erf(x * (1.0 / math.sqrt(2.0))))


def _sigmoid(x):
    return 1.0 / (1.0 + jnp.exp(-x))


def _norm_mm_body(x_ref, g_ref, w_ref, *out_refs, splits, emit_hn, eps):
    x = x_ref[...]
    y = x * lax.rsqrt(jnp.mean(x * x, axis=-1, keepdims=True) + eps) * g_ref[...]
    yb = y.astype(BF16)
    z = jnp.dot(yb, w_ref[...], preferred_element_type=F32)
    for r, (s, e) in zip(out_refs, splits):
        r[...] = z[:, s:e].astype(r.dtype)
    if emit_hn:
        out_refs[len(splits)][...] = yb


def _norm_matmul(x2, gain, w_bf16, splits, dtypes, cfg, emit_hn=False):
    n, d = x2.shape
    nout = w_bf16.shape[1]
    tm = min(cfg.tm, n)
    out_shape = [jax.ShapeDtypeStruct((n, e - s), dt) for (s, e), dt in zip(splits, dtypes)]
    out_specs = [pl.BlockSpec((tm, e - s), lambda i: (i, 0)) for (s, e) in splits]
    if emit_hn:
        out_shape.append(jax.ShapeDtypeStruct((n, d), BF16))
        out_specs.append(pl.BlockSpec((tm, d), lambda i: (i, 0)))
    est = 2 * (tm * d * 4 + d * nout * 2 + tm * nout * 4 + tm * d * 2) + 2 * tm * nout * 4
    return pl.pallas_call(
        functools.partial(_norm_mm_body, splits=tuple(splits), emit_hn=emit_hn, eps=cfg.eps),
        out_shape=out_shape,
        grid=(n // tm,),
        in_specs=[pl.BlockSpec((tm, d), lambda i: (i, 0)),
                  pl.BlockSpec((1, d), lambda i: (0, 0)),
                  pl.BlockSpec((d, nout), lambda i: (0, 0))],
        out_specs=out_specs,
        compiler_params=pltpu.CompilerParams(dimension_semantics=("parallel",),
                                             vmem_limit_bytes=_vmem_limit(est)),
        name="norm_matmul",
    )(x2, gain.reshape(1, d).astype(F32), w_bf16)


def _mm_res_body(*refs, n_in):
    acc = refs[2 * n_in][...]
    for a, w in zip(refs[:n_in], refs[n_in:2 * n_in]):
        acc = acc + jnp.dot(a[...], w[...], preferred_element_type=F32)
    refs[-1][...] = acc


def _matmul_residual(a_list, w_list, res, cfg):
    n, d = res.shape
    tm = min(cfg.tm, n)
    n_in = len(a_list)
    in_specs = ([pl.BlockSpec((tm, a.shape[1]), lambda i: (i, 0)) for a in a_list]
                + [pl.BlockSpec(w.shape, lambda i: (0, 0)) for w in w_list]
                + [pl.BlockSpec((tm, d), lambda i: (i, 0))])
    est = 2 * (sum(tm * a.shape[1] * 2 for a in a_list) + sum(w.size * 2 for w in w_list)
               + 2 * tm * d * 4) + 2 * tm * d * 4
    return pl.pallas_call(
        functools.partial(_mm_res_body, n_in=n_in),
        out_shape=jax.ShapeDtypeStruct((n, d), F32),
        grid=(n // tm,),
        in_specs=in_specs,
        out_specs=pl.BlockSpec((tm, d), lambda i: (i, 0)),
        compiler_params=pltpu.CompilerParams(dimension_semantics=("parallel",),
                                             vmem_limit_bytes=_vmem_limit(est)),
        name="matmul_residual",
    )(*a_list, *w_list, res)


def _ret_tables(ret_decay, chunk, width):
    lg = -jax.nn.softplus(-ret_decay.astype(F32))
    pos = jnp.arange(chunk, dtype=F32)
    diff = pos[:, None] - pos[None, :]
    d_f = jnp.where(diff >= 0, jnp.exp(lg[0][:, None, None] * jnp.maximum(diff, 0.0)[None]), 0.0)
    d_b = jnp.where(diff < 0, jnp.exp(lg[1][:, None, None] * jnp.maximum(-diff, 0.0)[None]), 0.0)
    cols = [jnp.exp(lg[0][:, None] * (chunk - 1.0 - pos)[None]),
            jnp.exp(lg[1][:, None] * pos[None]),
            jnp.exp(lg[0][:, None] * (pos + 1.0)[None]),
            jnp.exp(lg[1][:, None] * (chunk - pos)[None]),
            jnp.broadcast_to(jnp.exp(lg[0] * chunk)[:, None], (lg.shape[1], chunk)),
            jnp.broadcast_to(jnp.exp(lg[1] * chunk)[:, None], (lg.shape[1], chunk))]
    tab = jnp.stack(cols, axis=1)
    return d_f + d_b, jnp.broadcast_to(tab[..., None], tab.shape + (width,))


def _rope_tables(t, half, base):
    inv = base ** (-jnp.arange(half, dtype=F32) / half)
    ang = jnp.arange(t, dtype=F32)[:, None] * inv[None, :]
    cos, sin = jnp.cos(ang), jnp.sin(ang)
    return jnp.concatenate([cos, cos], axis=1), jnp.concatenate([-sin, sin], axis=1)


def _ret_body(q_ref, k_ref, v_ref, g_ref, cos_ref, sin_ref, d_ref, tab_ref, gn_ref, o_ref,
              qs, ks, sb, *, chunk, nc, kscale, eps):
    dk = q_ref.shape[1]
    half = dk // 2
    cos, sin = cos_ref[...], sin_ref[...]
    q = q_ref[...]
    qs[...] = q * cos + pltpu.roll(q, half, 1) * sin
    k = k_ref[...]
    ks[...] = (k * cos + pltpu.roll(k, half, 1) * sin) * kscale
    contract0 = (((0,), (0,)), ((), ()))
    contract1 = (((1,), (1,)), ((), ()))

    k_f, k_b, q_f, q_b = tab_ref[0, 0], tab_ref[0, 1], tab_ref[0, 2], tab_ref[0, 3]
    cd_f, cd_b = tab_ref[0, 4], tab_ref[0, 5]

    def bwd(i, state):
        n = nc - 1 - i
        r0 = pl.multiple_of(n * chunk, chunk)
        sb[n] = state
        kc = ks[pl.ds(r0, chunk), :]
        vc = v_ref[pl.ds(r0, chunk), :].astype(BF16)
        kv = lax.dot_general((kc * k_b).astype(BF16), vc, contract0, preferred_element_type=F32)
        return state * cd_b + kv

    lax.fori_loop(0, nc, bwd, jnp.zeros((dk, v_ref.shape[1]), F32))

    def fwd(n, state):
        r0 = pl.multiple_of(n * chunk, chunk)
        qc = qs[pl.ds(r0, chunk), :]
        kc = ks[pl.ds(r0, chunk), :]
        vc = v_ref[pl.ds(r0, chunk), :].astype(BF16)
        s = lax.dot_general(qc.astype(BF16), kc.astype(BF16), contract1, preferred_element_type=F32)
        o = jnp.dot((s * d_ref[0]).astype(BF16), vc, preferred_element_type=F32)
        o = o + jnp.dot((qc * q_f).astype(BF16), state.astype(BF16), preferred_element_type=F32)
        o = o + jnp.dot((qc * q_b).astype(BF16), sb[n].astype(BF16), preferred_element_type=F32)
        oc = o - jnp.mean(o, axis=-1, keepdims=True)
        o = oc * lax.rsqrt(jnp.mean(oc * oc, axis=-1, keepdims=True) + eps)
        g = g_ref[pl.ds(r0, chunk), :]
        o_ref[pl.ds(r0, chunk), :] = (o * gn_ref[...] * (g * _sigmoid(g))).astype(o_ref.dtype)
        kv = lax.dot_general((kc * k_f).astype(BF16), vc, contract0, preferred_element_type=F32)
        return state * cd_f + kv

    lax.fori_loop(0, nc, fwd, jnp.zeros((dk, v_ref.shape[1]), F32))


def _retention(zq, ret_decay, ret_gn, b, t, cfg):
    h, dk, c = cfg.ret_heads, cfg.ret_dk, cfg.ret_chunk
    assert t % c == 0 and zq.shape[1] == 4 * h * dk
    nc = t // c
    dmat, tab = _ret_tables(ret_decay, c, dk)
    cos2, sin2 = _rope_tables(t, dk // 2, cfg.rope_base)
    blk = lambda off: pl.BlockSpec((t, dk), lambda bi, hi, off=off: (bi, off + hi))
    est = 2 * (4 * t * dk * 4 + 2 * t * dk * 4 + c * c * 4 + 6 * c * dk * 4 + t * dk * 2) \
        + 2 * t * dk * 4 + nc * dk * dk * 4 + 8 * t * dk * 4
    return pl.pallas_call(
        functools.partial(_ret_body, chunk=c, nc=nc, kscale=dk ** -0.5, eps=cfg.eps),
        out_shape=jax.ShapeDtypeStruct((b * t, h * dk), BF16),
        grid=(b, h),
        in_specs=[blk(0), blk(h), blk(2 * h), blk(3 * h),
                  pl.BlockSpec((t, dk), lambda bi, hi: (0, 0)),
                  pl.BlockSpec((t, dk), lambda bi, hi: (0, 0)),
                  pl.BlockSpec((1, c, c), lambda bi, hi: (hi, 0, 0)),
                  pl.BlockSpec((1, 6, c, dk), lambda bi, hi: (hi, 0, 0, 0)),
                  pl.BlockSpec((1, dk), lambda bi, hi: (0, hi))],
        out_specs=pl.BlockSpec((t, dk), lambda bi, hi: (bi, hi)),
        scratch_shapes=[pltpu.VMEM((t, dk), F32), pltpu.VMEM((t, dk), F32),
                        pltpu.VMEM((nc, dk, dk), F32)],
        compiler_params=pltpu.CompilerParams(dimension_semantics=("parallel", "parallel"),
                                             vmem_limit_bytes=_vmem_limit(est)),
        name="retention",
    )(zq, zq, zq, zq, cos2, sin2, dmat, tab, ret_gn.reshape(1, h * dk).astype(F32))


def _s5_tables(lam_re, lam_im, log_dt, b_re, b_im, c_re, c_im, d_skip, glu_w, glu_b, L, nsteps):
    lam = lax.complex(lam_re.astype(F32), lam_im.astype(F32))
    ldt = lam * jnp.exp(log_dt.astype(F32))[..., None]
    lam_bar = jnp.exp(ldt)
    bmat = lax.complex(b_re.astype(F32), b_im.astype(F32))
    b_bar = ((lam_bar - 1.0) / lam)[..., None] * bmat[None]
    cmat = lax.complex(c_re.astype(F32), c_im.astype(F32))
    g, p, cg = bmat.shape
    tau = jnp.arange(L + 1, dtype=F32)
    pw = jnp.exp(ldt[:, :, None, :] * tau[None, None, :, None])
    kern = jnp.real(jnp.einsum('dgop,dgtp,dgpi->dgtoi', cmat, pw[:, :, :L], b_bar))
    li = np.arange(L)
    lag = li[None, :] - li[:, None]
    k_f = jnp.where((lag >= 0)[None, :, :, None, None], kern[0][:, np.clip(lag, 0, L - 1)], 0.0)
    k_b = jnp.where((lag <= 0)[None, :, :, None, None], kern[1][:, np.clip(-lag, 0, L - 1)], 0.0)
    m = (k_f + k_b).transpose(0, 1, 4, 2, 3).reshape(g, L * cg, L * cg)

    def cat(z):
        return jnp.concatenate([jnp.real(z), jnp.imag(z)], axis=-1)

    inc_f = cat(jnp.einsum('glp,gpi->glip', pw[0][:, L - 1 - li], b_bar[0])).reshape(g, L * cg, 2 * p)
    inc_b = cat(jnp.einsum('glp,gpi->glip', pw[1][:, li], b_bar[1])).reshape(g, L * cg, 2 * p)

    def out_mat(z):
        return jnp.concatenate([jnp.real(z), -jnp.imag(z)], axis=1).reshape(g, 2 * p, L * cg)

    out_f = out_mat(jnp.einsum('gop,glp->gplo', cmat[0], pw[0][:, li + 1]))
    out_b = out_mat(jnp.einsum('gop,glp->gplo', cmat[1], pw[1][:, L - li]))
    eye = jnp.eye(L, dtype=F32)
    glu = jnp.einsum('lm,gce->glcme', eye, glu_w.astype(F32)).reshape(g, L * cg, L * cg)
    vecs = jnp.stack([jnp.tile(d_skip.astype(F32), (1, L)), jnp.tile(glu_b.astype(F32), (1, L))], axis=1)
    steps = (2.0 ** jnp.arange(nsteps, dtype=F32)) * L
    a = jnp.exp(ldt[:, :, None, :] * steps[None, None, :, None])
    scan = jnp.stack([jnp.concatenate([jnp.real(a), jnp.real(a)], -1),
                      jnp.concatenate([-jnp.imag(a), jnp.imag(a)], -1)], axis=3)
    scan = scan.transpose(1, 0, 2, 3, 4).reshape(g, 2 * nsteps * 2, 2 * p)
    return (m.astype(BF16), inc_f.astype(BF16), inc_b.astype(BF16), out_f.astype(BF16),
            out_b.astype(BF16), glu.astype(BF16), vecs, scan)


def _s5_body(u_ref, m_ref, incf_ref, incb_ref, outf_ref, outb_ref, glu_ref, vec_ref, scan_ref, o_ref,
             *, ncs, nsteps):
    u = u_ref[0]
    rows = u.shape[0]
    ub = u.astype(BF16)
    y = jnp.dot(ub, m_ref[0], preferred_element_type=F32)
    xf = jnp.dot(ub, incf_ref[0], preferred_element_type=F32)
    xb = jnp.dot(ub, incb_ref[0], preferred_element_type=F32)
    p2 = xf.shape[1]
    cidx = lax.rem(lax.broadcasted_iota(I32, (rows, 1), 0), ncs)
    for kk in range(nsteps):
        s = 1 << kk
        a_f, b_f = scan_ref[0, pl.ds(2 * kk, 1), :], scan_ref[0, pl.ds(2 * kk + 1, 1), :]
        a_b = scan_ref[0, pl.ds(2 * nsteps + 2 * kk, 1), :]
        b_b = scan_ref[0, pl.ds(2 * nsteps + 2 * kk + 1, 1), :]
        pf = pltpu.roll(xf, s, 0)
        xf = xf + jnp.where(cidx >= s, a_f * pf + b_f * pltpu.roll(pf, p2 // 2, 1), 0.0)
        pb = pltpu.roll(xb, rows - s, 0)
        xb = xb + jnp.where(cidx < ncs - s, a_b * pb + b_b * pltpu.roll(pb, p2 // 2, 1), 0.0)
    x_prev = jnp.where(cidx >= 1, pltpu.roll(xf, 1, 0), 0.0)
    x_next = jnp.where(cidx < ncs - 1, pltpu.roll(xb, rows - 1, 0), 0.0)
    y = y + jnp.dot(x_prev.astype(BF16), outf_ref[0], preferred_element_type=F32)
    y = y + jnp.dot(x_next.astype(BF16), outb_ref[0], preferred_element_type=F32)
    y = y + u * vec_ref[0, pl.ds(0, 1), :]
    yg = _gelu(y)
    z = jnp.dot(yg.astype(BF16), glu_ref[0], preferred_element_type=F32) + vec_ref[0, pl.ds(1, 1), :]
    o_ref[0] = (yg * _sigmoid(z)).astype(o_ref.dtype)


def _s5(u, params, b, t, cfg):
    lam_re, lam_im, log_dt, b_re, b_im, c_re, c_im, d_skip, glu_w, glu_b = params
    L, cg = cfg.s5_chunk, cfg.s5_group
    n, ch = u.shape
    g = ch // cg
    assert t % L == 0
    ncs = t // L
    nsteps = max(1, (ncs - 1).bit_length())
    tabs = _s5_tables(lam_re, lam_im, log_dt, b_re, b_im, c_re, c_im, d_skip, glu_w, glu_b, L, nsteps)
    uc = u.reshape(n // L, L, g, cg).transpose(2, 0, 1, 3).reshape(g, n // L, L * cg)
    seqs = max(1, min(b, 1024 // ncs))
    while b % seqs:
        seqs -= 1
    rows = seqs * ncs
    w = L * cg
    p2 = 2 * cfg.s5_state
    per_g = lambda shape: pl.BlockSpec((1,) + shape, lambda gi, ri: (gi, 0, 0))
    est = 2 * (rows * w * 4 + rows * w * 2) + 2 * 2 * (3 * w * w + 4 * w * p2) + 12 * rows * w * 4
    out = pl.pallas_call(
        functools.partial(_s5_body, ncs=ncs, nsteps=nsteps),
        out_shape=jax.ShapeDtypeStruct((g, n // L, w), BF16),
        grid=(g, (n // L) // rows),
        in_specs=[pl.BlockSpec((1, rows, w), lambda gi, ri: (gi, ri, 0)),
                  per_g((w, w)), per_g((w, p2)), per_g((w, p2)), per_g((p2, w)), per_g((p2, w)),
                  per_g((w, w)), per_g((2, w)), per_g((4 * nsteps, p2))],
        out_specs=pl.BlockSpec((1, rows, w), lambda gi, ri: (gi, ri, 0)),
        compiler_params=pltpu.CompilerParams(dimension_semantics=("parallel", "parallel"),
                                             vmem_limit_bytes=_vmem_limit(est)),
        name="s5",
    )(uc, *tabs)
    return out.reshape(g, n // L, L, cg).transpose(1, 2, 0, 3).reshape(n, ch)


def _na_bias_tables(rpb, cfg, rows):
    kh, kw, w = cfg.na_kh, cfg.na_kw, cfg.grid_w
    j = np.arange(w)
    c_start = np.clip(j - kw // 2, 0, w - kw)
    c = np.arange(w)
    inside = (c[None, :] >= c_start[:, None]) & (c[None, :] < c_start[:, None] + kw)
    by_row = jnp.stack([rpb.astype(F32)[:, kh - 1 - dl:2 * kh - 1 - dl] for dl in range(kh)], axis=1)
    padded = jnp.pad(by_row, ((0, 0), (0, 0), (0, 0), (w - kw, w - kw)))
    bias = jnp.stack([padded[..., w - 1 - jq:2 * w - 1 - jq] for jq in range(w)], axis=2)
    bias = jnp.where(inside[None, None, :, None, :], bias, NEG_BIG)
    return bias.reshape(rpb.shape[0], kh, w, kh * w)


def _na_body(q_ref, k_ref, v_ref, qg_ref, kg_ref, tab_ref, o_ref, qn, kn, vb, *, w, rows, kh, dh, eps,
             unroll):
    lo = lax.broadcasted_iota(I32, (1, 2 * dh), 1) < dh

    def head_norm(x, gain):
        x2 = x * x
        s_lo = jnp.sum(jnp.where(lo, x2, 0.0), axis=-1, keepdims=True)
        s_hi = jnp.sum(jnp.where(lo, 0.0, x2), axis=-1, keepdims=True)
        ms = jnp.where(lo, s_lo, s_hi) * (1.0 / dh)
        return x * lax.rsqrt(ms + eps) * gain

    qh = (head_norm(q_ref[...], qg_ref[...]) * dh ** -0.5).astype(BF16)
    kh_ = head_norm(k_ref[...], kg_ref[...]).astype(BF16)
    vh = v_ref[...].astype(BF16)
    for hh in range(2):
        sl = slice(hh * dh, (hh + 1) * dh)
        qn[hh], kn[hh], vb[hh] = qh[:, sl], kh_[:, sl], vh[:, sl]
    contract1 = (((1,), (1,)), ((), ()))

    def group(gi, carry):
        chains = []
        for u in range(unroll):
            r = gi * unroll + u
            rs = jnp.clip(r - kh // 2, 0, rows - kh)
            q0 = pl.multiple_of(r * w, w)
            k0 = pl.multiple_of(rs * w, w)
            for hh in range(2):
                s = lax.dot_general(qn[hh, pl.ds(q0, w), :], kn[hh, pl.ds(k0, kh * w), :], contract1,
                                    preferred_element_type=F32)
                chains.append((s + tab_ref[hh, r - rs], hh, k0, q0))
        outs = []
        for s, hh, k0, q0 in chains:
            p = jnp.exp(s - jnp.max(s, axis=-1, keepdims=True))
            p = p / jnp.sum(p, axis=-1, keepdims=True)
            outs.append(jnp.dot(p.astype(BF16), vb[hh, pl.ds(k0, kh * w), :], preferred_element_type=F32))
        for u in range(unroll):
            q0 = chains[2 * u][3]
            o_ref[pl.ds(q0, w), :] = jnp.concatenate(outs[2 * u:2 * u + 2], axis=1).astype(o_ref.dtype)
        return carry

    lax.fori_loop(0, rows // unroll, group, 0)


def _neighbourhood_attention(z, q_gain, k_gain, rpb, b, t, cfg):
    h, w, kh = cfg.na_heads, cfg.grid_w, cfg.na_kh
    dh = z.shape[1] // (3 * h)
    rows = t // w
    assert t % w == 0 and rows >= kh and h % 2 == 0 and 2 * dh == V7X_LANES
    tab = _na_bias_tables(rpb, cfg, rows)
    hp = h // 2
    blk = lambda off: pl.BlockSpec((t, 2 * dh), lambda bi, pi, off=off: (bi, off + pi))
    gain = pl.BlockSpec((1, 2 * dh), lambda bi, pi: (0, pi))
    est = 2 * (3 * t * 2 * dh * 4 + 2 * kh * w * kh * w * 4 + t * 2 * dh * 2) + 3 * 2 * t * V7X_LANES * 2 \
        + 6 * t * 2 * dh * 4
    return pl.pallas_call(
        functools.partial(_na_body, w=w, rows=rows, kh=kh, dh=dh, eps=cfg.eps,
                          unroll=math.gcd(cfg.na_unroll, rows)),
        out_shape=jax.ShapeDtypeStruct((b * t, h * dh), BF16),
        grid=(b, hp),
        in_specs=[blk(0), blk(hp), blk(2 * hp), gain, gain,
                  pl.BlockSpec((2, kh, w, kh * w), lambda bi, pi: (pi, 0, 0, 0))],
        out_specs=pl.BlockSpec((t, 2 * dh), lambda bi, pi: (bi, pi)),
        scratch_shapes=[pltpu.VMEM((2, t, dh), BF16)] * 3,
        compiler_params=pltpu.CompilerParams(dimension_semantics=("parallel", "parallel"),
                                             vmem_limit_bytes=_vmem_limit(est)),
        name="neighbourhood_attention",
    )(z, z, z, q_gain.reshape(1, h * dh).astype(F32), k_gain.reshape(1, h * dh).astype(F32), tab)


def _peer_cells(topk):
    return [(k1, k2) for k1 in range(topk) for k2 in range(topk) if (k1 + 1) * (k2 + 1) <= topk]


def _top_rows(s, k, iota):
    nrows = s.shape[0]
    vals, idxs = [], []
    for _ in range(k):
        m = jnp.max(s, axis=0, keepdims=True)
        idx = jnp.min(jnp.where(s == m, iota, nrows), axis=0, keepdims=True)
        vals.append(m)
        idxs.append(idx)
        s = jnp.where(iota == idx, -jnp.inf, s)
    return vals, idxs


def _route_body(qp_ref, keys_ref, a_ref, b_ref, g_ref, a_t, b_t, g_t, c_s, c_a, c_b,
                *, heads, nkeys, topk, cells):
    tm = qp_ref.shape[0]
    dq = keys_ref.shape[2]
    iota_k = lax.broadcasted_iota(I32, (nkeys, tm), 0)
    ncell = len(cells)
    crow = c_s.shape[0]
    iota_c = lax.broadcasted_iota(I32, (crow, tm), 0)
    contract1 = (((1,), (1,)), ((), ()))
    c_s[pl.ds(ncell, crow - ncell), :] = jnp.full((crow - ncell, tm), -jnp.inf, F32)
    c_a[pl.ds(ncell, crow - ncell), :] = jnp.zeros((crow - ncell, tm), I32)
    c_b[pl.ds(ncell, crow - ncell), :] = jnp.zeros((crow - ncell, tm), I32)
    for h in range(heads):
        tops = []
        for p in range(2):
            c0 = (2 * h + p) * dq
            q = qp_ref[:, c0:c0 + dq].astype(BF16)
            s = lax.dot_general(keys_ref[p], q, contract1, preferred_element_type=F32)
            tops.append(_top_rows(s, topk, iota_k))
        (v1, i1), (v2, i2) = tops
        for ci, (k1, k2) in enumerate(cells):
            c_s[pl.ds(ci, 1), :] = v1[k1] + v2[k2]
            c_a[pl.ds(ci, 1), :] = i1[k1]
            c_b[pl.ds(ci, 1), :] = i2[k2]
        cs, ca, cb = c_s[...], c_a[...], c_b[...]
        best = []
        for kk in range(topk):
            m = jnp.max(cs, axis=0, keepdims=True)
            pos = jnp.min(jnp.where(cs == m, iota_c, crow), axis=0, keepdims=True)
            hit = iota_c == pos
            j = h * topk + kk
            a_t[pl.ds(j, 1), :] = jnp.sum(jnp.where(hit, ca, 0), axis=0, keepdims=True)
            b_t[pl.ds(j, 1), :] = jnp.sum(jnp.where(hit, cb, 0), axis=0, keepdims=True)
            best.append(m)
            cs = jnp.where(hit, -jnp.inf, cs)
        e = [jnp.exp(v - best[0]) for v in best]
        denom = e[0]
        for v in e[1:]:
            denom = denom + v
        for kk in range(topk):
            g_t[pl.ds(h * topk + kk, 1), :] = e[kk] / denom
    a_ref[...] = a_t[...].T
    b_ref[...] = b_t[...].T
    g_ref[...] = g_t[...].T


def _peer_route(qp, sub_keys, cfg):
    n = qp.shape[0]
    heads, nkeys, topk = cfg.peer_heads, cfg.peer_nkeys, cfg.peer_topk
    dq = sub_keys.shape[2]
    nj = heads * topk
    tm = min(cfg.topk_tm, n)
    assert nj == V7X_LANES and qp.shape[1] == heads * 2 * dq
    cells = _peer_cells(topk)
    crow = -(-(len(cells) + 1) // V7X_SUBLANES) * V7X_SUBLANES
    outs = pl.pallas_call(
        functools.partial(_route_body, heads=heads, nkeys=nkeys, topk=topk, cells=tuple(cells)),
        out_shape=[jax.ShapeDtypeStruct((n, nj), I32), jax.ShapeDtypeStruct((n, nj), I32),
                   jax.ShapeDtypeStruct((n, nj), F32)],
        grid=(n // tm,),
        in_specs=[pl.BlockSpec((tm, qp.shape[1]), lambda i: (i, 0)),
                  pl.BlockSpec(sub_keys.shape, lambda i: (0, 0, 0))],
        out_specs=[pl.BlockSpec((tm, nj), lambda i: (i, 0))] * 3,
        scratch_shapes=[pltpu.VMEM((nj, tm), I32), pltpu.VMEM((nj, tm), I32), pltpu.VMEM((nj, tm), F32),
                        pltpu.VMEM((crow, tm), F32), pltpu.VMEM((crow, tm), I32), pltpu.VMEM((crow, tm), I32)],
        compiler_params=pltpu.CompilerParams(dimension_semantics=("parallel",),
                                             vmem_limit_bytes=_vmem_limit(0)),
        name="peer_route",
    )(qp, sub_keys.astype(BF16))
    return outs


W_PITCH_PAD = 8
HIGH_HALF = np.uint32(0xFFFF0000)
HALF_ULP_BF16 = np.uint32(0x8000)


def _expert_body(hn_ref, a_ref, b_ref, g_ref, ut_ref, v_ref, x_ref, o_ref, wmap, acc, *, nk, pitch, unroll):
    j = pl.program_id(1)
    tm = hn_ref.shape[0]
    te = ut_ref.shape[1]
    nblk = te // nk
    half = nk // 2
    contract1 = (((1,), (1,)), ((), ()))

    @pl.when(j == 0)
    def _():
        acc[...] = jnp.zeros_like(acc)
        row = lax.broadcasted_iota(I32, (nk, a_ref.shape[1]), 0)
        a_of_row = jnp.where(row < half, 2 * row, 2 * (row - half) + 1)

        def build(n, carry):
            ar = a_ref[pl.ds(n, 1), :]
            br = b_ref[pl.ds(n, 1), :]
            gr = g_ref[pl.ds(n, 1), :] * 0.5
            one_a = jnp.where(a_of_row == ar, 1.0, 0.0).astype(BF16)
            gate_b = jnp.where(row == br, gr, 0.0).astype(BF16)
            wn = lax.dot_general(one_a, gate_b, contract1, preferred_element_type=F32)
            lo = pltpu.bitcast(wn[:half], jnp.uint32)
            hi = pltpu.bitcast(wn[half:], jnp.uint32)
            packed = ((hi + HALF_ULP_BF16) & HIGH_HALF) | ((lo + HALF_ULP_BF16) >> 16)
            wmap[pl.ds(pl.multiple_of(n * pitch, V7X_SUBLANES), half), :] = packed
            return carry

        lax.fori_loop(0, tm, build, 0, unroll=unroll)

    x = jnp.dot(hn_ref[...], ut_ref[...], preferred_element_type=F32)
    act = x * (1.0 + lax.erf(x * (1.0 / math.sqrt(2.0))))
    parts = []
    for i in range(nblk // 2):
        w32 = wmap[pl.ds(j * (nblk // 2) + i, tm, stride=pitch), :]
        w_even = pltpu.bitcast(w32 << 16, F32)
        w_odd = pltpu.bitcast(w32 & HIGH_HALF, F32)
        parts.append((act[:, (2 * i) * nk:(2 * i + 1) * nk] * w_even).astype(BF16))
        parts.append((act[:, (2 * i + 1) * nk:(2 * i + 2) * nk] * w_odd).astype(BF16))
    pmat = jnp.concatenate(parts, axis=1)
    acc[...] += jnp.dot(pmat, v_ref[...], preferred_element_type=F32)

    @pl.when(j == pl.num_programs(1) - 1)
    def _():
        o_ref[...] = x_ref[...] + acc[...]


def _peer_experts(hn, a, b, g, ut, v, x2, cfg):
    n, d = x2.shape
    nexp = ut.shape[1]
    nk = cfg.peer_nkeys
    tm = min(cfg.peer_tm, n)
    te = min(cfg.peer_te, nexp)
    assert nk == V7X_LANES and nexp == nk * nk and te % (2 * nk) == 0 and nexp % te == 0
    pitch = nk // 2 + W_PITCH_PAD
    nj = a.shape[1]
    est = 2 * (tm * d * 2 + 3 * tm * nj * 4 + 2 * d * te * 2 + 2 * tm * d * 4) \
        + tm * pitch * nk * 4 + tm * d * 4 + 4 * tm * te * 4
    tile = lambda width: pl.BlockSpec((tm, width), lambda i, j: (i, 0))
    return pl.pallas_call(
        functools.partial(_expert_body, nk=nk, pitch=pitch, unroll=min(cfg.peer_unroll, tm)),
        out_shape=jax.ShapeDtypeStruct((n, d), F32),
        grid=(n // tm, nexp // te),
        in_specs=[tile(d), tile(nj), tile(nj), tile(nj),
                  pl.BlockSpec((d, te), lambda i, j: (0, j)),
                  pl.BlockSpec((te, d), lambda i, j: (j, 0)),
                  tile(d)],
        out_specs=tile(d),
        scratch_shapes=[pltpu.VMEM((tm * pitch, nk), jnp.uint32), pltpu.VMEM((tm, d), F32)],
        compiler_params=pltpu.CompilerParams(dimension_semantics=("parallel", "arbitrary"),
                                             vmem_limit_bytes=_vmem_limit(est)),
        name="peer_experts",
    )(hn, a, b, g, ut, v, x2)


def _route_unit(qp_ref, keys_ref, a_dst, b_dst, g_dst, c_s, c_a, c_b, row0, *, nkeys, topk, cells):
    th = qp_ref.shape[0]
    dq = keys_ref.shape[2]
    ncell = len(cells)
    crow = c_s.shape[0]
    lw = V7X_LANES
    iota_k = lax.broadcasted_iota(I32, (nkeys, lw), 0)
    iota_c = lax.broadcasted_iota(I32, (crow, lw), 0)
    contract1 = (((1,), (1,)), ((), ()))
    scores = [lax.dot_general(keys_ref[p], qp_ref[:, p * dq:(p + 1) * dq], contract1,
                              preferred_element_type=F32) for p in range(2)]
    c_s[pl.ds(ncell, crow - ncell), :] = jnp.full((crow - ncell, th), -jnp.inf, F32)
    c_a[pl.ds(ncell, crow - ncell), :] = jnp.zeros((crow - ncell, th), I32)
    c_b[pl.ds(ncell, crow - ncell), :] = jnp.zeros((crow - ncell, th), I32)
    order = None
    for col in range(th // lw):
        cols = slice(col * lw, (col + 1) * lw)
        tops = []
        for p in range(2):
            s = scores[p][:, cols]
            if order is not None:
                s = s + order
            tops.append(_top_rows(s, topk, iota_k))
        (v1, i1), (v2, i2) = tops
        for ci, (k1, k2) in enumerate(cells):
            c_s[pl.ds(ci, 1), cols] = v1[k1] + v2[k2]
            c_a[pl.ds(ci, 1), cols] = i1[k1]
            c_b[pl.ds(ci, 1), cols] = i2[k2]
        cs, ca, cb = c_s[:, cols], c_a[:, cols], c_b[:, cols]
        best = []
        for kk in range(topk):
            m = jnp.max(cs, axis=0, keepdims=True)
            pos = jnp.min(jnp.where(cs == m, iota_c, crow), axis=0, keepdims=True)
            hit = iota_c == pos
            a_dst[col][pl.ds(row0 + kk, 1), :] = jnp.sum(jnp.where(hit, ca, 0), axis=0, keepdims=True)
            b_dst[col][pl.ds(row0 + kk, 1), :] = jnp.sum(jnp.where(hit, cb, 0), axis=0, keepdims=True)
            best.append(m)
            cs = jnp.where(hit, -jnp.inf, cs)
        e = [jnp.exp(v - best[0]) for v in best]
        denom = e[0]
        for v in e[1:]:
            denom = denom + v
        for kk in range(topk):
            gate = e[kk] / denom
            g_dst[col][pl.ds(row0 + kk, 1), :] = gate
        order = jnp.where(gate > 2.0, 1.0, 0.0)


def _peer_fused_body(hn_ref, qp_ref, keys_ref, ut_ref, v_ref, x_ref, o_ref,
                     wmap, acc, a_cur, b_cur, g_cur, a_t, b_t, g_t, c_s, c_a, c_b,
                     *, nk, pitch, unroll, topk, cells, nhalf):
    r = pl.program_id(0)
    j = pl.program_id(1)
    tm = hn_ref.shape[0]
    th = qp_ref.shape[0]
    te = ut_ref.shape[1]
    nblk = te // nk
    half = nk // 2
    contract1 = (((1,), (1,)), ((), ()))

    @pl.when(j == 0)
    def _():
        acc[...] = jnp.zeros_like(acc)

        @pl.when(r == 0)
        def _():
            a_cur[...] = jnp.zeros_like(a_cur)
            b_cur[...] = jnp.zeros_like(b_cur)
            g_cur[...] = jnp.zeros_like(g_cur)

        row = lax.broadcasted_iota(I32, (nk, a_cur.shape[1]), 0)
        a_of_row = jnp.where(row < half, 2 * row, 2 * (row - half) + 1)

        def build(n, carry):
            ar = a_cur[pl.ds(n, 1), :]
            br = b_cur[pl.ds(n, 1), :]
            gr = g_cur[pl.ds(n, 1), :] * 0.5
            one_a = jnp.where(a_of_row == ar, 1.0, 0.0).astype(BF16)
            gate_b = jnp.where(row == br, gr, 0.0).astype(BF16)
            wn = lax.dot_general(one_a, gate_b, contract1, preferred_element_type=F32)
            lo = pltpu.bitcast(wn[:half], jnp.uint32)
            hi = pltpu.bitcast(wn[half:], jnp.uint32)
            packed = ((hi + HALF_ULP_BF16) & HIGH_HALF) | ((lo + HALF_ULP_BF16) >> 16)
            wmap[pl.ds(pl.multiple_of(n * pitch, V7X_SUBLANES), half), :] = packed
            return carry

        lax.fori_loop(0, tm, build, 0, unroll=unroll)

    ncol = th // V7X_LANES
    blk0 = lax.rem(j, nhalf) * ncol
    row0 = lax.div(j, nhalf) * topk
    _route_unit(qp_ref, keys_ref, [a_t.at[blk0 + c] for c in range(ncol)], [b_t.at[blk0 + c] for c in range(ncol)],
                [g_t.at[blk0 + c] for c in range(ncol)], c_s, c_a, c_b, row0, nkeys=nk, topk=topk, cells=cells)

    x = jnp.dot(hn_ref[...], ut_ref[...], preferred_element_type=F32)
    act = x * (1.0 + lax.erf(x * (1.0 / math.sqrt(2.0))))
    parts = []
    for i in range(nblk // 2):
        w32 = wmap[pl.ds(j * (nblk // 2) + i, tm, stride=pitch), :]
        w_even = pltpu.bitcast(w32 << 16, F32)
        w_odd = pltpu.bitcast(w32 & HIGH_HALF, F32)
        parts.append((act[:, (2 * i) * nk:(2 * i + 1) * nk] * w_even).astype(BF16))
        parts.append((act[:, (2 * i + 1) * nk:(2 * i + 2) * nk] * w_odd).astype(BF16))
    pmat = jnp.concatenate(parts, axis=1)
    acc[...] += jnp.dot(pmat, v_ref[...], preferred_element_type=F32)

    @pl.when(j == pl.num_programs(1) - 1)
    def _():
        o_ref[...] = x_ref[...] + acc[...]
        for cb in range(tm // V7X_LANES):
            rows = slice(cb * V7X_LANES, (cb + 1) * V7X_LANES)
            a_cur[rows, :] = a_t[cb].T
            b_cur[rows, :] = b_t[cb].T
            g_cur[rows, :] = g_t[cb].T


def _peer_fused(hn, qp, sub_keys, ut, v, x2, cfg):
    n, d = x2.shape
    nexp = ut.shape[1]
    nk, heads, topk = cfg.peer_nkeys, cfg.peer_heads, cfg.peer_topk
    dq = sub_keys.shape[2]
    nj = heads * topk
    tm = min(cfg.peer_tm, n)
    te = min(cfg.peer_te, nexp)
    nsteps = nexp // te
    assert nk == V7X_LANES and nexp == nk * nk and te % (2 * nk) == 0 and nexp % te == 0
    assert nj == V7X_LANES and qp.shape[1] == heads * 2 * dq and n % tm == 0
    assert nsteps % heads == 0 and tm % (nsteps // heads) == 0
    nhalf = nsteps // heads
    th = tm // nhalf
    assert th % V7X_LANES == 0
    ntiles = n // tm
    pitch = nk // 2 + W_PITCH_PAD
    cells = _peer_cells(topk)
    crow = -(-(len(cells) + 1) // V7X_SUBLANES) * V7X_SUBLANES
    est = 2 * (tm * d * 2 + th * 2 * dq * 2 + 2 * d * te * 2 + 2 * tm * d * 4) \
        + tm * pitch * nk * 4 + tm * d * 4 + 6 * tm * nj * 4 + 3 * crow * th * 4 + 5 * tm * te * 4
    cur = lambda width: pl.BlockSpec((tm, width), lambda r, j: (jnp.maximum(r - 1, 0), 0))
    return pl.pallas_call(
        functools.partial(_peer_fused_body, nk=nk, pitch=pitch, unroll=min(cfg.peer_unroll, tm),
                          topk=topk, cells=tuple(cells), nhalf=nhalf),
        out_shape=jax.ShapeDtypeStruct((n, d), F32),
        grid=(ntiles + 1, nsteps),
        in_specs=[cur(d),
                  pl.BlockSpec((th, 2 * dq),
                               lambda r, j: (nhalf * jnp.minimum(r, ntiles - 1) + j % nhalf, j // nhalf)),
                  pl.BlockSpec(sub_keys.shape, lambda r, j: (0, 0, 0)),
                  pl.BlockSpec((d, te), lambda r, j: (0, j)),
                  pl.BlockSpec((te, d), lambda r, j: (j, 0)),
                  cur(d)],
        out_specs=cur(d),
        scratch_shapes=[pltpu.VMEM((tm * pitch, nk), jnp.uint32), pltpu.VMEM((tm, d), F32),
                        pltpu.VMEM((tm, nj), I32), pltpu.VMEM((tm, nj), I32), pltpu.VMEM((tm, nj), F32),
                        pltpu.VMEM((tm // V7X_LANES, nj, V7X_LANES), I32),
                        pltpu.VMEM((tm // V7X_LANES, nj, V7X_LANES), I32),
                        pltpu.VMEM((tm // V7X_LANES, nj, V7X_LANES), F32),
                        pltpu.VMEM((crow, th), F32), pltpu.VMEM((crow, th), I32), pltpu.VMEM((crow, th), I32)],
        compiler_params=pltpu.CompilerParams(dimension_semantics=("arbitrary", "arbitrary"),
                                             vmem_limit_bytes=_vmem_limit(est)),
        name="peer_fused",
    )(hn, qp, sub_keys.astype(BF16), ut, v, x2)


def _peer(x2, gain, w_q_bf16, sub_keys, ut, v, cfg):
    nq = w_q_bf16.shape[1]
    qp, hn = _norm_matmul(x2, gain, w_q_bf16, [(0, nq)], [BF16], cfg, emit_hn=True)
    a, b, g = _peer_route(qp, sub_keys, cfg)
    return _peer_experts(hn, a, b, g, ut, v, x2, cfg)


def _trunk(x, p, cfg):
    b, t, d = x.shape
    x2 = x.reshape(b * t, d)
    depth = p["norm_mix"].shape[0]
    for l in range(depth):
        i = l // 2
        if l % 2 == 0:
            rw = 4 * cfg.ret_heads * cfg.ret_dk
            w_in = p["ab_w_in"][i]
            zq, u = _norm_matmul(x2, p["norm_mix"][l], w_in, [(0, rw), (rw, w_in.shape[1])], [F32, F32], cfg)
            ret = _retention(zq, p["ab_ret_decay"][i], p["ab_ret_gn"][i], b, t, cfg)
            ssm = _s5(u, tuple(p[k][i] for k in ("ab_s5_lam_re", "ab_s5_lam_im", "ab_s5_log_dt", "ab_s5_b_re",
                                                  "ab_s5_b_im", "ab_s5_c_re", "ab_s5_c_im", "ab_s5_d",
                                                  "ab_s5_glu_w", "ab_s5_glu_b")), b, t, cfg)
            w_out = p["ab_w_out"][i]
            nr = ret.shape[1]
            x2 = _matmul_residual([ret, ssm], [w_out[:nr], w_out[nr:]], x2, cfg)
        else:
            w_qkv = p["na_w_qkv"][i]
            (z,) = _norm_matmul(x2, p["norm_mix"][l], w_qkv, [(0, w_qkv.shape[1])], [F32], cfg)
            att = _neighbourhood_attention(z, p["na_q_gain"][i], p["na_k_gain"][i], p["na_rpb"][i], b, t, cfg)
            x2 = _matmul_residual([att], [p["na_w_o"][i]], x2, cfg)
        x2 = _peer(x2, p["norm_ffn"][l], p["peer_w_q"][l], p["peer_sub_keys"][l], p["peer_ut"][l],
                   p["peer_v"][l], cfg)
    return x2.reshape(b, t, d)


def _prepare(params):
    p = dict(params)
    for k in ("ab_w_in", "ab_w_out", "na_w_qkv", "na_w_o", "peer_w_q", "peer_v"):
        p[k] = params[k].astype(BF16)
    p["peer_ut"] = jnp.swapaxes(params["peer_u"], 1, 2).astype(BF16)
    return p


def _forward(x_prompt, x_sample, params, cfg=Cfg()):
    p = _prepare(params)
    return _trunk(x_prompt, p, cfg), _trunk(x_sample, p, cfg)


def kernel(x_prompt, x_sample, norm_mix, norm_ffn, ab_w_in, ab_ret_decay, ab_ret_gn, ab_s5_lam_re, ab_s5_lam_im, ab_s5_log_dt, ab_s5_b_re, ab_s5_b_im, ab_s5_c_re, ab_s5_c_im, ab_s5_d, ab_s5_glu_w, ab_s5_glu_b, ab_w_out, na_w_qkv, na_q_gain, na_k_gain, na_rpb, na_w_o, peer_w_q, peer_sub_keys, peer_u, peer_v):
    params = dict(norm_mix=norm_mix, norm_ffn=norm_ffn, ab_w_in=ab_w_in, ab_ret_decay=ab_ret_decay,
                  ab_ret_gn=ab_ret_gn, ab_s5_lam_re=ab_s5_lam_re, ab_s5_lam_im=ab_s5_lam_im,
                  ab_s5_log_dt=ab_s5_log_dt, ab_s5_b_re=ab_s5_b_re, ab_s5_b_im=ab_s5_b_im,
                  ab_s5_c_re=ab_s5_c_re, ab_s5_c_im=ab_s5_c_im, ab_s5_d=ab_s5_d, ab_s5_glu_w=ab_s5_glu_w,
                  ab_s5_glu_b=ab_s5_glu_b, ab_w_out=ab_w_out, na_w_qkv=na_w_qkv, na_q_gain=na_q_gain,
                  na_k_gain=na_k_gain, na_rpb=na_rpb, na_w_o=na_w_o, peer_w_q=peer_w_q,
                  peer_sub_keys=peer_sub_keys, peer_u=peer_u, peer_v=peer_v)
    return _forward(x_prompt, x_sample, params)
```

```python
import functools
import math
from typing import NamedTuple

import numpy as np
import jax
import jax.numpy as jnp
from jax import lax
from jax.experimental import pallas as pl
from jax.experimental.pallas import tpu as pltpu

F32 = jnp.float32
BF16 = jnp.bfloat16
I32 = jnp.int32

V7X_LANES = 128
V7X_SUBLANES = 8
V7X_VMEM_BYTES = 64 * 2**20
VMEM_LIMIT_CAP = V7X_VMEM_BYTES - 8 * 2**20

NEG_BIG = -1e30


class Cfg(NamedTuple):
    eps: float = 1e-6
    grid_w: int = 64
    ret_heads: int = 4
    ret_dk: int = 128
    ret_chunk: int = 128
    rope_base: float = 10000.0
    s5_group: int = 16
    s5_state: int = 64
    s5_chunk: int = 16
    na_heads: int = 16
    na_kh: int = 8
    na_kw: int = 16
    peer_heads: int = 8
    peer_nkeys: int = 128
    peer_topk: int = 16
    tm: int = 512
    peer_tm: int = 512
    peer_unroll: int = 32
    na_unroll: int = 4
    peer_te: int = 1024
    topk_tm: int = 256


def _vmem_limit(nbytes):
    return int(min(VMEM_LIMIT_CAP, max(32 * 2**20, nbytes)))


def _gelu(x):
    return 0.5 * x * (1.0 + lax.erf(x * (1.0 / math.sqrt(2.0))))


def _sigmoid(x):
    return 1.0 / (1.0 + jnp.exp(-x))


def _norm_mm_body(x_ref, g_ref, w_ref, *out_refs, splits, emit_hn, eps):
    x = x_ref[...]
    y = x * lax.rsqrt(jnp.mean(x * x, axis=-1, keepdims=True) + eps) * g_ref[...]
    yb = y.astype(BF16)
    z = jnp.dot(yb, w_ref[...], preferred_element_type=F32)
    for r, (s, e) in zip(out_refs, splits):
        r[...] = z[:, s:e].astype(r.dtype)
    if emit_hn:
        out_refs[len(splits)][...] = yb


def _norm_matmul(x2, gain, w_bf16, splits, dtypes, cfg, emit_hn=False):
    n, d = x2.shape
    nout = w_bf16.shape[1]
    tm = min(cfg.tm, n)
    out_shape = [jax.ShapeDtypeStruct((n, e - s), dt) for (s, e), dt in zip(splits, dtypes)]
    out_specs = [pl.BlockSpec((tm, e - s), lambda i: (i, 0)) for (s, e) in splits]
    if emit_hn:
        out_shape.append(jax.ShapeDtypeStruct((n, d), BF16))
        out_specs.append(pl.BlockSpec((tm, d), lambda i: (i, 0)))
    est = 2 * (tm * d * 4 + d * nout * 2 + tm * nout * 4 + tm * d * 2) + 2 * tm * nout * 4
    return pl.pallas_call(
        functools.partial(_norm_mm_body, splits=tuple(splits), emit_hn=emit_hn, eps=cfg.eps),
        out_shape=out_shape,
        grid=(n // tm,),
        in_specs=[pl.BlockSpec((tm, d), lambda i: (i, 0)),
                  pl.BlockSpec((1, d), lambda i: (0, 0)),
                  pl.BlockSpec((d, nout), lambda i: (0, 0))],
        out_specs=out_specs,
        compiler_params=pltpu.CompilerParams(dimension_semantics=("parallel",),
                                             vmem_limit_bytes=_vmem_limit(est)),
        name="norm_matmul",
    )(x2, gain.reshape(1, d).astype(F32), w_bf16)


def _mm_res_body(*refs, n_in):
    acc = refs[2 * n_in][...]
    for a, w in zip(refs[:n_in], refs[n_in:2 * n_in]):
        acc = acc + jnp.dot(a[...], w[...], preferred_element_type=F32)
    refs[-1][...] = acc


def _matmul_residual(a_list, w_list, res, cfg):
    n, d = res.shape
    tm = min(cfg.tm, n)
    n_in = len(a_list)
    in_specs = ([pl.BlockSpec((tm, a.shape[1]), lambda i: (i, 0)) for a in a_list]
                + [pl.BlockSpec(w.shape, lambda i: (0, 0)) for w in w_list]
                + [pl.BlockSpec((tm, d), lambda i: (i, 0))])
    est = 2 * (sum(tm * a.shape[1] * 2 for a in a_list) + sum(w.size * 2 for w in w_list)
               + 2 * tm * d * 4) + 2 * tm * d * 4
    return pl.pallas_call(
        functools.partial(_mm_res_body, n_in=n_in),
        out_shape=jax.ShapeDtypeStruct((n, d), F32),
        grid=(n // tm,),
        in_specs=in_specs,
        out_specs=pl.BlockSpec((tm, d), lambda i: (i, 0)),
        compiler_params=pltpu.CompilerParams(dimension_semantics=("parallel",),
                                             vmem_limit_bytes=_vmem_limit(est)),
        name="matmul_residual",
    )(*a_list, *w_list, res)


def _ret_tables(ret_decay, chunk, width):
    lg = -jax.nn.softplus(-ret_decay.astype(F32))
    pos = jnp.arange(chunk, dtype=F32)
    diff = pos[:, None] - pos[None, :]
    d_f = jnp.where(diff >= 0, jnp.exp(lg[0][:, None, None] * jnp.maximum(diff, 0.0)[None]), 0.0)
    d_b = jnp.where(diff < 0, jnp.exp(lg[1][:, None, None] * jnp.maximum(-diff, 0.0)[None]), 0.0)
    cols = [jnp.exp(lg[0][:, None] * (chunk - 1.0 - pos)[None]),
            jnp.exp(lg[1][:, None] * pos[None]),
            jnp.exp(lg[0][:, None] * (pos + 1.0)[None]),
            jnp.exp(lg[1][:, None] * (chunk - pos)[None]),
            jnp.broadcast_to(jnp.exp(lg[0] * chunk)[:, None], (lg.shape[1], chunk)),
            jnp.broadcast_to(jnp.exp(lg[1] * chunk)[:, None], (lg.shape[1], chunk))]
    tab = jnp.stack(cols, axis=1)
    return d_f + d_b, jnp.broadcast_to(tab[..., None], tab.shape + (width,))


def _rope_tables(t, half, base):
    inv = base ** (-jnp.arange(half, dtype=F32) / half)
    ang = jnp.arange(t, dtype=F32)[:, None] * inv[None, :]
    cos, sin = jnp.cos(ang), jnp.sin(ang)
    return jnp.concatenate([cos, cos], axis=1), jnp.concatenate([-sin, sin], axis=1)


def _ret_body(q_ref, k_ref, v_ref, g_ref, cos_ref, sin_ref, d_ref, tab_ref, gn_ref, o_ref,
              qs, ks, sb, *, chunk, nc, kscale, eps):
    dk = q_ref.shape[1]
    half = dk // 2
    cos, sin = cos_ref[...], sin_ref[...]
    q = q_ref[...]
    qs[...] = q * cos + pltpu.roll(q, half, 1) * sin
    k = k_ref[...]
    ks[...] = (k * cos + pltpu.roll(k, half, 1) * sin) * kscale
    contract0 = (((0,), (0,)), ((), ()))
    contract1 = (((1,), (1,)), ((), ()))

    k_f, k_b, q_f, q_b = tab_ref[0, 0], tab_ref[0, 1], tab_ref[0, 2], tab_ref[0, 3]
    cd_f, cd_b = tab_ref[0, 4], tab_ref[0, 5]

    def bwd(i, state):
        n = nc - 1 - i
        r0 = pl.multiple_of(n * chunk, chunk)
        sb[n] = state
        kc = ks[pl.ds(r0, chunk), :]
        vc = v_ref[pl.ds(r0, chunk), :].astype(BF16)
        kv = lax.dot_general((kc * k_b).astype(BF16), vc, contract0, preferred_element_type=F32)
        return state * cd_b + kv

    lax.fori_loop(0, nc, bwd, jnp.zeros((dk, v_ref.shape[1]), F32))

    def fwd(n, state):
        r0 = pl.multiple_of(n * chunk, chunk)
        qc = qs[pl.ds(r0, chunk), :]
        kc = ks[pl.ds(r0, chunk), :]
        vc = v_ref[pl.ds(r0, chunk), :].astype(BF16)
        s = lax.dot_general(qc.astype(BF16), kc.astype(BF16), contract1, preferred_element_type=F32)
        o = jnp.dot((s * d_ref[0]).astype(BF16), vc, preferred_element_type=F32)
        o = o + jnp.dot((qc * q_f).astype(BF16), state.astype(BF16), preferred_element_type=F32)
        o = o + jnp.dot((qc * q_b).astype(BF16), sb[n].astype(BF16), preferred_element_type=F32)
        oc = o - jnp.mean(o, axis=-1, keepdims=True)
        o = oc * lax.rsqrt(jnp.mean(oc * oc, axis=-1, keepdims=True) + eps)
        g = g_ref[pl.ds(r0, chunk), :]
        o_ref[pl.ds(r0, chunk), :] = (o * gn_ref[...] * (g * _sigmoid(g))).astype(o_ref.dtype)
        kv = lax.dot_general((kc * k_f).astype(BF16), vc, contract0, preferred_element_type=F32)
        return state * cd_f + kv

    lax.fori_loop(0, nc, fwd, jnp.zeros((dk, v_ref.shape[1]), F32))


def _retention(zq, ret_decay, ret_gn, b, t, cfg):
    h, dk, c = cfg.ret_heads, cfg.ret_dk, cfg.ret_chunk
    assert t % c == 0 and zq.shape[1] == 4 * h * dk
    nc = t // c
    dmat, tab = _ret_tables(ret_decay, c, dk)
    cos2, sin2 = _rope_tables(t, dk // 2, cfg.rope_base)
    blk = lambda off: pl.BlockSpec((t, dk), lambda bi, hi, off=off: (bi, off + hi))
    est = 2 * (4 * t * dk * 4 + 2 * t * dk * 4 + c * c * 4 + 6 * c * dk * 4 + t * dk * 2) \
        + 2 * t * dk * 4 + nc * dk * dk * 4 + 8 * t * dk * 4
    return pl.pallas_call(
        functools.partial(_ret_body, chunk=c, nc=nc, kscale=dk ** -0.5, eps=cfg.eps),
        out_shape=jax.ShapeDtypeStruct((b * t, h * dk), BF16),
        grid=(b, h),
        in_specs=[blk(0), blk(h), blk(2 * h), blk(3 * h),
                  pl.BlockSpec((t, dk), lambda bi, hi: (0, 0)),
                  pl.BlockSpec((t, dk), lambda bi, hi: (0, 0)),
                  pl.BlockSpec((1, c, c), lambda bi, hi: (hi, 0, 0)),
                  pl.BlockSpec((1, 6, c, dk), lambda bi, hi: (hi, 0, 0, 0)),
                  pl.BlockSpec((1, dk), lambda bi, hi: (0, hi))],
        out_specs=pl.BlockSpec((t, dk), lambda bi, hi: (bi, hi)),
        scratch_shapes=[pltpu.VMEM((t, dk), F32), pltpu.VMEM((t, dk), F32),
                        pltpu.VMEM((nc, dk, dk), F32)],
        compiler_params=pltpu.CompilerParams(dimension_semantics=("parallel", "parallel"),
                                             vmem_limit_bytes=_vmem_limit(est)),
        name="retention",
    )(zq, zq, zq, zq, cos2, sin2, dmat, tab, ret_gn.reshape(1, h * dk).astype(F32))


def _s5_tables(lam_re, lam_im, log_dt, b_re, b_im, c_re, c_im, d_skip, glu_w, glu_b, L, nsteps):
    lam = lax.complex(lam_re.astype(F32), lam_im.astype(F32))
    ldt = lam * jnp.exp(log_dt.astype(F32))[..., None]
    lam_bar = jnp.exp(ldt)
    bmat = lax.complex(b_re.astype(F32), b_im.astype(F32))
    b_bar = ((lam_bar - 1.0) / lam)[..., None] * bmat[None]
    cmat = lax.complex(c_re.astype(F32), c_im.astype(F32))
    g, p, cg = bmat.shape
    tau = jnp.arange(L + 1, dtype=F32)
    pw = jnp.exp(ldt[:, :, None, :] * tau[None, None, :, None])
    kern = jnp.real(jnp.einsum('dgop,dgtp,dgpi->dgtoi', cmat, pw[:, :, :L], b_bar))
    li = np.arange(L)
    lag = li[None, :] - li[:, None]
    k_f = jnp.where((lag >= 0)[None, :, :, None, None], kern[0][:, np.clip(lag, 0, L - 1)], 0.0)
    k_b = jnp.where((lag <= 0)[None, :, :, None, None], kern[1][:, np.clip(-lag, 0, L - 1)], 0.0)
    m = (k_f + k_b).transpose(0, 1, 4, 2, 3).reshape(g, L * cg, L * cg)

    def cat(z):
        return jnp.concatenate([jnp.real(z), jnp.imag(z)], axis=-1)

    inc_f = cat(jnp.einsum('glp,gpi->glip', pw[0][:, L - 1 - li], b_bar[0])).reshape(g, L * cg, 2 * p)
    inc_b = cat(jnp.einsum('glp,gpi->glip', pw[1][:, li], b_bar[1])).reshape(g, L * cg, 2 * p)

    def out_mat(z):
        return jnp.concatenate([jnp.real(z), -jnp.imag(z)], axis=1).reshape(g, 2 * p, L * cg)

    out_f = out_mat(jnp.einsum('gop,glp->gplo', cmat[0], pw[0][:, li + 1]))
    out_b = out_mat(jnp.einsum('gop,glp->gplo', cmat[1], pw[1][:, L - li]))
    eye = jnp.eye(L, dtype=F32)
    glu = jnp.einsum('lm,gce->glcme', eye, glu_w.astype(F32)).reshape(g, L * cg, L * cg)
    vecs = jnp.stack([jnp.tile(d_skip.astype(F32), (1, L)), jnp.tile(glu_b.astype(F32), (1, L))], axis=1)
    steps = (2.0 ** jnp.arange(nsteps, dtype=F32)) * L
    a = jnp.exp(ldt[:, :, None, :] * steps[None, None, :, None])
    scan = jnp.stack([jnp.concatenate([jnp.real(a), jnp.real(a)], -1),
                      jnp.concatenate([-jnp.imag(a), jnp.imag(a)], -1)], axis=3)
    scan = scan.transpose(1, 0, 2, 3, 4).reshape(g, 2 * nsteps * 2, 2 * p)
    return (m.astype(BF16), inc_f.astype(BF16), inc_b.astype(BF16), out_f.astype(BF16),
            out_b.astype(BF16), glu.astype(BF16), vecs, scan)


def _s5_body(u_ref, m_ref, incf_ref, incb_ref, outf_ref, outb_ref, glu_ref, vec_ref, scan_ref, o_ref,
             *, ncs, nsteps):
    u = u_ref[0]
    rows = u.shape[0]
    ub = u.astype(BF16)
    y = jnp.dot(ub, m_ref[0], preferred_element_type=F32)
    xf = jnp.dot(ub, incf_ref[0], preferred_element_type=F32)
    xb = jnp.dot(ub, incb_ref[0], preferred_element_type=F32)
    p2 = xf.shape[1]
    cidx = lax.rem(lax.broadcasted_iota(I32, (rows, 1), 0), ncs)
    for kk in range(nsteps):
        s = 1 << kk
        a_f, b_f = scan_ref[0, pl.ds(2 * kk, 1), :], scan_ref[0, pl.ds(2 * kk + 1, 1), :]
        a_b = scan_ref[0, pl.ds(2 * nsteps + 2 * kk, 1), :]
        b_b = scan_ref[0, pl.ds(2 * nsteps + 2 * kk + 1, 1), :]
        pf = pltpu.roll(xf, s, 0)
        xf = xf + jnp.where(cidx >= s, a_f * pf + b_f * pltpu.roll(pf, p2 // 2, 1), 0.0)
        pb = pltpu.roll(xb, rows - s, 0)
        xb = xb + jnp.where(cidx < ncs - s, a_b * pb + b_b * pltpu.roll(pb, p2 // 2, 1), 0.0)
    x_prev = jnp.where(cidx >= 1, pltpu.roll(xf, 1, 0), 0.0)
    x_next = jnp.where(cidx < ncs - 1, pltpu.roll(xb, rows - 1, 0), 0.0)
    y = y + jnp.dot(x_prev.astype(BF16), outf_ref[0], preferred_element_type=F32)
    y = y + jnp.dot(x_next.astype(BF16), outb_ref[0], preferred_element_type=F32)
    y = y + u * vec_ref[0, pl.ds(0, 1), :]
    yg = _gelu(y)
    z = jnp.dot(yg.astype(BF16), glu_ref[0], preferred_element_type=F32) + vec_ref[0, pl.ds(1, 1), :]
    o_ref[0] = (yg * _sigmoid(z)).astype(o_ref.dtype)


def _s5(u, params, b, t, cfg):
    lam_re, lam_im, log_dt, b_re, b_im, c_re, c_im, d_skip, glu_w, glu_b = params
    L, cg = cfg.s5_chunk, cfg.s5_group
    n, ch = u.shape
    g = ch // cg
    assert t % L == 0
    ncs = t // L
    nsteps = max(1, (ncs - 1).bit_length())
    tabs = _s5_tables(lam_re, lam_im, log_dt, b_re, b_im, c_re, c_im, d_skip, glu_w, glu_b, L, nsteps)
    uc = u.reshape(n // L, L, g, cg).transpose(2, 0, 1, 3).reshape(g, n // L, L * cg)
    seqs = max(1, min(b, 1024 // ncs))
    while b % seqs:
        seqs -= 1
    rows = seqs * ncs
    w = L * cg
    p2 = 2 * cfg.s5_state
    per_g = lambda shape: pl.BlockSpec((1,) + shape, lambda gi, ri: (gi, 0, 0))
    est = 2 * (rows * w * 4 + rows * w * 2) + 2 * 2 * (3 * w * w + 4 * w * p2) + 12 * rows * w * 4
    out = pl.pallas_call(
        functools.partial(_s5_body, ncs=ncs, nsteps=nsteps),
        out_shape=jax.ShapeDtypeStruct((g, n // L, w), BF16),
        grid=(g, (n // L) // rows),
        in_specs=[pl.BlockSpec((1, rows, w), lambda gi, ri: (gi, ri, 0)),
                  per_g((w, w)), per_g((w, p2)), per_g((w, p2)), per_g((p2, w)), per_g((p2, w)),
                  per_g((w, w)), per_g((2, w)), per_g((4 * nsteps, p2))],
        out_specs=pl.BlockSpec((1, rows, w), lambda gi, ri: (gi, ri, 0)),
        compiler_params=pltpu.CompilerParams(dimension_semantics=("parallel", "parallel"),
                                             vmem_limit_bytes=_vmem_limit(est)),
        name="s5",
    )(uc, *tabs)
    return out.reshape(g, n // L, L, cg).transpose(1, 2, 0, 3).reshape(n, ch)


def _na_bias_tables(rpb, cfg, rows):
    kh, kw, w = cfg.na_kh, cfg.na_kw, cfg.grid_w
    j = np.arange(w)
    c_start = np.clip(j - kw // 2, 0, w - kw)
    c = np.arange(w)
    inside = (c[None, :] >= c_start[:, None]) & (c[None, :] < c_start[:, None] + kw)
    by_row = jnp.stack([rpb.astype(F32)[:, kh - 1 - dl:2 * kh - 1 - dl] for dl in range(kh)], axis=1)
    padded = jnp.pad(by_row, ((0, 0), (0, 0), (0, 0), (w - kw, w - kw)))
    bias = jnp.stack([padded[..., w - 1 - jq:2 * w - 1 - jq] for jq in range(w)], axis=2)
    bias = jnp.where(inside[None, None, :, None, :], bias, NEG_BIG)
    return bias.reshape(rpb.shape[0], kh, w, kh * w)


def _na_body(q_ref, k_ref, v_ref, qg_ref, kg_ref, tab_ref, o_ref, qn, kn, vb, *, w, rows, kh, dh, eps,
             unroll):
    lo = lax.broadcasted_iota(I32, (1, 2 * dh), 1) < dh

    def head_norm(x, gain):
        x2 = x * x
        s_lo = jnp.sum(jnp.where(lo, x2, 0.0), axis=-1, keepdims=True)
        s_hi = jnp.sum(jnp.where(lo, 0.0, x2), axis=-1, keepdims=True)
        ms = jnp.where(lo, s_lo, s_hi) * (1.0 / dh)
        return x * lax.rsqrt(ms + eps) * gain

    qh = (head_norm(q_ref[...], qg_ref[...]) * dh ** -0.5).astype(BF16)
    kh_ = head_norm(k_ref[...], kg_ref[...]).astype(BF16)
    vh = v_ref[...].astype(BF16)
    for hh in range(2):
        sl = slice(hh * dh, (hh + 1) * dh)
        qn[hh], kn[hh], vb[hh] = qh[:, sl], kh_[:, sl], vh[:, sl]
    contract1 = (((1,), (1,)), ((), ()))

    def group(gi, carry):
        chains = []
        for u in range(unroll):
            r = gi * unroll + u
            rs = jnp.clip(r - kh // 2, 0, rows - kh)
            q0 = pl.multiple_of(r * w, w)
            k0 = pl.multiple_of(rs * w, w)
            for hh in range(2):
                s = lax.dot_general(qn[hh, pl.ds(q0, w), :], kn[hh, pl.ds(k0, kh * w), :], contract1,
                                    preferred_element_type=F32)
                chains.append((s + tab_ref[hh, r - rs], hh, k0, q0))
        outs = []
        for s, hh, k0, q0 in chains:
            p = jnp.exp(s - jnp.max(s, axis=-1, keepdims=True))
            p = p / jnp.sum(p, axis=-1, keepdims=True)
            outs.append(jnp.dot(p.astype(BF16), vb[hh, pl.ds(k0, kh * w), :], preferred_element_type=F32))
        for u in range(unroll):
            q0 = chains[2 * u][3]
            o_ref[pl.ds(q0, w), :] = jnp.concatenate(outs[2 * u:2 * u + 2], axis=1).astype(o_ref.dtype)
        return carry

    lax.fori_loop(0, rows // unroll, group, 0)


def _neighbourhood_attention(z, q_gain, k_gain, rpb, b, t, cfg):
    h, w, kh = cfg.na_heads, cfg.grid_w, cfg.na_kh
    dh = z.shape[1] // (3 * h)
    rows = t // w
    assert t % w == 0 and rows >= kh and h % 2 == 0 and 2 * dh == V7X_LANES
    tab = _na_bias_tables(rpb, cfg, rows)
    hp = h // 2
    blk = lambda off: pl.BlockSpec((t, 2 * dh), lambda bi, pi, off=off: (bi, off + pi))
    gain = pl.BlockSpec((1, 2 * dh), lambda bi, pi: (0, pi))
    est = 2 * (3 * t * 2 * dh * 4 + 2 * kh * w * kh * w * 4 + t * 2 * dh * 2) + 3 * 2 * t * V7X_LANES * 2 \
        + 6 * t * 2 * dh * 4
    return pl.pallas_call(
        functools.partial(_na_body, w=w, rows=rows, kh=kh, dh=dh, eps=cfg.eps,
                          unroll=math.gcd(cfg.na_unroll, rows)),
        out_shape=jax.ShapeDtypeStruct((b * t, h * dh), BF16),
        grid=(b, hp),
        in_specs=[blk(0), blk(hp), blk(2 * hp), gain, gain,
                  pl.BlockSpec((2, kh, w, kh * w), lambda bi, pi: (pi, 0, 0, 0))],
        out_specs=pl.BlockSpec((t, 2 * dh), lambda bi, pi: (bi, pi)),
        scratch_shapes=[pltpu.VMEM((2, t, dh), BF16)] * 3,
        compiler_params=pltpu.CompilerParams(dimension_semantics=("parallel", "parallel"),
                                             vmem_limit_bytes=_vmem_limit(est)),
        name="neighbourhood_attention",
    )(z, z, z, q_gain.reshape(1, h * dh).astype(F32), k_gain.reshape(1, h * dh).astype(F32), tab)


def _peer_cells(topk):
    return [(k1, k2) for k1 in range(topk) for k2 in range(topk) if (k1 + 1) * (k2 + 1) <= topk]


def _top_rows(s, k, iota):
    nrows = s.shape[0]
    vals, idxs = [], []
    for _ in range(k):
        m = jnp.max(s, axis=0, keepdims=True)
        idx = jnp.min(jnp.where(s == m, iota, nrows), axis=0, keepdims=True)
        vals.append(m)
        idxs.append(idx)
        s = jnp.where(iota == idx, -jnp.inf, s)
    return vals, idxs


def _route_body(qp_ref, keys_ref, a_ref, b_ref, g_ref, a_t, b_t, g_t, c_s, c_a, c_b,
                *, heads, nkeys, topk, cells):
    tm = qp_ref.shape[0]
    dq = keys_ref.shape[2]
    iota_k = lax.broadcasted_iota(I32, (nkeys, tm), 0)
    ncell = len(cells)
    crow = c_s.shape[0]
    iota_c = lax.broadcasted_iota(I32, (crow, tm), 0)
    contract1 = (((1,), (1,)), ((), ()))
    c_s[pl.ds(ncell, crow - ncell), :] = jnp.full((crow - ncell, tm), -jnp.inf, F32)
    c_a[pl.ds(ncell, crow - ncell), :] = jnp.zeros((crow - ncell, tm), I32)
    c_b[pl.ds(ncell, crow - ncell), :] = jnp.zeros((crow - ncell, tm), I32)
    for h in range(heads):
        tops = []
        for p in range(2):
            c0 = (2 * h + p) * dq
            q = qp_ref[:, c0:c0 + dq].astype(BF16)
            s = lax.dot_general(keys_ref[p], q, contract1, preferred_element_type=F32)
            tops.append(_top_rows(s, topk, iota_k))
        (v1, i1), (v2, i2) = tops
        for ci, (k1, k2) in enumerate(cells):
            c_s[pl.ds(ci, 1), :] = v1[k1] + v2[k2]
            c_a[pl.ds(ci, 1), :] = i1[k1]
            c_b[pl.ds(ci, 1), :] = i2[k2]
        cs, ca, cb = c_s[...], c_a[...], c_b[...]
        best = []
        for kk in range(topk):
            m = jnp.max(cs, axis=0, keepdims=True)
            pos = jnp.min(jnp.where(cs == m, iota_c, crow), axis=0, keepdims=True)
            hit = iota_c == pos
            j = h * topk + kk
            a_t[pl.ds(j, 1), :] = jnp.sum(jnp.where(hit, ca, 0), axis=0, keepdims=True)
            b_t[pl.ds(j, 1), :] = jnp.sum(jnp.where(hit, cb, 0), axis=0, keepdims=True)
            best.append(m)
            cs = jnp.where(hit, -jnp.inf, cs)
        e = [jnp.exp(v - best[0]) for v in best]
        denom = e[0]
        for v in e[1:]:
            denom = denom + v
        for kk in range(topk):
            g_t[pl.ds(h * topk + kk, 1), :] = e[kk] / denom
    a_ref[...] = a_t[...].T
    b_ref[...] = b_t[...].T
    g_ref[...] = g_t[...].T


def _peer_route(qp, sub_keys, cfg):
    n = qp.shape[0]
    heads, nkeys, topk = cfg.peer_heads, cfg.peer_nkeys, cfg.peer_topk
    dq = sub_keys.shape[2]
    nj = heads * topk
    tm = min(cfg.topk_tm, n)
    assert nj == V7X_LANES and qp.shape[1] == heads * 2 * dq
    cells = _peer_cells(topk)
    crow = -(-(len(cells) + 1) // V7X_SUBLANES) * V7X_SUBLANES
    outs = pl.pallas_call(
        functools.partial(_route_body, heads=heads, nkeys=nkeys, topk=topk, cells=tuple(cells)),
        out_shape=[jax.ShapeDtypeStruct((n, nj), I32), jax.ShapeDtypeStruct((n, nj), I32),
                   jax.ShapeDtypeStruct((n, nj), F32)],
        grid=(n // tm,),
        in_specs=[pl.BlockSpec((tm, qp.shape[1]), lambda i: (i, 0)),
                  pl.BlockSpec(sub_keys.shape, lambda i: (0, 0, 0))],
        out_specs=[pl.BlockSpec((tm, nj), lambda i: (i, 0))] * 3,
        scratch_shapes=[pltpu.VMEM((nj, tm), I32), pltpu.VMEM((nj, tm), I32), pltpu.VMEM((nj, tm), F32),
                        pltpu.VMEM((crow, tm), F32), pltpu.VMEM((crow, tm), I32), pltpu.VMEM((crow, tm), I32)],
        compiler_params=pltpu.CompilerParams(dimension_semantics=("parallel",),
                                             vmem_limit_bytes=_vmem_limit(0)),
        name="peer_route",
    )(qp, sub_keys.astype(BF16))
    return outs


W_PITCH_PAD = 8
HIGH_HALF = np.uint32(0xFFFF0000)
HALF_ULP_BF16 = np.uint32(0x8000)


def _expert_body(hn_ref, a_ref, b_ref, g_ref, ut_ref, v_ref, x_ref, o_ref, wmap, acc, *, nk, pitch, unroll):
    j = pl.program_id(1)
    tm = hn_ref.shape[0]
    te = ut_ref.shape[1]
    nblk = te // nk
    half = nk // 2
    contract1 = (((1,), (1,)), ((), ()))

    @pl.when(j == 0)
    def _():
        acc[...] = jnp.zeros_like(acc)
        row = lax.broadcasted_iota(I32, (nk, a_ref.shape[1]), 0)
        a_of_row = jnp.where(row < half, 2 * row, 2 * (row - half) + 1)

        def build(n, carry):
            ar = a_ref[pl.ds(n, 1), :]
            br = b_ref[pl.ds(n, 1), :]
            gr = g_ref[pl.ds(n, 1), :] * 0.5
            one_a = jnp.where(a_of_row == ar, 1.0, 0.0).astype(BF16)
            gate_b = jnp.where(row == br, gr, 0.0).astype(BF16)
            wn = lax.dot_general(one_a, gate_b, contract1, preferred_element_type=F32)
            lo = pltpu.bitcast(wn[:half], jnp.uint32)
            hi = pltpu.bitcast(wn[half:], jnp.uint32)
            packed = ((hi + HALF_ULP_BF16) & HIGH_HALF) | ((lo + HALF_ULP_BF16) >> 16)
            wmap[pl.ds(pl.multiple_of(n * pitch, V7X_SUBLANES), half), :] = packed
            return carry

        lax.fori_loop(0, tm, build, 0, unroll=unroll)

    x = jnp.dot(hn_ref[...], ut_ref[...], preferred_element_type=F32)
    act = x * (1.0 + lax.erf(x * (1.0 / math.sqrt(2.0))))
    parts = []
    for i in range(nblk // 2):
        w32 = wmap[pl.ds(j * (nblk // 2) + i, tm, stride=pitch), :]
        w_even = pltpu.bitcast(w32 << 16, F32)
        w_odd = pltpu.bitcast(w32 & HIGH_HALF, F32)
        parts.append((act[:, (2 * i) * nk:(2 * i + 1) * nk] * w_even).astype(BF16))
        parts.append((act[:, (2 * i + 1) * nk:(2 * i + 2) * nk] * w_odd).astype(BF16))
    pmat = jnp.concatenate(parts, axis=1)
    acc[...] += jnp.dot(pmat, v_ref[...], preferred_element_type=F32)

    @pl.when(j == pl.num_programs(1) - 1)
    def _():
        o_ref[...] = x_ref[...] + acc[...]


def _peer_experts(hn, a, b, g, ut, v, x2, cfg):
    n, d = x2.shape
    nexp = ut.shape[1]
    nk = cfg.peer_nkeys
    tm = min(cfg.peer_tm, n)
    te = min(cfg.peer_te, nexp)
    assert nk == V7X_LANES and nexp == nk * nk and te % (2 * nk) == 0 and nexp % te == 0
    pitch = nk // 2 + W_PITCH_PAD
    nj = a.shape[1]
    est = 2 * (tm * d * 2 + 3 * tm * nj * 4 + 2 * d * te * 2 + 2 * tm * d * 4) \
        + tm * pitch * nk * 4 + tm * d * 4 + 4 * tm * te * 4
    tile = lambda width: pl.BlockSpec((tm, width), lambda i, j: (i, 0))
    return pl.pallas_call(
        functools.partial(_expert_body, nk=nk, pitch=pitch, unroll=min(cfg.peer_unroll, tm)),
        out_shape=jax.ShapeDtypeStruct((n, d), F32),
        grid=(n // tm, nexp // te),
        in_specs=[tile(d), tile(nj), tile(nj), tile(nj),
                  pl.BlockSpec((d, te), lambda i, j: (0, j)),
                  pl.BlockSpec((te, d), lambda i, j: (j, 0)),
                  tile(d)],
        out_specs=tile(d),
        scratch_shapes=[pltpu.VMEM((tm * pitch, nk), jnp.uint32), pltpu.VMEM((tm, d), F32)],
        compiler_params=pltpu.CompilerParams(dimension_semantics=("parallel", "arbitrary"),
                                             vmem_limit_bytes=_vmem_limit(est)),
        name="peer_experts",
    )(hn, a, b, g, ut, v, x2)


class _RouteScratch(NamedTuple):
    s: object
    ts: object
    ti: object
    c_s: object
    c_a: object
    c_b: object
    best: object
    a_t: object
    b_t: object
    g_t: object


def _route_scores(qp_ref, keys_ref, rs):
    dq = keys_ref.shape[2]
    contract1 = (((1,), (1,)), ((), ()))
    for p in range(2):
        rs.s[p] = lax.dot_general(keys_ref[p], qp_ref[:, p * dq:(p + 1) * dq], contract1,
                                  preferred_element_type=F32)


def _route_first_stage(rs, it0, n_it):
    nkeys, th = rs.s.shape[1:]
    iota = lax.broadcasted_iota(I32, (nkeys, th), 0)
    for p in range(2):
        s = rs.s[p]
        for k in range(n_it):
            m = jnp.max(s, axis=0, keepdims=True)
            idx = jnp.min(jnp.where(s == m, iota, nkeys), axis=0, keepdims=True)
            rs.ts[p, pl.ds(it0 + k, 1), :] = m
            rs.ti[p, pl.ds(it0 + k, 1), :] = idx
            s = jnp.where(iota == idx, -jnp.inf, s)
        rs.s[p] = s


def _route_cells(rs, cells):
    crow, th = rs.c_s.shape
    ncell = len(cells)
    rs.c_s[pl.ds(ncell, crow - ncell), :] = jnp.full((crow - ncell, th), -jnp.inf, F32)
    rs.c_a[pl.ds(ncell, crow - ncell), :] = jnp.zeros((crow - ncell, th), I32)
    rs.c_b[pl.ds(ncell, crow - ncell), :] = jnp.zeros((crow - ncell, th), I32)
    for ci, (k1, k2) in enumerate(cells):
        rs.c_s[pl.ds(ci, 1), :] = rs.ts[0, pl.ds(k1, 1), :] + rs.ts[1, pl.ds(k2, 1), :]
        rs.c_a[pl.ds(ci, 1), :] = rs.ti[0, pl.ds(k1, 1), :]
        rs.c_b[pl.ds(ci, 1), :] = rs.ti[1, pl.ds(k2, 1), :]


def _route_second_stage(rs, it0, n_it, blk0, row0):
    crow, th = rs.c_s.shape
    iota = lax.broadcasted_iota(I32, (crow, th), 0)
    cs, ca, cb = rs.c_s[...], rs.c_a[...], rs.c_b[...]
    for k in range(n_it):
        m = jnp.max(cs, axis=0, keepdims=True)
        pos = jnp.min(jnp.where(cs == m, iota, crow), axis=0, keepdims=True)
        hit = iota == pos
        a_row = jnp.sum(jnp.where(hit, ca, 0), axis=0, keepdims=True)
        b_row = jnp.sum(jnp.where(hit, cb, 0), axis=0, keepdims=True)
        for c in range(th // V7X_LANES):
            cols = slice(c * V7X_LANES, (c + 1) * V7X_LANES)
            rs.a_t[blk0 + c, pl.ds(row0 + it0 + k, 1), :] = a_row[:, cols]
            rs.b_t[blk0 + c, pl.ds(row0 + it0 + k, 1), :] = b_row[:, cols]
        rs.best[pl.ds(it0 + k, 1), :] = m
        cs = jnp.where(hit, -jnp.inf, cs)
    rs.c_s[...] = cs


def _route_gates(rs, blk0, row0):
    topk, th = rs.best.shape
    best = rs.best[...]
    e = jnp.exp(best - best[0:1, :])
    gates = e / jnp.sum(e, axis=0, keepdims=True)
    for c in range(th // V7X_LANES):
        rs.g_t[blk0 + c, pl.ds(row0, topk), :] = gates[:, c * V7X_LANES:(c + 1) * V7X_LANES]


def _peer_fused_body(hn_ref, qp_ref, keys_ref, ut_ref, v_ref, x_ref, o_ref,
                     wmap, acc, a_cur, b_cur, g_cur, a_t, b_t, g_t, c_s, c_a, c_b,
                     *, nk, pitch, unroll, topk, cells, nhalf):
    r = pl.program_id(0)
    j = pl.program_id(1)
    tm = hn_ref.shape[0]
    th = qp_ref.shape[0]
    te = ut_ref.shape[1]
    nblk = te // nk
    half = nk // 2
    contract1 = (((1,), (1,)), ((), ()))

    @pl.when(j == 0)
    def _():
        acc[...] = jnp.zeros_like(acc)

        @pl.when(r == 0)
        def _():
            a_cur[...] = jnp.zeros_like(a_cur)
            b_cur[...] = jnp.zeros_like(b_cur)
            g_cur[...] = jnp.zeros_like(g_cur)

        row = lax.broadcasted_iota(I32, (nk, a_cur.shape[1]), 0)
        a_of_row = jnp.where(row < half, 2 * row, 2 * (row - half) + 1)

        def build(n, carry):
            ar = a_cur[pl.ds(n, 1), :]
            br = b_cur[pl.ds(n, 1), :]
            gr = g_cur[pl.ds(n, 1), :] * 0.5
            one_a = jnp.where(a_of_row == ar, 1.0, 0.0).astype(BF16)
            gate_b = jnp.where(row == br, gr, 0.0).astype(BF16)
            wn = lax.dot_general(one_a, gate_b, contract1, preferred_element_type=F32)
            lo = pltpu.bitcast(wn[:half], jnp.uint32)
            hi = pltpu.bitcast(wn[half:], jnp.uint32)
            packed = ((hi + HALF_ULP_BF16) & HIGH_HALF) | ((lo + HALF_ULP_BF16) >> 16)
            wmap[pl.ds(pl.multiple_of(n * pitch, V7X_SUBLANES), half), :] = packed
            return carry

        lax.fori_loop(0, tm, build, 0, unroll=unroll)

    x = jnp.dot(hn_ref[...], ut_ref[...], preferred_element_type=F32)
    act = x * (1.0 + lax.erf(x * (1.0 / math.sqrt(2.0))))
    parts = []
    for i in range(nblk // 2):
        w32 = wmap[pl.ds(j * (nblk // 2) + i, tm, stride=pitch), :]
        w_even = pltpu.bitcast(w32 << 16, F32)
        w_odd = pltpu.bitcast(w32 & HIGH_HALF, F32)
        parts.append((act[:, (2 * i) * nk:(2 * i + 1) * nk] * w_even).astype(BF16))
        parts.append((act[:, (2 * i + 1) * nk:(2 * i + 2) * nk] * w_odd).astype(BF16))
    pmat = jnp.concatenate(parts, axis=1)
    acc[...] += jnp.dot(pmat, v_ref[...], preferred_element_type=F32)

    @pl.when(j == pl.num_programs(1) - 1)
    def _():
        o_ref[...] = x_ref[...] + acc[...]
        for cb in range(tm // V7X_LANES):
            rows = slice(cb * V7X_LANES, (cb + 1) * V7X_LANES)
            a_cur[rows, :] = a_t[cb].T
            b_cur[rows, :] = b_t[cb].T
            g_cur[rows, :] = g_t[cb].T


def _peer_fused(hn, qp, sub_keys, ut, v, x2, cfg):
    n, d = x2.shape
    nexp = ut.shape[1]
    nk, heads, topk = cfg.peer_nkeys, cfg.peer_heads, cfg.peer_topk
    dq = sub_keys.shape[2]
    nj = heads * topk
    tm = min(cfg.peer_tm, n)
    te = min(cfg.peer_te, nexp)
    nsteps = nexp // te
    assert nk == V7X_LANES and nexp == nk * nk and te % (2 * nk) == 0 and nexp % te == 0
    assert nj == V7X_LANES and qp.shape[1] == heads * 2 * dq and n % tm == 0
    assert nsteps % heads == 0 and tm % (nsteps // heads) == 0
    nhalf = nsteps // heads
    th = tm // nhalf
    assert th % V7X_LANES == 0
    ntiles = n // tm
    pitch = nk // 2 + W_PITCH_PAD
    cells = _peer_cells(topk)
    crow = -(-(len(cells) + 1) // V7X_SUBLANES) * V7X_SUBLANES
    est = 2 * (tm * d * 2 + th * 2 * dq * 2 + 2 * d * te * 2 + 2 * tm * d * 4) \
        + tm * pitch * nk * 4 + tm * d * 4 + 6 * tm * nj * 4 + 3 * crow * th * 4 + 5 * tm * te * 4
    cur = lambda width: pl.BlockSpec((tm, width), lambda r, j: (jnp.maximum(r - 1, 0), 0))
    return pl.pallas_call(
        functools.partial(_peer_fused_body, nk=nk, pitch=pitch, unroll=min(cfg.peer_unroll, tm),
                          topk=topk, cells=tuple(cells), nhalf=nhalf),
        out_shape=jax.ShapeDtypeStruct((n, d), F32),
        grid=(ntiles + 1, nsteps),
        in_specs=[cur(d),
                  pl.BlockSpec((th, 2 * dq),
                               lambda r, j: (nhalf * jnp.minimum(r, ntiles - 1) + j % nhalf, j // nhalf)),
                  pl.BlockSpec(sub_keys.shape, lambda r, j: (0, 0, 0)),
                  pl.BlockSpec((d, te), lambda r, j: (0, j)),
                  pl.BlockSpec((te, d), lambda r, j: (j, 0)),
                  cur(d)],
        out_specs=cur(d),
        scratch_shapes=[pltpu.VMEM((tm * pitch, nk), jnp.uint32), pltpu.VMEM((tm, d), F32),
                        pltpu.VMEM((tm, nj), I32), pltpu.VMEM((tm, nj), I32), pltpu.VMEM((tm, nj), F32),
                        pltpu.VMEM((tm // V7X_LANES, nj, V7X_LANES), I32),
                        pltpu.VMEM((tm // V7X_LANES, nj, V7X_LANES), I32),
                        pltpu.VMEM((tm // V7X_LANES, nj, V7X_LANES), F32),
                        pltpu.VMEM((crow, th), F32), pltpu.VMEM((crow, th), I32), pltpu.VMEM((crow, th), I32)],
        compiler_params=pltpu.CompilerParams(dimension_semantics=("arbitrary", "arbitrary"),
                                             vmem_limit_bytes=_vmem_limit(est)),
        name="peer_fused",
    )(hn, qp, sub_keys.astype(BF16), ut, v, x2)


PEER_SUB = 2 * V7X_LANES


def _peer_pipe_body(hn_ref, qp_ref, keys_ref, ut_ref, v_ref, x_ref, o_ref,
                    wmap, acc, pmat, a_cur, b_cur, g_cur, s_scr, ts, ti, c_s, c_a, c_b, best, a_t, b_t, g_t,
                    *, nk, pitch, unroll, heads, topk, cells, nhalf, nsteps):
    r = pl.program_id(0)
    j = pl.program_id(1)
    tm = hn_ref.shape[0]
    th = qp_ref.shape[0]
    nsub, _, sub = ut_ref.shape
    half = nk // 2
    ncol = th // V7X_LANES
    it_per = topk // nsub
    nj = heads * topk
    rs = _RouteScratch(s_scr, ts, ti, c_s, c_a, c_b, best, a_t, b_t, g_t)
    contract1 = (((1,), (1,)), ((), ()))

    @pl.when(j == 0)
    def _():
        @pl.when(r == 0)
        def _():
            for ref in (acc, pmat, a_cur, b_cur, g_cur, ts, ti):
                ref[...] = jnp.zeros_like(ref)

        @pl.when(r > 0)
        def _():
            blk0 = ((nsteps - 1) % nhalf) * ncol
            row0 = ((nsteps - 1) // nhalf) * topk
            _route_cells(rs, cells)
            _route_second_stage(rs, 0, topk, blk0, row0)
            _route_gates(rs, blk0, row0)
            for cb in range(tm // V7X_LANES):
                rows = slice(cb * V7X_LANES, (cb + 1) * V7X_LANES)
                a_cur[rows, :] = a_t[cb, 0:nj, :].T
                b_cur[rows, :] = b_t[cb, 0:nj, :].T
                g_cur[rows, :] = g_t[cb, 0:nj, :].T

        row = lax.broadcasted_iota(I32, (nk, nj), 0)
        a_of_row = jnp.where(row < half, 2 * row, 2 * (row - half) + 1)

        def build(n, carry):
            ar = a_cur[pl.ds(n, 1), :]
            br = b_cur[pl.ds(n, 1), :]
            gr = g_cur[pl.ds(n, 1), :] * 0.5
            one_a = jnp.where(a_of_row == ar, 1.0, 0.0).astype(BF16)
            gate_b = jnp.where(row == br, gr, 0.0).astype(BF16)
            wn = lax.dot_general(one_a, gate_b, contract1, preferred_element_type=F32)
            lo = pltpu.bitcast(wn[:half], jnp.uint32)
            hi = pltpu.bitcast(wn[half:], jnp.uint32)
            packed = ((hi + HALF_ULP_BF16) & HIGH_HALF) | ((lo + HALF_ULP_BF16) >> 16)
            wmap[pl.ds(pl.multiple_of(n * pitch, V7X_SUBLANES), half), :] = packed
            return carry

        lax.fori_loop(0, tm, build, 0, unroll=unroll)

    ju = jnp.maximum(j - 1, 0)
    c_blk0 = lax.rem(ju, nhalf) * ncol
    c_row0 = pl.multiple_of(jnp.where(j == 0, nj, lax.div(ju, nhalf) * topk), topk)
    _route_cells(rs, cells)
    _route_scores(qp_ref, keys_ref, rs)
    cur = lax.rem(j, 2)
    prev = 1 - cur

    def trip(sb, carry):
        part = acc[sb]
        for k in range(nsub):
            part = part + jnp.dot(pmat[prev * nsub + k], v_ref[sb, k * sub:(k + 1) * sub, :],
                                  preferred_element_type=F32)
        acc[sb] = part
        xv = jnp.dot(hn_ref[...], ut_ref[sb], preferred_element_type=F32)
        act = xv * (1.0 + lax.erf(xv * (1.0 / math.sqrt(2.0))))
        w32 = wmap[pl.ds(j * nsub + sb, tm, stride=pitch), :]
        pm = jnp.concatenate([act[:, :nk] * pltpu.bitcast(w32 << 16, F32),
                              act[:, nk:] * pltpu.bitcast(w32 & HIGH_HALF, F32)], axis=1)
        pmat[cur * nsub + sb] = pm.astype(BF16)
        _route_first_stage(rs, sb * it_per, it_per)
        _route_second_stage(rs, sb * it_per, it_per, c_blk0, c_row0)
        return carry

    lax.fori_loop(0, nsub, trip, 0)
    _route_gates(rs, c_blk0, c_row0)

    @pl.when(j == 0)
    def _():
        for k in range(nsub):
            o_ref[:, k * sub:(k + 1) * sub] = x_ref[:, k * sub:(k + 1) * sub] + acc[k]
        acc[...] = jnp.zeros_like(acc)


def _peer_fused(hn, qp, sub_keys, u3, v4, x2, cfg):
    n, d = x2.shape
    nexp = u3.shape[0] * u3.shape[2]
    nk, heads, topk = cfg.peer_nkeys, cfg.peer_heads, cfg.peer_topk
    dq = sub_keys.shape[2]
    nj = heads * topk
    sub = PEER_SUB
    te = d
    nsub = te // sub
    nsteps = nexp // te
    assert nk == V7X_LANES and nexp == nk * nk and sub == 2 * nk and d % sub == 0 and nexp % te == 0
    assert nj == V7X_LANES and qp.shape[1] == heads * 2 * dq and topk % nsub == 0 and nsteps % 2 == 0
    assert nsteps % heads == 0
    nhalf = nsteps // heads
    tm = min(cfg.peer_tm, n)
    tm = max(tm, nhalf * V7X_LANES)
    assert n % tm == 0 and tm % nhalf == 0
    th = tm // nhalf
    assert th % V7X_LANES == 0
    ntiles = n // tm
    pitch = nk // 2 + W_PITCH_PAD
    cells = _peer_cells(topk)
    crow = -(-(len(cells) + 1) // V7X_SUBLANES) * V7X_SUBLANES
    nblk = tm // V7X_LANES
    est = 2 * (tm * d * 2 + th * 2 * dq * 2 + 2 * d * te * 2 + 2 * tm * d * 4) \
        + tm * pitch * nk * 4 + tm * d * 4 + 2 * tm * te * 2 + 3 * tm * nj * 4 \
        + 3 * nblk * (nj + topk) * V7X_LANES * 4 + (3 * crow + 2 * nk + 5 * topk) * th * 4 + 6 * tm * sub * 4
    tile_in = pl.BlockSpec((tm, d), lambda r, j: (jnp.clip(r - 1, 0, ntiles - 1), 0))
    tile_out = pl.BlockSpec((tm, d), lambda r, j: (jnp.clip(jnp.where(j == 0, r - 2, r - 1), 0, ntiles - 1), 0))
    return pl.pallas_call(
        functools.partial(_peer_pipe_body, nk=nk, pitch=pitch, unroll=min(cfg.peer_unroll, tm), heads=heads,
                          topk=topk, cells=tuple(cells), nhalf=nhalf, nsteps=nsteps),
        out_shape=jax.ShapeDtypeStruct((n, d), F32),
        grid=(ntiles + 2, nsteps),
        in_specs=[tile_in,
                  pl.BlockSpec((th, 2 * dq),
                               lambda r, j: (nhalf * jnp.minimum(r, ntiles - 1) + j % nhalf, j // nhalf)),
                  pl.BlockSpec(sub_keys.shape, lambda r, j: (0, 0, 0)),
                  pl.BlockSpec((nsub, d, sub), lambda r, j: (j, 0, 0)),
                  pl.BlockSpec((d // sub, te, sub), lambda r, j: (0, (j + nsteps - 1) % nsteps, 0)),
                  tile_out],
        out_specs=tile_out,
        scratch_shapes=[pltpu.VMEM((tm * pitch, nk), jnp.uint32),
                        pltpu.VMEM((nsub, tm, sub), F32), pltpu.VMEM((2 * nsub, tm, sub), BF16),
                        pltpu.VMEM((tm, nj), I32), pltpu.VMEM((tm, nj), I32), pltpu.VMEM((tm, nj), F32),
                        pltpu.VMEM((2, nk, th), F32), pltpu.VMEM((2, topk, th), F32), pltpu.VMEM((2, topk, th), I32),
                        pltpu.VMEM((crow, th), F32), pltpu.VMEM((crow, th), I32), pltpu.VMEM((crow, th), I32),
                        pltpu.VMEM((topk, th), F32),
                        pltpu.VMEM((nblk, nj + topk, V7X_LANES), I32), pltpu.VMEM((nblk, nj + topk, V7X_LANES), I32),
                        pltpu.VMEM((nblk, nj + topk, V7X_LANES), F32)],
        compiler_params=pltpu.CompilerParams(dimension_semantics=("arbitrary", "arbitrary"),
                                             vmem_limit_bytes=_vmem_limit(est)),
        name="peer_fused",
    )(hn, qp, sub_keys.astype(BF16), u3, v4, x2)


def _peer(x2, gain, w_q_bf16, sub_keys, u3, v4, cfg):
    nq = w_q_bf16.shape[1]
    qp, hn = _norm_matmul(x2, gain, w_q_bf16, [(0, nq)], [BF16], cfg, emit_hn=True)
    return _peer_fused(hn, qp, sub_keys, u3, v4, x2, cfg)


def _trunk(x, p, cfg):
    b, t, d = x.shape
    x2 = x.reshape(b * t, d)
    depth = p["norm_mix"].shape[0]
    for l in range(depth):
        i = l // 2
        if l % 2 == 0:
            rw = 4 * cfg.ret_heads * cfg.ret_dk
            w_in = p["ab_w_in"][i]
            zq, u = _norm_matmul(x2, p["norm_mix"][l], w_in, [(0, rw), (rw, w_in.shape[1])], [F32, F32], cfg)
            ret = _retention(zq, p["ab_ret_decay"][i], p["ab_ret_gn"][i], b, t, cfg)
            ssm = _s5(u, tuple(p[k][i] for k in ("ab_s5_lam_re", "ab_s5_lam_im", "ab_s5_log_dt", "ab_s5_b_re",
                                                  "ab_s5_b_im", "ab_s5_c_re", "ab_s5_c_im", "ab_s5_d",
                                                  "ab_s5_glu_w", "ab_s5_glu_b")), b, t, cfg)
            w_out = p["ab_w_out"][i]
            nr = ret.shape[1]
            x2 = _matmul_residual([ret, ssm], [w_out[:nr], w_out[nr:]], x2, cfg)
        else:
            w_qkv = p["na_w_qkv"][i]
            (z,) = _norm_matmul(x2, p["norm_mix"][l], w_qkv, [(0, w_qkv.shape[1])], [F32], cfg)
            att = _neighbourhood_attention(z, p["na_q_gain"][i], p["na_k_gain"][i], p["na_rpb"][i], b, t, cfg)
            x2 = _matmul_residual([att], [p["na_w_o"][i]], x2, cfg)
        x2 = _peer(x2, p["norm_ffn"][l], p["peer_w_q"][l], p["peer_sub_keys"][l], p["peer_u3"][l],
                   p["peer_v4"][l], cfg)
    return x2.reshape(b, t, d)


def _prepare(params):
    p = dict(params)
    for k in ("ab_w_in", "ab_w_out", "na_w_qkv", "na_w_o", "peer_w_q"):
        p[k] = params[k].astype(BF16)
    u, v = params["peer_u"].astype(BF16), params["peer_v"].astype(BF16)
    layers, nexp, d = u.shape
    p["peer_u3"] = u.reshape(layers, nexp // PEER_SUB, PEER_SUB, d).transpose(0, 1, 3, 2)
    p["peer_v4"] = v.reshape(layers, nexp, d // PEER_SUB, PEER_SUB).transpose(0, 2, 1, 3)
    return p


def _forward(x_prompt, x_sample, params, cfg=Cfg()):
    p = _prepare(params)
    return _trunk(x_prompt, p, cfg), _trunk(x_sample, p, cfg)


def kernel(x_prompt, x_sample, norm_mix, norm_ffn, ab_w_in, ab_ret_decay, ab_ret_gn, ab_s5_lam_re, ab_s5_lam_im, ab_s5_log_dt, ab_s5_b_re, ab_s5_b_im, ab_s5_c_re, ab_s5_c_im, ab_s5_d, ab_s5_glu_w, ab_s5_glu_b, ab_w_out, na_w_qkv, na_q_gain, na_k_gain, na_rpb, na_w_o, peer_w_q, peer_sub_keys, peer_u, peer_v):
    params = dict(norm_mix=norm_mix, norm_ffn=norm_ffn, ab_w_in=ab_w_in, ab_ret_decay=ab_ret_decay,
                  ab_ret_gn=ab_ret_gn, ab_s5_lam_re=ab_s5_lam_re, ab_s5_lam_im=ab_s5_lam_im,
                  ab_s5_log_dt=ab_s5_log_dt, ab_s5_b_re=ab_s5_b_re, ab_s5_b_im=ab_s5_b_im,
                  ab_s5_c_re=ab_s5_c_re, ab_s5_c_im=ab_s5_c_im, ab_s5_d=ab_s5_d, ab_s5_glu_w=ab_s5_glu_w,
                  ab_s5_glu_b=ab_s5_glu_b, ab_w_out=ab_w_out, na_w_qkv=na_w_qkv, na_q_gain=na_q_gain,
                  na_k_gain=na_k_gain, na_rpb=na_rpb, na_w_o=na_w_o, peer_w_q=peer_w_q,
                  peer_sub_keys=peer_sub_keys, peer_u=peer_u, peer_v=peer_v)
    return _forward(x_prompt, x_sample, params)
```

```python
import functools
import math
from typing import NamedTuple

import numpy as np
import jax
import jax.numpy as jnp
from jax import lax
from jax.experimental import pallas as pl
from jax.experimental.pallas import tpu as pltpu

F32 = jnp.float32
BF16 = jnp.bfloat16
I32 = jnp.int32

V7X_LANES = 128
V7X_SUBLANES = 8
V7X_VMEM_BYTES = 64 * 2**20
VMEM_LIMIT_CAP = V7X_VMEM_BYTES - 8 * 2**20

NEG_BIG = -1e30


class Cfg(NamedTuple):
    eps: float = 1e-6
    grid_w: int = 64
    ret_heads: int = 4
    ret_dk: int = 128
    ret_chunk: int = 128
    rope_base: float = 10000.0
    s5_group: int = 16
    s5_state: int = 64
    s5_chunk: int = 16
    na_heads: int = 16
    na_kh: int = 8
    na_kw: int = 16
    peer_heads: int = 8
    peer_nkeys: int = 128
    peer_topk: int = 16
    tm: int = 512
    peer_tm: int = 512
    peer_unroll: int = 32
    na_unroll: int = 4
    ret_group: int = 4
    peer_te: int = 1024
    topk_tm: int = 256


def _vmem_limit(nbytes):
    return int(min(VMEM_LIMIT_CAP, max(32 * 2**20, nbytes)))


def _gelu(x):
    return 0.5 * x * (1.0 + lax.erf(x * (1.0 / math.sqrt(2.0))))


def _sigmoid(x):
    return 1.0 / (1.0 + jnp.exp(-x))


def _norm_mm_body(x_ref, g_ref, w_ref, *out_refs, splits, emit_hn, eps):
    x = x_ref[...]
    y = x * lax.rsqrt(jnp.mean(x * x, axis=-1, keepdims=True) + eps) * g_ref[...]
    yb = y.astype(BF16)
    z = jnp.dot(yb, w_ref[...], preferred_element_type=F32)
    for r, (s, e) in zip(out_refs, splits):
        r[...] = z[:, s:e].astype(r.dtype)
    if emit_hn:
        out_refs[len(splits)][...] = yb


def _norm_matmul(x2, gain, w_bf16, splits, dtypes, cfg, emit_hn=False):
    n, d = x2.shape
    nout = w_bf16.shape[1]
    tm = min(cfg.tm, n)
    out_shape = [jax.ShapeDtypeStruct((n, e - s), dt) for (s, e), dt in zip(splits, dtypes)]
    out_specs = [pl.BlockSpec((tm, e - s), lambda i: (i, 0)) for (s, e) in splits]
    if emit_hn:
        out_shape.append(jax.ShapeDtypeStruct((n, d), BF16))
        out_specs.append(pl.BlockSpec((tm, d), lambda i: (i, 0)))
    est = 2 * (tm * d * 4 + d * nout * 2 + tm * nout * 4 + tm * d * 2) + 2 * tm * nout * 4
    return pl.pallas_call(
        functools.partial(_norm_mm_body, splits=tuple(splits), emit_hn=emit_hn, eps=cfg.eps),
        out_shape=out_shape,
        grid=(n // tm,),
        in_specs=[pl.BlockSpec((tm, d), lambda i: (i, 0)),
                  pl.BlockSpec((1, d), lambda i: (0, 0)),
                  pl.BlockSpec((d, nout), lambda i: (0, 0))],
        out_specs=out_specs,
        compiler_params=pltpu.CompilerParams(dimension_semantics=("parallel",),
                                             vmem_limit_bytes=_vmem_limit(est)),
        name="norm_matmul",
    )(x2, gain.reshape(1, d).astype(F32), w_bf16)


def _mm_res_body(*refs, n_in):
    acc = refs[2 * n_in][...]
    for a, w in zip(refs[:n_in], refs[n_in:2 * n_in]):
        acc = acc + jnp.dot(a[...], w[...], preferred_element_type=F32)
    refs[-1][...] = acc


def _matmul_residual(a_list, w_list, res, cfg):
    n, d = res.shape
    tm = min(cfg.tm, n)
    n_in = len(a_list)
    in_specs = ([pl.BlockSpec((tm, a.shape[1]), lambda i: (i, 0)) for a in a_list]
                + [pl.BlockSpec(w.shape, lambda i: (0, 0)) for w in w_list]
                + [pl.BlockSpec((tm, d), lambda i: (i, 0))])
    est = 2 * (sum(tm * a.shape[1] * 2 for a in a_list) + sum(w.size * 2 for w in w_list)
               + 2 * tm * d * 4) + 2 * tm * d * 4
    return pl.pallas_call(
        functools.partial(_mm_res_body, n_in=n_in),
        out_shape=jax.ShapeDtypeStruct((n, d), F32),
        grid=(n // tm,),
        in_specs=in_specs,
        out_specs=pl.BlockSpec((tm, d), lambda i: (i, 0)),
        compiler_params=pltpu.CompilerParams(dimension_semantics=("parallel",),
                                             vmem_limit_bytes=_vmem_limit(est)),
        name="matmul_residual",
    )(*a_list, *w_list, res)


def _ret_tables(ret_decay, chunk, width):
    lg = -jax.nn.softplus(-ret_decay.astype(F32))
    pos = jnp.arange(chunk, dtype=F32)
    diff = pos[:, None] - pos[None, :]
    d_f = jnp.where(diff >= 0, jnp.exp(lg[0][:, None, None] * jnp.maximum(diff, 0.0)[None]), 0.0)
    d_b = jnp.where(diff < 0, jnp.exp(lg[1][:, None, None] * jnp.maximum(-diff, 0.0)[None]), 0.0)
    cols = [jnp.exp(lg[0][:, None] * (chunk - 1.0 - pos)[None]),
            jnp.exp(lg[1][:, None] * pos[None]),
            jnp.exp(lg[0][:, None] * (pos + 1.0)[None]),
            jnp.exp(lg[1][:, None] * (chunk - pos)[None]),
            jnp.broadcast_to(jnp.exp(lg[0] * chunk)[:, None], (lg.shape[1], chunk)),
            jnp.broadcast_to(jnp.exp(lg[1] * chunk)[:, None], (lg.shape[1], chunk))]
    tab = jnp.stack(cols, axis=1)
    return d_f + d_b, jnp.broadcast_to(tab[..., None], tab.shape + (width,))


def _rope_tables(t, half, base):
    inv = base ** (-jnp.arange(half, dtype=F32) / half)
    ang = jnp.arange(t, dtype=F32)[:, None] * inv[None, :]
    cos, sin = jnp.cos(ang), jnp.sin(ang)
    return jnp.concatenate([cos, cos], axis=1), jnp.concatenate([-sin, sin], axis=1)


def _ret_body(q_ref, k_ref, v_ref, g_ref, cos_ref, sin_ref, d_ref, tab_ref, gn_ref, o_ref,
              qs, ks, sf, sb, *, chunk, nc, kscale, eps, group):
    dk = q_ref.shape[1]
    half = dk // 2
    cos, sin = cos_ref[...], sin_ref[...]
    q = q_ref[...]
    qs[...] = q * cos + pltpu.roll(q, half, 1) * sin
    k = k_ref[...]
    ks[...] = (k * cos + pltpu.roll(k, half, 1) * sin) * kscale
    contract0 = (((0,), (0,)), ((), ()))
    contract1 = (((1,), (1,)), ((), ()))

    k_f, k_b, q_f, q_b = tab_ref[0, 0], tab_ref[0, 1], tab_ref[0, 2], tab_ref[0, 3]
    cd_f, cd_b = tab_ref[0, 4], tab_ref[0, 5]

    def increments(n, carry):
        r0 = pl.multiple_of(n * chunk, chunk)
        kc = ks[pl.ds(r0, chunk), :]
        vc = v_ref[pl.ds(r0, chunk), :].astype(BF16)
        sf[n] = lax.dot_general((kc * k_f).astype(BF16), vc, contract0, preferred_element_type=F32)
        sb[n] = lax.dot_general((kc * k_b).astype(BF16), vc, contract0, preferred_element_type=F32)
        return carry

    lax.fori_loop(0, nc, increments, 0, unroll=group)

    def sweep_f(n, state):
        inc = sf[n]
        sf[n] = state
        return state * cd_f + inc

    def sweep_b(i, state):
        n = nc - 1 - i
        inc = sb[n]
        sb[n] = state
        return state * cd_b + inc

    zero = jnp.zeros((dk, v_ref.shape[1]), F32)
    lax.fori_loop(0, nc, sweep_f, zero)
    lax.fori_loop(0, nc, sweep_b, zero)

    def outputs(gi, carry):
        first = []
        for u in range(group):
            n = gi * group + u
            r0 = pl.multiple_of(n * chunk, chunk)
            qc = qs[pl.ds(r0, chunk), :]
            s = lax.dot_general(qc.astype(BF16), ks[pl.ds(r0, chunk), :].astype(BF16), contract1,
                                preferred_element_type=F32)
            cross = jnp.dot((qc * q_f).astype(BF16), sf[n].astype(BF16), preferred_element_type=F32)
            cross = cross + jnp.dot((qc * q_b).astype(BF16), sb[n].astype(BF16), preferred_element_type=F32)
            first.append((r0, s, cross))
        for r0, s, cross in first:
            vc = v_ref[pl.ds(r0, chunk), :].astype(BF16)
            o = jnp.dot((s * d_ref[0]).astype(BF16), vc, preferred_element_type=F32) + cross
            oc = o - jnp.mean(o, axis=-1, keepdims=True)
            o = oc * lax.rsqrt(jnp.mean(oc * oc, axis=-1, keepdims=True) + eps)
            g = g_ref[pl.ds(r0, chunk), :]
            o_ref[pl.ds(r0, chunk), :] = (o * gn_ref[...] * (g * _sigmoid(g))).astype(o_ref.dtype)
        return carry

    lax.fori_loop(0, nc // group, outputs, 0)


def _retention(zq, ret_decay, ret_gn, b, t, cfg):
    h, dk, c = cfg.ret_heads, cfg.ret_dk, cfg.ret_chunk
    assert t % c == 0 and zq.shape[1] == 4 * h * dk
    nc = t // c
    dmat, tab = _ret_tables(ret_decay, c, dk)
    cos2, sin2 = _rope_tables(t, dk // 2, cfg.rope_base)
    blk = lambda off: pl.BlockSpec((t, dk), lambda bi, hi, off=off: (bi, off + hi))
    est = 2 * (4 * t * dk * 4 + 2 * t * dk * 4 + c * c * 4 + 6 * c * dk * 4 + t * dk * 2) \
        + 2 * t * dk * 4 + 2 * nc * dk * dk * 4 + 8 * t * dk * 4
    return pl.pallas_call(
        functools.partial(_ret_body, chunk=c, nc=nc, kscale=dk ** -0.5, eps=cfg.eps,
                          group=math.gcd(cfg.ret_group, nc)),
        out_shape=jax.ShapeDtypeStruct((b * t, h * dk), BF16),
        grid=(b, h),
        in_specs=[blk(0), blk(h), blk(2 * h), blk(3 * h),
                  pl.BlockSpec((t, dk), lambda bi, hi: (0, 0)),
                  pl.BlockSpec((t, dk), lambda bi, hi: (0, 0)),
                  pl.BlockSpec((1, c, c), lambda bi, hi: (hi, 0, 0)),
                  pl.BlockSpec((1, 6, c, dk), lambda bi, hi: (hi, 0, 0, 0)),
                  pl.BlockSpec((1, dk), lambda bi, hi: (0, hi))],
        out_specs=pl.BlockSpec((t, dk), lambda bi, hi: (bi, hi)),
        scratch_shapes=[pltpu.VMEM((t, dk), F32), pltpu.VMEM((t, dk), F32),
                        pltpu.VMEM((nc, dk, dk), F32), pltpu.VMEM((nc, dk, dk), F32)],
        compiler_params=pltpu.CompilerParams(dimension_semantics=("parallel", "parallel"),
                                             vmem_limit_bytes=_vmem_limit(est)),
        name="retention",
    )(zq, zq, zq, zq, cos2, sin2, dmat, tab, ret_gn.reshape(1, h * dk).astype(F32))


def _s5_tables(lam_re, lam_im, log_dt, b_re, b_im, c_re, c_im, d_skip, glu_w, glu_b, L, nsteps):
    lam = lax.complex(lam_re.astype(F32), lam_im.astype(F32))
    ldt = lam * jnp.exp(log_dt.astype(F32))[..., None]
    lam_bar = jnp.exp(ldt)
    bmat = lax.complex(b_re.astype(F32), b_im.astype(F32))
    b_bar = ((lam_bar - 1.0) / lam)[..., None] * bmat[None]
    cmat = lax.complex(c_re.astype(F32), c_im.astype(F32))
    g, p, cg = bmat.shape
    tau = jnp.arange(L + 1, dtype=F32)
    pw = jnp.exp(ldt[:, :, None, :] * tau[None, None, :, None])
    kern = jnp.real(jnp.einsum('dgop,dgtp,dgpi->dgtoi', cmat, pw[:, :, :L], b_bar))
    li = np.arange(L)
    lag = li[None, :] - li[:, None]
    k_f = jnp.where((lag >= 0)[None, :, :, None, None], kern[0][:, np.clip(lag, 0, L - 1)], 0.0)
    k_b = jnp.where((lag <= 0)[None, :, :, None, None], kern[1][:, np.clip(-lag, 0, L - 1)], 0.0)
    m = (k_f + k_b).transpose(0, 1, 4, 2, 3).reshape(g, L * cg, L * cg)

    def cat(z):
        return jnp.concatenate([jnp.real(z), jnp.imag(z)], axis=-1)

    inc_f = cat(jnp.einsum('glp,gpi->glip', pw[0][:, L - 1 - li], b_bar[0])).reshape(g, L * cg, 2 * p)
    inc_b = cat(jnp.einsum('glp,gpi->glip', pw[1][:, li], b_bar[1])).reshape(g, L * cg, 2 * p)

    def out_mat(z):
        return jnp.concatenate([jnp.real(z), -jnp.imag(z)], axis=1).reshape(g, 2 * p, L * cg)

    out_f = out_mat(jnp.einsum('gop,glp->gplo', cmat[0], pw[0][:, li + 1]))
    out_b = out_mat(jnp.einsum('gop,glp->gplo', cmat[1], pw[1][:, L - li]))
    eye = jnp.eye(L, dtype=F32)
    glu = jnp.einsum('lm,gce->glcme', eye, glu_w.astype(F32)).reshape(g, L * cg, L * cg)
    vecs = jnp.stack([jnp.tile(d_skip.astype(F32), (1, L)), jnp.tile(glu_b.astype(F32), (1, L))], axis=1)
    steps = (2.0 ** jnp.arange(nsteps, dtype=F32)) * L
    a = jnp.exp(ldt[:, :, None, :] * steps[None, None, :, None])
    scan = jnp.stack([jnp.concatenate([jnp.real(a), jnp.real(a)], -1),
                      jnp.concatenate([-jnp.imag(a), jnp.imag(a)], -1)], axis=3)
    scan = scan.transpose(1, 0, 2, 3, 4).reshape(g, 2 * nsteps * 2, 2 * p)
    return (m.astype(BF16), inc_f.astype(BF16), inc_b.astype(BF16), out_f.astype(BF16),
            out_b.astype(BF16), glu.astype(BF16), vecs, scan)


def _s5_body(u_ref, m_ref, incf_ref, incb_ref, outf_ref, outb_ref, glu_ref, vec_ref, scan_ref, o_ref,
             *, ncs, nsteps):
    u = u_ref[0]
    rows = u.shape[0]
    ub = u.astype(BF16)
    y = jnp.dot(ub, m_ref[0], preferred_element_type=F32)
    xf = jnp.dot(ub, incf_ref[0], preferred_element_type=F32)
    xb = jnp.dot(ub, incb_ref[0], preferred_element_type=F32)
    p2 = xf.shape[1]
    cidx = lax.rem(lax.broadcasted_iota(I32, (rows, 1), 0), ncs)
    for kk in range(nsteps):
        s = 1 << kk
        a_f, b_f = scan_ref[0, pl.ds(2 * kk, 1), :], scan_ref[0, pl.ds(2 * kk + 1, 1), :]
        a_b = scan_ref[0, pl.ds(2 * nsteps + 2 * kk, 1), :]
        b_b = scan_ref[0, pl.ds(2 * nsteps + 2 * kk + 1, 1), :]
        pf = pltpu.roll(xf, s, 0)
        xf = xf + jnp.where(cidx >= s, a_f * pf + b_f * pltpu.roll(pf, p2 // 2, 1), 0.0)
        pb = pltpu.roll(xb, rows - s, 0)
        xb = xb + jnp.where(cidx < ncs - s, a_b * pb + b_b * pltpu.roll(pb, p2 // 2, 1), 0.0)
    x_prev = jnp.where(cidx >= 1, pltpu.roll(xf, 1, 0), 0.0)
    x_next = jnp.where(cidx < ncs - 1, pltpu.roll(xb, rows - 1, 0), 0.0)
    y = y + jnp.dot(x_prev.astype(BF16), outf_ref[0], preferred_element_type=F32)
    y = y + jnp.dot(x_next.astype(BF16), outb_ref[0], preferred_element_type=F32)
    y = y + u * vec_ref[0, pl.ds(0, 1), :]
    yg = _gelu(y)
    z = jnp.dot(yg.astype(BF16), glu_ref[0], preferred_element_type=F32) + vec_ref[0, pl.ds(1, 1), :]
    o_ref[0] = (yg * _sigmoid(z)).astype(o_ref.dtype)


def _s5(u, params, b, t, cfg):
    lam_re, lam_im, log_dt, b_re, b_im, c_re, c_im, d_skip, glu_w, glu_b = params
    L, cg = cfg.s5_chunk, cfg.s5_group
    n, ch = u.shape
    g = ch // cg
    assert t % L == 0
    ncs = t // L
    nsteps = max(1, (ncs - 1).bit_length())
    tabs = _s5_tables(lam_re, lam_im, log_dt, b_re, b_im, c_re, c_im, d_skip, glu_w, glu_b, L, nsteps)
    uc = u.reshape(n // L, L, g, cg).transpose(2, 0, 1, 3).reshape(g, n // L, L * cg)
    seqs = max(1, min(b, 1024 // ncs))
    while b % seqs:
        seqs -= 1
    rows = seqs * ncs
    w = L * cg
    p2 = 2 * cfg.s5_state
    per_g = lambda shape: pl.BlockSpec((1,) + shape, lambda gi, ri: (gi, 0, 0))
    est = 2 * (rows * w * 4 + rows * w * 2) + 2 * 2 * (3 * w * w + 4 * w * p2) + 12 * rows * w * 4
    out = pl.pallas_call(
        functools.partial(_s5_body, ncs=ncs, nsteps=nsteps),
        out_shape=jax.ShapeDtypeStruct((g, n // L, w), BF16),
        grid=(g, (n // L) // rows),
        in_specs=[pl.BlockSpec((1, rows, w), lambda gi, ri: (gi, ri, 0)),
                  per_g((w, w)), per_g((w, p2)), per_g((w, p2)), per_g((p2, w)), per_g((p2, w)),
                  per_g((w, w)), per_g((2, w)), per_g((4 * nsteps, p2))],
        out_specs=pl.BlockSpec((1, rows, w), lambda gi, ri: (gi, ri, 0)),
        compiler_params=pltpu.CompilerParams(dimension_semantics=("parallel", "parallel"),
                                             vmem_limit_bytes=_vmem_limit(est)),
        name="s5",
    )(uc, *tabs)
    return out.reshape(g, n // L, L, cg).transpose(1, 2, 0, 3).reshape(n, ch)


def _na_bias_tables(rpb, cfg, rows):
    kh, kw, w = cfg.na_kh, cfg.na_kw, cfg.grid_w
    j = np.arange(w)
    c_start = np.clip(j - kw // 2, 0, w - kw)
    c = np.arange(w)
    inside = (c[None, :] >= c_start[:, None]) & (c[None, :] < c_start[:, None] + kw)
    by_row = jnp.stack([rpb.astype(F32)[:, kh - 1 - dl:2 * kh - 1 - dl] for dl in range(kh)], axis=1)
    padded = jnp.pad(by_row, ((0, 0), (0, 0), (0, 0), (w - kw, w - kw)))
    bias = jnp.stack([padded[..., w - 1 - jq:2 * w - 1 - jq] for jq in range(w)], axis=2)
    bias = jnp.where(inside[None, None, :, None, :], bias, NEG_BIG)
    return bias.reshape(rpb.shape[0], kh, w, kh * w)


def _na_body(q_ref, k_ref, v_ref, qg_ref, kg_ref, tab_ref, o_ref, qn, kn, vb, *, w, rows, kh, dh, eps,
             unroll):
    lo = lax.broadcasted_iota(I32, (1, 2 * dh), 1) < dh
    same_head = ((lax.broadcasted_iota(I32, (2 * dh, 2 * dh), 0) < dh)
                 == (lax.broadcasted_iota(I32, (2 * dh, 2 * dh), 1) < dh))
    head_mean = jnp.where(same_head, 1.0 / dh, 0.0).astype(BF16)

    def head_norm(x, gain):
        x2 = x * x
        hi = x2.astype(BF16)
        rest = (x2 - hi.astype(F32)).astype(BF16)
        ms = (jnp.dot(hi, head_mean, preferred_element_type=F32)
              + jnp.dot(rest, head_mean, preferred_element_type=F32))
        return x * lax.rsqrt(ms + eps) * gain

    qh = head_norm(q_ref[...], qg_ref[...]) * dh ** -0.5
    qn[0] = jnp.where(lo, qh, 0.0).astype(BF16)
    qn[1] = jnp.where(lo, 0.0, qh).astype(BF16)
    kn[...] = head_norm(k_ref[...], kg_ref[...]).astype(BF16)
    vb[...] = v_ref[...].astype(BF16)
    contract1 = (((1,), (1,)), ((), ()))

    def group(gi, carry):
        chains = []
        for u in range(unroll):
            r = gi * unroll + u
            rs = jnp.clip(r - kh // 2, 0, rows - kh)
            q0 = pl.multiple_of(r * w, w)
            k0 = pl.multiple_of(rs * w, w)
            for hh in range(2):
                s = lax.dot_general(qn[hh, pl.ds(q0, w), :], kn[pl.ds(k0, kh * w), :], contract1,
                                    preferred_element_type=F32)
                chains.append((s + tab_ref[hh, r - rs], k0, q0))
        outs = []
        for s, k0, q0 in chains:
            p = jnp.exp(s - jnp.max(s, axis=-1, keepdims=True))
            p = p / jnp.sum(p, axis=-1, keepdims=True)
            outs.append(jnp.dot(p.astype(BF16), vb[pl.ds(k0, kh * w), :], preferred_element_type=F32))
        for u in range(unroll):
            q0 = chains[2 * u][2]
            o_ref[pl.ds(q0, w), :] = jnp.where(lo, outs[2 * u], outs[2 * u + 1]).astype(o_ref.dtype)
        return carry

    lax.fori_loop(0, rows // unroll, group, 0)


def _neighbourhood_attention(z, q_gain, k_gain, rpb, b, t, cfg):
    h, w, kh = cfg.na_heads, cfg.grid_w, cfg.na_kh
    dh = z.shape[1] // (3 * h)
    rows = t // w
    assert t % w == 0 and rows >= kh and h % 2 == 0 and 2 * dh == V7X_LANES
    tab = _na_bias_tables(rpb, cfg, rows)
    hp = h // 2
    blk = lambda off: pl.BlockSpec((t, 2 * dh), lambda bi, pi, off=off: (bi, off + pi))
    gain = pl.BlockSpec((1, 2 * dh), lambda bi, pi: (0, pi))
    est = 2 * (3 * t * 2 * dh * 4 + 2 * kh * w * kh * w * 4 + t * 2 * dh * 2) + 4 * t * 2 * dh * 2 \
        + 6 * t * 2 * dh * 4
    return pl.pallas_call(
        functools.partial(_na_body, w=w, rows=rows, kh=kh, dh=dh, eps=cfg.eps,
                          unroll=math.gcd(cfg.na_unroll, rows)),
        out_shape=jax.ShapeDtypeStruct((b * t, h * dh), BF16),
        grid=(b, hp),
        in_specs=[blk(0), blk(hp), blk(2 * hp), gain, gain,
                  pl.BlockSpec((2, kh, w, kh * w), lambda bi, pi: (pi, 0, 0, 0))],
        out_specs=pl.BlockSpec((t, 2 * dh), lambda bi, pi: (bi, pi)),
        scratch_shapes=[pltpu.VMEM((2, t, 2 * dh), BF16), pltpu.VMEM((t, 2 * dh), BF16),
                        pltpu.VMEM((t, 2 * dh), BF16)],
        compiler_params=pltpu.CompilerParams(dimension_semantics=("parallel", "parallel"),
                                             vmem_limit_bytes=_vmem_limit(est)),
        name="neighbourhood_attention",
    )(z, z, z, q_gain.reshape(1, h * dh).astype(F32), k_gain.reshape(1, h * dh).astype(F32), tab)


def _peer_cells(topk):
    return [(k1, k2) for k1 in range(topk) for k2 in range(topk) if (k1 + 1) * (k2 + 1) <= topk]


def _top_rows(s, k, iota):
    nrows = s.shape[0]
    vals, idxs = [], []
    for _ in range(k):
        m = jnp.max(s, axis=0, keepdims=True)
        idx = jnp.min(jnp.where(s == m, iota, nrows), axis=0, keepdims=True)
        vals.append(m)
        idxs.append(idx)
        s = jnp.where(iota == idx, -jnp.inf, s)
    return vals, idxs


def _route_body(qp_ref, keys_ref, a_ref, b_ref, g_ref, a_t, b_t, g_t, c_s, c_a, c_b,
                *, heads, nkeys, topk, cells):
    tm = qp_ref.shape[0]
    dq = keys_ref.shape[2]
    iota_k = lax.broadcasted_iota(I32, (nkeys, tm), 0)
    ncell = len(cells)
    crow = c_s.shape[0]
    iota_c = lax.broadcasted_iota(I32, (crow, tm), 0)
    contract1 = (((1,), (1,)), ((), ()))
    c_s[pl.ds(ncell, crow - ncell), :] = jnp.full((crow - ncell, tm), -jnp.inf, F32)
    c_a[pl.ds(ncell, crow - ncell), :] = jnp.zeros((crow - ncell, tm), I32)
    c_b[pl.ds(ncell, crow - ncell), :] = jnp.zeros((crow - ncell, tm), I32)
    for h in range(heads):
        tops = []
        for p in range(2):
            c0 = (2 * h + p) * dq
            q = qp_ref[:, c0:c0 + dq].astype(BF16)
            s = lax.dot_general(keys_ref[p], q, contract1, preferred_element_type=F32)
            tops.append(_top_rows(s, topk, iota_k))
        (v1, i1), (v2, i2) = tops
        for ci, (k1, k2) in enumerate(cells):
            c_s[pl.ds(ci, 1), :] = v1[k1] + v2[k2]
            c_a[pl.ds(ci, 1), :] = i1[k1]
            c_b[pl.ds(ci, 1), :] = i2[k2]
        cs, ca, cb = c_s[...], c_a[...], c_b[...]
        best = []
        for kk in range(topk):
            m = jnp.max(cs, axis=0, keepdims=True)
            pos = jnp.min(jnp.where(cs == m, iota_c, crow), axis=0, keepdims=True)
            hit = iota_c == pos
            j = h * topk + kk
            a_t[pl.ds(j, 1), :] = jnp.sum(jnp.where(hit, ca, 0), axis=0, keepdims=True)
            b_t[pl.ds(j, 1), :] = jnp.sum(jnp.where(hit, cb, 0), axis=0, keepdims=True)
            best.append(m)
            cs = jnp.where(hit, -jnp.inf, cs)
        e = [jnp.exp(v - best[0]) for v in best]
        denom = e[0]
        for v in e[1:]:
            denom = denom + v
        for kk in range(topk):
            g_t[pl.ds(h * topk + kk, 1), :] = e[kk] / denom
    a_ref[...] = a_t[...].T
    b_ref[...] = b_t[...].T
    g_ref[...] = g_t[...].T


def _peer_route(qp, sub_keys, cfg):
    n = qp.shape[0]
    heads, nkeys, topk = cfg.peer_heads, cfg.peer_nkeys, cfg.peer_topk
    dq = sub_keys.shape[2]
    nj = heads * topk
    tm = min(cfg.topk_tm, n)
    assert nj == V7X_LANES and qp.shape[1] == heads * 2 * dq
    cells = _peer_cells(topk)
    crow = -(-(len(cells) + 1) // V7X_SUBLANES) * V7X_SUBLANES
    outs = pl.pallas_call(
        functools.partial(_route_body, heads=heads, nkeys=nkeys, topk=topk, cells=tuple(cells)),
        out_shape=[jax.ShapeDtypeStruct((n, nj), I32), jax.ShapeDtypeStruct((n, nj), I32),
                   jax.ShapeDtypeStruct((n, nj), F32)],
        grid=(n // tm,),
        in_specs=[pl.BlockSpec((tm, qp.shape[1]), lambda i: (i, 0)),
                  pl.BlockSpec(sub_keys.shape, lambda i: (0, 0, 0))],
        out_specs=[pl.BlockSpec((tm, nj), lambda i: (i, 0))] * 3,
        scratch_shapes=[pltpu.VMEM((nj, tm), I32), pltpu.VMEM((nj, tm), I32), pltpu.VMEM((nj, tm), F32),
                        pltpu.VMEM((crow, tm), F32), pltpu.VMEM((crow, tm), I32), pltpu.VMEM((crow, tm), I32)],
        compiler_params=pltpu.CompilerParams(dimension_semantics=("parallel",),
                                             vmem_limit_bytes=_vmem_limit(0)),
        name="peer_route",
    )(qp, sub_keys.astype(BF16))
    return outs


W_PITCH_PAD = 8
HIGH_HALF = np.uint32(0xFFFF0000)
HALF_ULP_BF16 = np.uint32(0x8000)


def _expert_body(hn_ref, a_ref, b_ref, g_ref, ut_ref, v_ref, x_ref, o_ref, wmap, acc, *, nk, pitch, unroll):
    j = pl.program_id(1)
    tm = hn_ref.shape[0]
    te = ut_ref.shape[1]
    nblk = te // nk
    half = nk // 2
    contract1 = (((1,), (1,)), ((), ()))

    @pl.when(j == 0)
    def _():
        acc[...] = jnp.zeros_like(acc)
        row = lax.broadcasted_iota(I32, (nk, a_ref.shape[1]), 0)
        a_of_row = jnp.where(row < half, 2 * row, 2 * (row - half) + 1)

        def build(n, carry):
            ar = a_ref[pl.ds(n, 1), :]
            br = b_ref[pl.ds(n, 1), :]
            gr = g_ref[pl.ds(n, 1), :] * 0.5
            one_a = jnp.where(a_of_row == ar, 1.0, 0.0).astype(BF16)
            gate_b = jnp.where(row == br, gr, 0.0).astype(BF16)
            wn = lax.dot_general(one_a, gate_b, contract1, preferred_element_type=F32)
            lo = pltpu.bitcast(wn[:half], jnp.uint32)
            hi = pltpu.bitcast(wn[half:], jnp.uint32)
            packed = ((hi + HALF_ULP_BF16) & HIGH_HALF) | ((lo + HALF_ULP_BF16) >> 16)
            wmap[pl.ds(pl.multiple_of(n * pitch, V7X_SUBLANES), half), :] = packed
            return carry

        lax.fori_loop(0, tm, build, 0, unroll=unroll)

    x = jnp.dot(hn_ref[...], ut_ref[...], preferred_element_type=F32)
    act = x * (1.0 + lax.erf(x * (1.0 / math.sqrt(2.0))))
    parts = []
    for i in range(nblk // 2):
        w32 = wmap[pl.ds(j * (nblk // 2) + i, tm, stride=pitch), :]
        w_even = pltpu.bitcast(w32 << 16, F32)
        w_odd = pltpu.bitcast(w32 & HIGH_HALF, F32)
        parts.append((act[:, (2 * i) * nk:(2 * i + 1) * nk] * w_even).astype(BF16))
        parts.append((act[:, (2 * i + 1) * nk:(2 * i + 2) * nk] * w_odd).astype(BF16))
    pmat = jnp.concatenate(parts, axis=1)
    acc[...] += jnp.dot(pmat, v_ref[...], preferred_element_type=F32)

    @pl.when(j == pl.num_programs(1) - 1)
    def _():
        o_ref[...] = x_ref[...] + acc[...]


def _peer_experts(hn, a, b, g, ut, v, x2, cfg):
    n, d = x2.shape
    nexp = ut.shape[1]
    nk = cfg.peer_nkeys
    tm = min(cfg.peer_tm, n)
    te = min(cfg.peer_te, nexp)
    assert nk == V7X_LANES and nexp == nk * nk and te % (2 * nk) == 0 and nexp % te == 0
    pitch = nk // 2 + W_PITCH_PAD
    nj = a.shape[1]
    est = 2 * (tm * d * 2 + 3 * tm * nj * 4 + 2 * d * te * 2 + 2 * tm * d * 4) \
        + tm * pitch * nk * 4 + tm * d * 4 + 4 * tm * te * 4
    tile = lambda width: pl.BlockSpec((tm, width), lambda i, j: (i, 0))
    return pl.pallas_call(
        functools.partial(_expert_body, nk=nk, pitch=pitch, unroll=min(cfg.peer_unroll, tm)),
        out_shape=jax.ShapeDtypeStruct((n, d), F32),
        grid=(n // tm, nexp // te),
        in_specs=[tile(d), tile(nj), tile(nj), tile(nj),
                  pl.BlockSpec((d, te), lambda i, j: (0, j)),
                  pl.BlockSpec((te, d), lambda i, j: (j, 0)),
                  tile(d)],
        out_specs=tile(d),
        scratch_shapes=[pltpu.VMEM((tm * pitch, nk), jnp.uint32), pltpu.VMEM((tm, d), F32)],
        compiler_params=pltpu.CompilerParams(dimension_semantics=("parallel", "arbitrary"),
                                             vmem_limit_bytes=_vmem_limit(est)),
        name="peer_experts",
    )(hn, a, b, g, ut, v, x2)


class _RouteScratch(NamedTuple):
    s: object
    ts: object
    ti: object
    c_s: object
    c_a: object
    c_b: object
    best: object
    a_t: object
    b_t: object
    g_t: object


def _route_scores(qp_ref, keys_ref, rs):
    dq = keys_ref.shape[2]
    contract1 = (((1,), (1,)), ((), ()))
    for p in range(2):
        rs.s[p] = lax.dot_general(keys_ref[p], qp_ref[:, p * dq:(p + 1) * dq], contract1,
                                  preferred_element_type=F32)


def _route_first_stage(rs, it0, n_it):
    nkeys, th = rs.s.shape[1:]
    iota = lax.broadcasted_iota(I32, (nkeys, th), 0)
    for p in range(2):
        s = rs.s[p]
        for k in range(n_it):
            m = jnp.max(s, axis=0, keepdims=True)
            idx = jnp.min(jnp.where(s == m, iota, nkeys), axis=0, keepdims=True)
            rs.ts[p, pl.ds(it0 + k, 1), :] = m
            rs.ti[p, pl.ds(it0 + k, 1), :] = idx
            s = jnp.where(iota == idx, -jnp.inf, s)
        rs.s[p] = s


def _route_cells(rs, cells):
    crow, th = rs.c_s.shape
    ncell = len(cells)
    rs.c_s[pl.ds(ncell, crow - ncell), :] = jnp.full((crow - ncell, th), -jnp.inf, F32)
    rs.c_a[pl.ds(ncell, crow - ncell), :] = jnp.zeros((crow - ncell, th), I32)
    rs.c_b[pl.ds(ncell, crow - ncell), :] = jnp.zeros((crow - ncell, th), I32)
    for ci, (k1, k2) in enumerate(cells):
        rs.c_s[pl.ds(ci, 1), :] = rs.ts[0, pl.ds(k1, 1), :] + rs.ts[1, pl.ds(k2, 1), :]
        rs.c_a[pl.ds(ci, 1), :] = rs.ti[0, pl.ds(k1, 1), :]
        rs.c_b[pl.ds(ci, 1), :] = rs.ti[1, pl.ds(k2, 1), :]


def _route_second_stage(rs, it0, n_it, blk0, row0):
    crow, th = rs.c_s.shape
    iota = lax.broadcasted_iota(I32, (crow, th), 0)
    cs, ca, cb = rs.c_s[...], rs.c_a[...], rs.c_b[...]
    for k in range(n_it):
        m = jnp.max(cs, axis=0, keepdims=True)
        pos = jnp.min(jnp.where(cs == m, iota, crow), axis=0, keepdims=True)
        hit = iota == pos
        a_row = jnp.sum(jnp.where(hit, ca, 0), axis=0, keepdims=True)
        b_row = jnp.sum(jnp.where(hit, cb, 0), axis=0, keepdims=True)
        for c in range(th // V7X_LANES):
            cols = slice(c * V7X_LANES, (c + 1) * V7X_LANES)
            rs.a_t[blk0 + c, pl.ds(row0 + it0 + k, 1), :] = a_row[:, cols]
            rs.b_t[blk0 + c, pl.ds(row0 + it0 + k, 1), :] = b_row[:, cols]
        rs.best[pl.ds(it0 + k, 1), :] = m
        cs = jnp.where(hit, -jnp.inf, cs)
    rs.c_s[...] = cs


def _route_gates(rs, blk0, row0):
    topk, th = rs.best.shape
    best = rs.best[...]
    e = jnp.exp(best - best[0:1, :])
    gates = e / jnp.sum(e, axis=0, keepdims=True)
    for c in range(th // V7X_LANES):
        rs.g_t[blk0 + c, pl.ds(row0, topk), :] = gates[:, c * V7X_LANES:(c + 1) * V7X_LANES]


def _peer_fused_body(hn_ref, qp_ref, keys_ref, ut_ref, v_ref, x_ref, o_ref,
                     wmap, acc, a_cur, b_cur, g_cur, a_t, b_t, g_t, c_s, c_a, c_b,
                     *, nk, pitch, unroll, topk, cells, nhalf):
    r = pl.program_id(0)
    j = pl.program_id(1)
    tm = hn_ref.shape[0]
    th = qp_ref.shape[0]
    te = ut_ref.shape[1]
    nblk = te // nk
    half = nk // 2
    contract1 = (((1,), (1,)), ((), ()))

    @pl.when(j == 0)
    def _():
        acc[...] = jnp.zeros_like(acc)

        @pl.when(r == 0)
        def _():
            a_cur[...] = jnp.zeros_like(a_cur)
            b_cur[...] = jnp.zeros_like(b_cur)
            g_cur[...] = jnp.zeros_like(g_cur)

        row = lax.broadcasted_iota(I32, (nk, a_cur.shape[1]), 0)
        a_of_row = jnp.where(row < half, 2 * row, 2 * (row - half) + 1)

        def build(n, carry):
            ar = a_cur[pl.ds(n, 1), :]
            br = b_cur[pl.ds(n, 1), :]
            gr = g_cur[pl.ds(n, 1), :] * 0.5
            one_a = jnp.where(a_of_row == ar, 1.0, 0.0).astype(BF16)
            gate_b = jnp.where(row == br, gr, 0.0).astype(BF16)
            wn = lax.dot_general(one_a, gate_b, contract1, preferred_element_type=F32)
            lo = pltpu.bitcast(wn[:half], jnp.uint32)
            hi = pltpu.bitcast(wn[half:], jnp.uint32)
            packed = ((hi + HALF_ULP_BF16) & HIGH_HALF) | ((lo + HALF_ULP_BF16) >> 16)
            wmap[pl.ds(pl.multiple_of(n * pitch, V7X_SUBLANES), half), :] = packed
            return carry

        lax.fori_loop(0, tm, build, 0, unroll=unroll)

    x = jnp.dot(hn_ref[...], ut_ref[...], preferred_element_type=F32)
    act = x * (1.0 + lax.erf(x * (1.0 / math.sqrt(2.0))))
    parts = []
    for i in range(nblk // 2):
        w32 = wmap[pl.ds(j * (nblk // 2) + i, tm, stride=pitch), :]
        w_even = pltpu.bitcast(w32 << 16, F32)
        w_odd = pltpu.bitcast(w32 & HIGH_HALF, F32)
        parts.append((act[:, (2 * i) * nk:(2 * i + 1) * nk] * w_even).astype(BF16))
        parts.append((act[:, (2 * i + 1) * nk:(2 * i + 2) * nk] * w_odd).astype(BF16))
    pmat = jnp.concatenate(parts, axis=1)
    acc[...] += jnp.dot(pmat, v_ref[...], preferred_element_type=F32)

    @pl.when(j == pl.num_programs(1) - 1)
    def _():
        o_ref[...] = x_ref[...] + acc[...]
        for cb in range(tm // V7X_LANES):
            rows = slice(cb * V7X_LANES, (cb + 1) * V7X_LANES)
            a_cur[rows, :] = a_t[cb].T
            b_cur[rows, :] = b_t[cb].T
            g_cur[rows, :] = g_t[cb].T


def _peer_fused(hn, qp, sub_keys, ut, v, x2, cfg):
    n, d = x2.shape
    nexp = ut.shape[1]
    nk, heads, topk = cfg.peer_nkeys, cfg.peer_heads, cfg.peer_topk
    dq = sub_keys.shape[2]
    nj = heads * topk
    tm = min(cfg.peer_tm, n)
    te = min(cfg.peer_te, nexp)
    nsteps = nexp // te
    assert nk == V7X_LANES and nexp == nk * nk and te % (2 * nk) == 0 and nexp % te == 0
    assert nj == V7X_LANES and qp.shape[1] == heads * 2 * dq and n % tm == 0
    assert nsteps % heads == 0 and tm % (nsteps // heads) == 0
    nhalf = nsteps // heads
    th = tm // nhalf
    assert th % V7X_LANES == 0
    ntiles = n // tm
    pitch = nk // 2 + W_PITCH_PAD
    cells = _peer_cells(topk)
    crow = -(-(len(cells) + 1) // V7X_SUBLANES) * V7X_SUBLANES
    est = 2 * (tm * d * 2 + th * 2 * dq * 2 + 2 * d * te * 2 + 2 * tm * d * 4) \
        + tm * pitch * nk * 4 + tm * d * 4 + 6 * tm * nj * 4 + 3 * crow * th * 4 + 5 * tm * te * 4
    cur = lambda width: pl.BlockSpec((tm, width), lambda r, j: (jnp.maximum(r - 1, 0), 0))
    return pl.pallas_call(
        functools.partial(_peer_fused_body, nk=nk, pitch=pitch, unroll=min(cfg.peer_unroll, tm),
                          topk=topk, cells=tuple(cells), nhalf=nhalf),
        out_shape=jax.ShapeDtypeStruct((n, d), F32),
        grid=(ntiles + 1, nsteps),
        in_specs=[cur(d),
                  pl.BlockSpec((th, 2 * dq),
                               lambda r, j: (nhalf * jnp.minimum(r, ntiles - 1) + j % nhalf, j // nhalf)),
                  pl.BlockSpec(sub_keys.shape, lambda r, j: (0, 0, 0)),
                  pl.BlockSpec((d, te), lambda r, j: (0, j)),
                  pl.BlockSpec((te, d), lambda r, j: (j, 0)),
                  cur(d)],
        out_specs=cur(d),
        scratch_shapes=[pltpu.VMEM((tm * pitch, nk), jnp.uint32), pltpu.VMEM((tm, d), F32),
                        pltpu.VMEM((tm, nj), I32), pltpu.VMEM((tm, nj), I32), pltpu.VMEM((tm, nj), F32),
                        pltpu.VMEM((tm // V7X_LANES, nj, V7X_LANES), I32),
                        pltpu.VMEM((tm // V7X_LANES, nj, V7X_LANES), I32),
                        pltpu.VMEM((tm // V7X_LANES, nj, V7X_LANES), F32),
                        pltpu.VMEM((crow, th), F32), pltpu.VMEM((crow, th), I32), pltpu.VMEM((crow, th), I32)],
        compiler_params=pltpu.CompilerParams(dimension_semantics=("arbitrary", "arbitrary"),
                                             vmem_limit_bytes=_vmem_limit(est)),
        name="peer_fused",
    )(hn, qp, sub_keys.astype(BF16), ut, v, x2)


PEER_SUB = 2 * V7X_LANES


def _peer_pipe_body(hn_ref, qp_ref, keys_ref, ut_ref, v_ref, x_ref, o_ref,
                    wmap, acc, pmat, a_cur, b_cur, g_cur, s_scr, ts, ti, c_s, c_a, c_b, best, a_t, b_t, g_t,
                    *, nk, pitch, unroll, heads, topk, cells, nhalf, nsteps):
    r = pl.program_id(0)
    j = pl.program_id(1)
    tm = hn_ref.shape[0]
    th = qp_ref.shape[0]
    nsub, _, sub = ut_ref.shape
    nout = acc.shape[0]
    half = nk // 2
    ncol = th // V7X_LANES
    it_per = topk // nsub
    nj = heads * topk
    rs = _RouteScratch(s_scr, ts, ti, c_s, c_a, c_b, best, a_t, b_t, g_t)
    contract1 = (((1,), (1,)), ((), ()))

    @pl.when(j == 0)
    def _():
        @pl.when(r == 0)
        def _():
            for ref in (acc, pmat, a_cur, b_cur, g_cur, ts, ti):
                ref[...] = jnp.zeros_like(ref)

        @pl.when(r > 0)
        def _():
            blk0 = ((nsteps - 1) % nhalf) * ncol
            row0 = ((nsteps - 1) // nhalf) * topk
            _route_cells(rs, cells)
            _route_second_stage(rs, 0, topk, blk0, row0)
            _route_gates(rs, blk0, row0)
            for cb in range(tm // V7X_LANES):
                rows = slice(cb * V7X_LANES, (cb + 1) * V7X_LANES)
                a_cur[rows, :] = a_t[cb, 0:nj, :].T
                b_cur[rows, :] = b_t[cb, 0:nj, :].T
                g_cur[rows, :] = g_t[cb, 0:nj, :].T

        row = lax.broadcasted_iota(I32, (nk, nj), 0)
        a_of_row = jnp.where(row < half, 2 * row, 2 * (row - half) + 1)

        def build(n, carry):
            ar = a_cur[pl.ds(n, 1), :]
            br = b_cur[pl.ds(n, 1), :]
            gr = g_cur[pl.ds(n, 1), :] * 0.5
            one_a = jnp.where(a_of_row == ar, 1.0, 0.0).astype(BF16)
            gate_b = jnp.where(row == br, gr, 0.0).astype(BF16)
            wn = lax.dot_general(one_a, gate_b, contract1, preferred_element_type=F32)
            lo = pltpu.bitcast(wn[:half], jnp.uint32)
            hi = pltpu.bitcast(wn[half:], jnp.uint32)
            packed = ((hi + HALF_ULP_BF16) & HIGH_HALF) | ((lo + HALF_ULP_BF16) >> 16)
            wmap[pl.ds(pl.multiple_of(n * pitch, V7X_SUBLANES), half), :] = packed
            return carry

        lax.fori_loop(0, tm, build, 0, unroll=unroll)

    ju = jnp.maximum(j - 1, 0)
    c_blk0 = lax.rem(ju, nhalf) * ncol
    c_row0 = pl.multiple_of(jnp.where(j == 0, nj, lax.div(ju, nhalf) * topk), topk)
    _route_cells(rs, cells)
    _route_scores(qp_ref, keys_ref, rs)
    cur = lax.rem(j, 2)
    prev = 1 - cur

    def trip(sb, carry):
        ob = lax.rem(sb, nout)
        k0 = lax.div(sb, nout) * nout
        part = acc[ob]
        for k in range(nout):
            rows = pl.ds(pl.multiple_of((k0 + k) * sub, sub), sub)
            part = part + jnp.dot(pmat[prev * nsub + k0 + k], v_ref[ob, rows, :], preferred_element_type=F32)
        acc[ob] = part
        xv = jnp.dot(hn_ref[...], ut_ref[sb], preferred_element_type=F32)
        act = xv * (1.0 + lax.erf(xv * (1.0 / math.sqrt(2.0))))
        w32 = wmap[pl.ds(j * nsub + sb, tm, stride=pitch), :]
        pm = jnp.concatenate([act[:, :nk] * pltpu.bitcast(w32 << 16, F32),
                              act[:, nk:] * pltpu.bitcast(w32 & HIGH_HALF, F32)], axis=1)
        pmat[cur * nsub + sb] = pm.astype(BF16)
        _route_first_stage(rs, sb * it_per, it_per)
        _route_second_stage(rs, sb * it_per, it_per, c_blk0, c_row0)
        return carry

    lax.fori_loop(0, nsub, trip, 0)
    _route_gates(rs, c_blk0, c_row0)

    @pl.when(j == 0)
    def _():
        for k in range(nout):
            o_ref[:, k * sub:(k + 1) * sub] = x_ref[:, k * sub:(k + 1) * sub] + acc[k]
        acc[...] = jnp.zeros_like(acc)


def _peer_fused(hn, qp, sub_keys, u3, v4, x2, cfg):
    n, d = x2.shape
    nexp = u3.shape[0] * u3.shape[2]
    nk, heads, topk = cfg.peer_nkeys, cfg.peer_heads, cfg.peer_topk
    dq = sub_keys.shape[2]
    nj = heads * topk
    sub = PEER_SUB
    nout = d // sub
    te = max(d, min(cfg.peer_te, nexp) // d * d)
    nsub = te // sub
    nsteps = nexp // te
    assert nk == V7X_LANES and nexp == nk * nk and sub == 2 * nk and d % sub == 0 and nexp % te == 0
    assert te % d == 0 and nj == V7X_LANES and qp.shape[1] == heads * 2 * dq and topk % nsub == 0
    assert nsteps % heads == 0 and nsteps % 2 == 0
    nhalf = nsteps // heads
    tm = min(cfg.peer_tm, n)
    tm = max(tm, nhalf * V7X_LANES)
    assert n % tm == 0 and tm % nhalf == 0
    th = tm // nhalf
    assert th % V7X_LANES == 0
    ntiles = n // tm
    pitch = nk // 2 + W_PITCH_PAD
    cells = _peer_cells(topk)
    crow = -(-(len(cells) + 1) // V7X_SUBLANES) * V7X_SUBLANES
    nblk = tm // V7X_LANES
    est = 2 * (tm * d * 2 + th * 2 * dq * 2 + 2 * d * te * 2 + 2 * tm * d * 4) \
        + tm * pitch * nk * 4 + tm * d * 4 + 2 * tm * te * 2 + 3 * tm * nj * 4 \
        + 3 * nblk * (nj + topk) * V7X_LANES * 4 + (3 * crow + 2 * nk + 5 * topk) * th * 4 + 6 * tm * sub * 4
    tile_in = pl.BlockSpec((tm, d), lambda r, j: (jnp.clip(r - 1, 0, ntiles - 1), 0))
    tile_out = pl.BlockSpec((tm, d), lambda r, j: (jnp.clip(jnp.where(j == 0, r - 2, r - 1), 0, ntiles - 1), 0))
    return pl.pallas_call(
        functools.partial(_peer_pipe_body, nk=nk, pitch=pitch, unroll=min(cfg.peer_unroll, tm), heads=heads,
                          topk=topk, cells=tuple(cells), nhalf=nhalf, nsteps=nsteps),
        out_shape=jax.ShapeDtypeStruct((n, d), F32),
        grid=(ntiles + 2, nsteps),
        in_specs=[tile_in,
                  pl.BlockSpec((th, 2 * dq),
                               lambda r, j: (nhalf * jnp.minimum(r, ntiles - 1) + j % nhalf, j // nhalf)),
                  pl.BlockSpec(sub_keys.shape, lambda r, j: (0, 0, 0)),
                  pl.BlockSpec((nsub, d, sub), lambda r, j: (j, 0, 0)),
                  pl.BlockSpec((nout, te, sub), lambda r, j: (0, (j + nsteps - 1) % nsteps, 0)),
                  tile_out],
        out_specs=tile_out,
        scratch_shapes=[pltpu.VMEM((tm * pitch, nk), jnp.uint32),
                        pltpu.VMEM((nout, tm, sub), F32), pltpu.VMEM((2 * nsub, tm, sub), BF16),
                        pltpu.VMEM((tm, nj), I32), pltpu.VMEM((tm, nj), I32), pltpu.VMEM((tm, nj), F32),
                        pltpu.VMEM((2, nk, th), F32), pltpu.VMEM((2, topk, th), F32), pltpu.VMEM((2, topk, th), I32),
                        pltpu.VMEM((crow, th), F32), pltpu.VMEM((crow, th), I32), pltpu.VMEM((crow, th), I32),
                        pltpu.VMEM((topk, th), F32),
                        pltpu.VMEM((nblk, nj + topk, V7X_LANES), I32), pltpu.VMEM((nblk, nj + topk, V7X_LANES), I32),
                        pltpu.VMEM((nblk, nj + topk, V7X_LANES), F32)],
        compiler_params=pltpu.CompilerParams(dimension_semantics=("arbitrary", "arbitrary"),
                                             vmem_limit_bytes=_vmem_limit(est)),
        name="peer_fused",
    )(hn, qp, sub_keys.astype(BF16), u3, v4, x2)


def _peer(x2, gain, w_q_bf16, sub_keys, u3, v4, cfg):
    nq = w_q_bf16.shape[1]
    qp, hn = _norm_matmul(x2, gain, w_q_bf16, [(0, nq)], [BF16], cfg, emit_hn=True)
    return _peer_fused(hn, qp, sub_keys, u3, v4, x2, cfg)


def _trunk(x, p, cfg):
    b, t, d = x.shape
    x2 = x.reshape(b * t, d)
    depth = p["norm_mix"].shape[0]
    for l in range(depth):
        i = l // 2
        if l % 2 == 0:
            rw = 4 * cfg.ret_heads * cfg.ret_dk
            w_in = p["ab_w_in"][i]
            zq, u = _norm_matmul(x2, p["norm_mix"][l], w_in, [(0, rw), (rw, w_in.shape[1])], [F32, F32], cfg)
            ret = _retention(zq, p["ab_ret_decay"][i], p["ab_ret_gn"][i], b, t, cfg)
            ssm = _s5(u, tuple(p[k][i] for k in ("ab_s5_lam_re", "ab_s5_lam_im", "ab_s5_log_dt", "ab_s5_b_re",
                                                  "ab_s5_b_im", "ab_s5_c_re", "ab_s5_c_im", "ab_s5_d",
                                                  "ab_s5_glu_w", "ab_s5_glu_b")), b, t, cfg)
            w_out = p["ab_w_out"][i]
            nr = ret.shape[1]
            x2 = _matmul_residual([ret, ssm], [w_out[:nr], w_out[nr:]], x2, cfg)
        else:
            w_qkv = p["na_w_qkv"][i]
            (z,) = _norm_matmul(x2, p["norm_mix"][l], w_qkv, [(0, w_qkv.shape[1])], [F32], cfg)
            att = _neighbourhood_attention(z, p["na_q_gain"][i], p["na_k_gain"][i], p["na_rpb"][i], b, t, cfg)
            x2 = _matmul_residual([att], [p["na_w_o"][i]], x2, cfg)
        x2 = _peer(x2, p["norm_ffn"][l], p["peer_w_q"][l], p["peer_sub_keys"][l], p["peer_u3"][l],
                   p["peer_v4"][l], cfg)
    return x2.reshape(b, t, d)


def _prepare(params):
    p = dict(params)
    for k in ("ab_w_in", "ab_w_out", "na_w_qkv", "na_w_o", "peer_w_q"):
        p[k] = params[k].astype(BF16)
    u, v = params["peer_u"].astype(BF16), params["peer_v"].astype(BF16)
    layers, nexp, d = u.shape
    p["peer_u3"] = u.reshape(layers, nexp // PEER_SUB, PEER_SUB, d).transpose(0, 1, 3, 2)
    p["peer_v4"] = v.reshape(layers, nexp, d // PEER_SUB, PEER_SUB).transpose(0, 2, 1, 3)
    return p


def _forward(x_prompt, x_sample, params, cfg=Cfg()):
    p = _prepare(params)
    return _trunk(x_prompt, p, cfg), _trunk(x_sample, p, cfg)


def kernel(x_prompt, x_sample, norm_mix, norm_ffn, ab_w_in, ab_ret_decay, ab_ret_gn, ab_s5_lam_re, ab_s5_lam_im, ab_s5_log_dt, ab_s5_b_re, ab_s5_b_im, ab_s5_c_re, ab_s5_c_im, ab_s5_d, ab_s5_glu_w, ab_s5_glu_b, ab_w_out, na_w_qkv, na_q_gain, na_k_gain, na_rpb, na_w_o, peer_w_q, peer_sub_keys, peer_u, peer_v):
    params = dict(norm_mix=norm_mix, norm_ffn=norm_ffn, ab_w_in=ab_w_in, ab_ret_decay=ab_ret_decay,
                  ab_ret_gn=ab_ret_gn, ab_s5_lam_re=ab_s5_lam_re, ab_s5_lam_im=ab_s5_lam_im,
                  ab_s5_log_dt=ab_s5_log_dt, ab_s5_b_re=ab_s5_b_re, ab_s5_b_im=ab_s5_b_im,
                  ab_s5_c_re=ab_s5_c_re, ab_s5_c_im=ab_s5_c_im, ab_s5_d=ab_s5_d, ab_s5_glu_w=ab_s5_glu_w,
                  ab_s5_glu_b=ab_s5_glu_b, ab_w_out=ab_w_out, na_w_qkv=na_w_qkv, na_q_gain=na_q_gain,
                  na_k_gain=na_k_gain, na_rpb=na_rpb, na_w_o=na_w_o, peer_w_q=peer_w_q,
                  peer_sub_keys=peer_sub_keys, peer_u=peer_u, peer_v=peer_v)
    return _forward(x_prompt, x_sample, params)
```

```python
import functools
import math
from typing import NamedTuple

import numpy as np
import jax
import jax.numpy as jnp
from jax import lax
from jax.experimental import pallas as pl
from jax.experimental.pallas import tpu as pltpu

F32 = jnp.float32
BF16 = jnp.bfloat16
I32 = jnp.int32

V7X_LANES = 128
V7X_SUBLANES = 8
V7X_VMEM_BYTES = 64 * 2**20
VMEM_LIMIT_CAP = V7X_VMEM_BYTES - 8 * 2**20

NEG_BIG = -1e30


class Cfg(NamedTuple):
    eps: float = 1e-6
    grid_w: int = 64
    ret_heads: int = 4
    ret_dk: int = 128
    ret_chunk: int = 128
    rope_base: float = 10000.0
    s5_group: int = 16
    s5_state: int = 64
    s5_chunk: int = 16
    na_heads: int = 16
    na_kh: int = 8
    na_kw: int = 16
    peer_heads: int = 8
    peer_nkeys: int = 128
    peer_topk: int = 16
    tm: int = 512
    peer_tm: int = 512
    peer_unroll: int = 32
    na_unroll: int = 8
    ret_group: int = 8
    peer_te: int = 1024


def _vmem_limit(nbytes):
    return int(min(VMEM_LIMIT_CAP, max(32 * 2**20, nbytes)))


def _gelu(x):
    return 0.5 * x * (1.0 + lax.erf(x * (1.0 / math.sqrt(2.0))))


def _sigmoid(x):
    return 1.0 / (1.0 + jnp.exp(-x))


def _norm_mm_body(x_ref, g_ref, w_ref, *out_refs, splits, emit_hn, eps):
    x = x_ref[...]
    y = x * lax.rsqrt(jnp.mean(x * x, axis=-1, keepdims=True) + eps) * g_ref[...]
    yb = y.astype(BF16)
    z = jnp.dot(yb, w_ref[...], preferred_element_type=F32)
    for r, (s, e) in zip(out_refs, splits):
        r[...] = z[:, s:e].astype(r.dtype)
    if emit_hn:
        out_refs[len(splits)][...] = yb


def _norm_matmul(x2, gain, w_bf16, splits, dtypes, cfg, emit_hn=False):
    n, d = x2.shape
    nout = w_bf16.shape[1]
    tm = min(cfg.tm, n)
    out_shape = [jax.ShapeDtypeStruct((n, e - s), dt) for (s, e), dt in zip(splits, dtypes)]
    out_specs = [pl.BlockSpec((tm, e - s), lambda i: (i, 0)) for (s, e) in splits]
    if emit_hn:
        out_shape.append(jax.ShapeDtypeStruct((n, d), BF16))
        out_specs.append(pl.BlockSpec((tm, d), lambda i: (i, 0)))
    est = 2 * (tm * d * 4 + d * nout * 2 + tm * nout * 4 + tm * d * 2) + 2 * tm * nout * 4
    return pl.pallas_call(
        functools.partial(_norm_mm_body, splits=tuple(splits), emit_hn=emit_hn, eps=cfg.eps),
        out_shape=out_shape,
        grid=(n // tm,),
        in_specs=[pl.BlockSpec((tm, d), lambda i: (i, 0)),
                  pl.BlockSpec((1, d), lambda i: (0, 0)),
                  pl.BlockSpec((d, nout), lambda i: (0, 0))],
        out_specs=out_specs,
        compiler_params=pltpu.CompilerParams(dimension_semantics=("parallel",),
                                             vmem_limit_bytes=_vmem_limit(est)),
        name="norm_matmul",
    )(x2, gain.reshape(1, d).astype(F32), w_bf16)


def _mm_res_body(*refs, n_in):
    acc = refs[2 * n_in][...]
    for a, w in zip(refs[:n_in], refs[n_in:2 * n_in]):
        acc = acc + jnp.dot(a[...], w[...], preferred_element_type=F32)
    refs[-1][...] = acc


def _matmul_residual(a_list, w_list, res, cfg):
    n, d = res.shape
    tm = min(cfg.tm, n)
    n_in = len(a_list)
    in_specs = ([pl.BlockSpec((tm, a.shape[1]), lambda i: (i, 0)) for a in a_list]
                + [pl.BlockSpec(w.shape, lambda i: (0, 0)) for w in w_list]
                + [pl.BlockSpec((tm, d), lambda i: (i, 0))])
    est = 2 * (sum(tm * a.shape[1] * 2 for a in a_list) + sum(w.size * 2 for w in w_list)
               + 2 * tm * d * 4) + 2 * tm * d * 4
    return pl.pallas_call(
        functools.partial(_mm_res_body, n_in=n_in),
        out_shape=jax.ShapeDtypeStruct((n, d), F32),
        grid=(n // tm,),
        in_specs=in_specs,
        out_specs=pl.BlockSpec((tm, d), lambda i: (i, 0)),
        compiler_params=pltpu.CompilerParams(dimension_semantics=("parallel",),
                                             vmem_limit_bytes=_vmem_limit(est)),
        name="matmul_residual",
    )(*a_list, *w_list, res)


def _ret_tables(ret_decay, chunk, width):
    lg = -jax.nn.softplus(-ret_decay.astype(F32))
    pos = jnp.arange(chunk, dtype=F32)
    diff = pos[:, None] - pos[None, :]
    d_f = jnp.where(diff >= 0, jnp.exp(lg[0][:, None, None] * jnp.maximum(diff, 0.0)[None]), 0.0)
    d_b = jnp.where(diff < 0, jnp.exp(lg[1][:, None, None] * jnp.maximum(-diff, 0.0)[None]), 0.0)
    cols = [jnp.exp(lg[0][:, None] * (chunk - 1.0 - pos)[None]),
            jnp.exp(lg[1][:, None] * pos[None]),
            jnp.exp(lg[0][:, None] * (pos + 1.0)[None]),
            jnp.exp(lg[1][:, None] * (chunk - pos)[None]),
            jnp.broadcast_to(jnp.exp(lg[0] * chunk)[:, None], (lg.shape[1], chunk)),
            jnp.broadcast_to(jnp.exp(lg[1] * chunk)[:, None], (lg.shape[1], chunk))]
    tab = jnp.stack(cols, axis=1)
    return d_f + d_b, jnp.broadcast_to(tab[..., None], tab.shape + (width,))


def _rope_tables(t, half, base):
    inv = base ** (-jnp.arange(half, dtype=F32) / half)
    ang = jnp.arange(t, dtype=F32)[:, None] * inv[None, :]
    cos, sin = jnp.cos(ang), jnp.sin(ang)
    return jnp.concatenate([cos, cos], axis=1), jnp.concatenate([-sin, sin], axis=1)


def _ret_body(q_ref, k_ref, v_ref, g_ref, cos_ref, sin_ref, d_ref, tab_ref, gn_ref, o_ref,
              qs, ks, sf, sb, *, chunk, nc, kscale, eps, group):
    dk = q_ref.shape[1]
    half = dk // 2
    cos, sin = cos_ref[...], sin_ref[...]
    q = q_ref[...]
    qs[...] = q * cos + pltpu.roll(q, half, 1) * sin
    k = k_ref[...]
    ks[...] = (k * cos + pltpu.roll(k, half, 1) * sin) * kscale
    contract0 = (((0,), (0,)), ((), ()))
    contract1 = (((1,), (1,)), ((), ()))

    k_f, k_b, q_f, q_b = tab_ref[0, 0], tab_ref[0, 1], tab_ref[0, 2], tab_ref[0, 3]
    cd_f, cd_b = tab_ref[0, 4], tab_ref[0, 5]

    def increments(n, carry):
        r0 = pl.multiple_of(n * chunk, chunk)
        kc = ks[pl.ds(r0, chunk), :]
        vc = v_ref[pl.ds(r0, chunk), :].astype(BF16)
        sf[n] = lax.dot_general((kc * k_f).astype(BF16), vc, contract0, preferred_element_type=F32)
        sb[n] = lax.dot_general((kc * k_b).astype(BF16), vc, contract0, preferred_element_type=F32)
        return carry

    lax.fori_loop(0, nc, increments, 0, unroll=group)

    def sweep_f(n, state):
        inc = sf[n]
        sf[n] = state
        return state * cd_f + inc

    def sweep_b(i, state):
        n = nc - 1 - i
        inc = sb[n]
        sb[n] = state
        return state * cd_b + inc

    zero = jnp.zeros((dk, v_ref.shape[1]), F32)
    lax.fori_loop(0, nc, sweep_f, zero)
    lax.fori_loop(0, nc, sweep_b, zero)

    def outputs(gi, carry):
        first = []
        for u in range(group):
            n = gi * group + u
            r0 = pl.multiple_of(n * chunk, chunk)
            qc = qs[pl.ds(r0, chunk), :]
            s = lax.dot_general(qc.astype(BF16), ks[pl.ds(r0, chunk), :].astype(BF16), contract1,
                                preferred_element_type=F32)
            cross = jnp.dot((qc * q_f).astype(BF16), sf[n].astype(BF16), preferred_element_type=F32)
            cross = cross + jnp.dot((qc * q_b).astype(BF16), sb[n].astype(BF16), preferred_element_type=F32)
            first.append((r0, s, cross))
        for r0, s, cross in first:
            vc = v_ref[pl.ds(r0, chunk), :].astype(BF16)
            o = jnp.dot((s * d_ref[0]).astype(BF16), vc, preferred_element_type=F32) + cross
            oc = o - jnp.mean(o, axis=-1, keepdims=True)
            o = oc * lax.rsqrt(jnp.mean(oc * oc, axis=-1, keepdims=True) + eps)
            g = g_ref[pl.ds(r0, chunk), :]
            o_ref[pl.ds(r0, chunk), :] = (o * gn_ref[...] * (g * _sigmoid(g))).astype(o_ref.dtype)
        return carry

    lax.fori_loop(0, nc // group, outputs, 0)


def _retention(zq, ret_decay, ret_gn, b, t, cfg):
    h, dk, c = cfg.ret_heads, cfg.ret_dk, cfg.ret_chunk
    assert t % c == 0 and zq.shape[1] == 4 * h * dk
    nc = t // c
    dmat, tab = _ret_tables(ret_decay, c, dk)
    cos2, sin2 = _rope_tables(t, dk // 2, cfg.rope_base)
    blk = lambda off: pl.BlockSpec((t, dk), lambda bi, hi, off=off: (bi, off + hi))
    est = 2 * (4 * t * dk * 4 + 2 * t * dk * 4 + c * c * 4 + 6 * c * dk * 4 + t * dk * 2) \
        + 2 * t * dk * 4 + 2 * nc * dk * dk * 4 + 8 * t * dk * 4
    return pl.pallas_call(
        functools.partial(_ret_body, chunk=c, nc=nc, kscale=dk ** -0.5, eps=cfg.eps,
                          group=math.gcd(cfg.ret_group, nc)),
        out_shape=jax.ShapeDtypeStruct((b * t, h * dk), BF16),
        grid=(b, h),
        in_specs=[blk(0), blk(h), blk(2 * h), blk(3 * h),
                  pl.BlockSpec((t, dk), lambda bi, hi: (0, 0)),
                  pl.BlockSpec((t, dk), lambda bi, hi: (0, 0)),
                  pl.BlockSpec((1, c, c), lambda bi, hi: (hi, 0, 0)),
                  pl.BlockSpec((1, 6, c, dk), lambda bi, hi: (hi, 0, 0, 0)),
                  pl.BlockSpec((1, dk), lambda bi, hi: (0, hi))],
        out_specs=pl.BlockSpec((t, dk), lambda bi, hi: (bi, hi)),
        scratch_shapes=[pltpu.VMEM((t, dk), F32), pltpu.VMEM((t, dk), F32),
                        pltpu.VMEM((nc, dk, dk), F32), pltpu.VMEM((nc, dk, dk), F32)],
        compiler_params=pltpu.CompilerParams(dimension_semantics=("parallel", "parallel"),
                                             vmem_limit_bytes=_vmem_limit(est)),
        name="retention",
    )(zq, zq, zq, zq, cos2, sin2, dmat, tab, ret_gn.reshape(1, h * dk).astype(F32))


def _s5_tables(lam_re, lam_im, log_dt, b_re, b_im, c_re, c_im, d_skip, glu_w, glu_b, L, nsteps):
    lam = lax.complex(lam_re.astype(F32), lam_im.astype(F32))
    ldt = lam * jnp.exp(log_dt.astype(F32))[..., None]
    lam_bar = jnp.exp(ldt)
    bmat = lax.complex(b_re.astype(F32), b_im.astype(F32))
    b_bar = ((lam_bar - 1.0) / lam)[..., None] * bmat[None]
    cmat = lax.complex(c_re.astype(F32), c_im.astype(F32))
    g, p, cg = bmat.shape
    tau = jnp.arange(L + 1, dtype=F32)
    pw = jnp.exp(ldt[:, :, None, :] * tau[None, None, :, None])
    kern = jnp.real(jnp.einsum('dgop,dgtp,dgpi->dgtoi', cmat, pw[:, :, :L], b_bar))
    li = np.arange(L)
    lag = li[None, :] - li[:, None]
    k_f = jnp.where((lag >= 0)[None, :, :, None, None], kern[0][:, np.clip(lag, 0, L - 1)], 0.0)
    k_b = jnp.where((lag <= 0)[None, :, :, None, None], kern[1][:, np.clip(-lag, 0, L - 1)], 0.0)
    m = (k_f + k_b).transpose(0, 1, 4, 2, 3).reshape(g, L * cg, L * cg)

    def cat(z):
        return jnp.concatenate([jnp.real(z), jnp.imag(z)], axis=-1)

    inc_f = cat(jnp.einsum('glp,gpi->glip', pw[0][:, L - 1 - li], b_bar[0])).reshape(g, L * cg, 2 * p)
    inc_b = cat(jnp.einsum('glp,gpi->glip', pw[1][:, li], b_bar[1])).reshape(g, L * cg, 2 * p)

    def out_mat(z):
        return jnp.concatenate([jnp.real(z), -jnp.imag(z)], axis=1).reshape(g, 2 * p, L * cg)

    out_f = out_mat(jnp.einsum('gop,glp->gplo', cmat[0], pw[0][:, li + 1]))
    out_b = out_mat(jnp.einsum('gop,glp->gplo', cmat[1], pw[1][:, L - li]))
    eye = jnp.eye(L, dtype=F32)
    glu = jnp.einsum('lm,gce->glcme', eye, glu_w.astype(F32)).reshape(g, L * cg, L * cg)
    vecs = jnp.stack([jnp.tile(d_skip.astype(F32), (1, L)), jnp.tile(glu_b.astype(F32), (1, L))], axis=1)
    steps = (2.0 ** jnp.arange(nsteps, dtype=F32)) * L
    a = jnp.exp(ldt[:, :, None, :] * steps[None, None, :, None])
    scan = jnp.stack([jnp.concatenate([jnp.real(a), jnp.real(a)], -1),
                      jnp.concatenate([-jnp.imag(a), jnp.imag(a)], -1)], axis=3)
    scan = scan.transpose(1, 0, 2, 3, 4).reshape(g, 2 * nsteps * 2, 2 * p)
    return (m.astype(BF16), inc_f.astype(BF16), inc_b.astype(BF16), out_f.astype(BF16),
            out_b.astype(BF16), glu.astype(BF16), vecs, scan)


def _s5_body(u_ref, m_ref, incf_ref, incb_ref, outf_ref, outb_ref, glu_ref, vec_ref, scan_ref, o_ref,
             *, ncs, nsteps):
    u = u_ref[0]
    rows = u.shape[0]
    ub = u.astype(BF16)
    y = jnp.dot(ub, m_ref[0], preferred_element_type=F32)
    xf = jnp.dot(ub, incf_ref[0], preferred_element_type=F32)
    xb = jnp.dot(ub, incb_ref[0], preferred_element_type=F32)
    p2 = xf.shape[1]
    cidx = lax.rem(lax.broadcasted_iota(I32, (rows, 1), 0), ncs)
    for kk in range(nsteps):
        s = 1 << kk
        a_f, b_f = scan_ref[0, pl.ds(2 * kk, 1), :], scan_ref[0, pl.ds(2 * kk + 1, 1), :]
        a_b = scan_ref[0, pl.ds(2 * nsteps + 2 * kk, 1), :]
        b_b = scan_ref[0, pl.ds(2 * nsteps + 2 * kk + 1, 1), :]
        pf = pltpu.roll(xf, s, 0)
        xf = xf + jnp.where(cidx >= s, a_f * pf + b_f * pltpu.roll(pf, p2 // 2, 1), 0.0)
        pb = pltpu.roll(xb, rows - s, 0)
        xb = xb + jnp.where(cidx < ncs - s, a_b * pb + b_b * pltpu.roll(pb, p2 // 2, 1), 0.0)
    x_prev = jnp.where(cidx >= 1, pltpu.roll(xf, 1, 0), 0.0)
    x_next = jnp.where(cidx < ncs - 1, pltpu.roll(xb, rows - 1, 0), 0.0)
    y = y + jnp.dot(x_prev.astype(BF16), outf_ref[0], preferred_element_type=F32)
    y = y + jnp.dot(x_next.astype(BF16), outb_ref[0], preferred_element_type=F32)
    y = y + u * vec_ref[0, pl.ds(0, 1), :]
    yg = _gelu(y)
    z = jnp.dot(yg.astype(BF16), glu_ref[0], preferred_element_type=F32) + vec_ref[0, pl.ds(1, 1), :]
    o_ref[0] = (yg * _sigmoid(z)).astype(o_ref.dtype)


def _s5(u, params, b, t, cfg):
    lam_re, lam_im, log_dt, b_re, b_im, c_re, c_im, d_skip, glu_w, glu_b = params
    L, cg = cfg.s5_chunk, cfg.s5_group
    n, ch = u.shape
    g = ch // cg
    assert t % L == 0
    ncs = t // L
    nsteps = max(1, (ncs - 1).bit_length())
    tabs = _s5_tables(lam_re, lam_im, log_dt, b_re, b_im, c_re, c_im, d_skip, glu_w, glu_b, L, nsteps)
    uc = u.reshape(n // L, L, g, cg).transpose(2, 0, 1, 3).reshape(g, n // L, L * cg)
    seqs = max(1, min(b, 1024 // ncs))
    while b % seqs:
        seqs -= 1
    rows = seqs * ncs
    w = L * cg
    p2 = 2 * cfg.s5_state
    per_g = lambda shape: pl.BlockSpec((1,) + shape, lambda gi, ri: (gi, 0, 0))
    est = 2 * (rows * w * 4 + rows * w * 2) + 2 * 2 * (3 * w * w + 4 * w * p2) + 12 * rows * w * 4
    out = pl.pallas_call(
        functools.partial(_s5_body, ncs=ncs, nsteps=nsteps),
        out_shape=jax.ShapeDtypeStruct((g, n // L, w), BF16),
        grid=(g, (n // L) // rows),
        in_specs=[pl.BlockSpec((1, rows, w), lambda gi, ri: (gi, ri, 0)),
                  per_g((w, w)), per_g((w, p2)), per_g((w, p2)), per_g((p2, w)), per_g((p2, w)),
                  per_g((w, w)), per_g((2, w)), per_g((4 * nsteps, p2))],
        out_specs=pl.BlockSpec((1, rows, w), lambda gi, ri: (gi, ri, 0)),
        compiler_params=pltpu.CompilerParams(dimension_semantics=("parallel", "parallel"),
                                             vmem_limit_bytes=_vmem_limit(est)),
        name="s5",
    )(uc, *tabs)
    return out.reshape(g, n // L, L, cg).transpose(1, 2, 0, 3).reshape(n, ch)


def _na_bias_tables(rpb, cfg, rows):
    kh, kw, w = cfg.na_kh, cfg.na_kw, cfg.grid_w
    j = np.arange(w)
    c_start = np.clip(j - kw // 2, 0, w - kw)
    c = np.arange(w)
    inside = (c[None, :] >= c_start[:, None]) & (c[None, :] < c_start[:, None] + kw)
    by_row = jnp.stack([rpb.astype(F32)[:, kh - 1 - dl:2 * kh - 1 - dl] for dl in range(kh)], axis=1)
    padded = jnp.pad(by_row, ((0, 0), (0, 0), (0, 0), (w - kw, w - kw)))
    bias = jnp.stack([padded[..., w - 1 - jq:2 * w - 1 - jq] for jq in range(w)], axis=2)
    bias = jnp.where(inside[None, None, :, None, :], bias, NEG_BIG)
    return bias.reshape(rpb.shape[0], kh, w, kh * w)


def _na_body(q_ref, k_ref, v_ref, qg_ref, kg_ref, tab_ref, o_ref, qn, kn, vb, *, w, rows, kh, dh, eps,
             unroll):
    lo = lax.broadcasted_iota(I32, (1, 2 * dh), 1) < dh
    same_head = ((lax.broadcasted_iota(I32, (2 * dh, 2 * dh), 0) < dh)
                 == (lax.broadcasted_iota(I32, (2 * dh, 2 * dh), 1) < dh))
    head_mean = jnp.where(same_head, 1.0 / dh, 0.0).astype(BF16)

    def head_norm(x, gain):
        x2 = x * x
        hi = x2.astype(BF16)
        rest = (x2 - hi.astype(F32)).astype(BF16)
        ms = (jnp.dot(hi, head_mean, preferred_element_type=F32)
              + jnp.dot(rest, head_mean, preferred_element_type=F32))
        return x * lax.rsqrt(ms + eps) * gain

    qh = head_norm(q_ref[...], qg_ref[...]) * dh ** -0.5
    qn[0] = jnp.where(lo, qh, 0.0).astype(BF16)
    qn[1] = jnp.where(lo, 0.0, qh).astype(BF16)
    kn[...] = head_norm(k_ref[...], kg_ref[...]).astype(BF16)
    vb[...] = v_ref[...].astype(BF16)
    contract1 = (((1,), (1,)), ((), ()))

    def group(gi, carry):
        chains = []
        for u in range(unroll):
            r = gi * unroll + u
            rs = jnp.clip(r - kh // 2, 0, rows - kh)
            q0 = pl.multiple_of(r * w, w)
            k0 = pl.multiple_of(rs * w, w)
            for hh in range(2):
                s = lax.dot_general(qn[hh, pl.ds(q0, w), :], kn[pl.ds(k0, kh * w), :], contract1,
                                    preferred_element_type=F32)
                chains.append((s + tab_ref[hh, r - rs], k0, q0))
        outs = []
        for s, k0, q0 in chains:
            p = jnp.exp(s - jnp.max(s, axis=-1, keepdims=True))
            p = p / jnp.sum(p, axis=-1, keepdims=True)
            outs.append(jnp.dot(p.astype(BF16), vb[pl.ds(k0, kh * w), :], preferred_element_type=F32))
        for u in range(unroll):
            q0 = chains[2 * u][2]
            o_ref[pl.ds(q0, w), :] = jnp.where(lo, outs[2 * u], outs[2 * u + 1]).astype(o_ref.dtype)
        return carry

    lax.fori_loop(0, rows // unroll, group, 0)


def _neighbourhood_attention(z, q_gain, k_gain, rpb, b, t, cfg):
    h, w, kh = cfg.na_heads, cfg.grid_w, cfg.na_kh
    dh = z.shape[1] // (3 * h)
    rows = t // w
    assert t % w == 0 and rows >= kh and h % 2 == 0 and 2 * dh == V7X_LANES
    tab = _na_bias_tables(rpb, cfg, rows)
    hp = h // 2
    blk = lambda off: pl.BlockSpec((t, 2 * dh), lambda bi, pi, off=off: (bi, off + pi))
    gain = pl.BlockSpec((1, 2 * dh), lambda bi, pi: (0, pi))
    est = 2 * (3 * t * 2 * dh * 4 + 2 * kh * w * kh * w * 4 + t * 2 * dh * 2) + 4 * t * 2 * dh * 2 \
        + 6 * t * 2 * dh * 4
    return pl.pallas_call(
        functools.partial(_na_body, w=w, rows=rows, kh=kh, dh=dh, eps=cfg.eps,
                          unroll=math.gcd(cfg.na_unroll, rows)),
        out_shape=jax.ShapeDtypeStruct((b * t, h * dh), BF16),
        grid=(b, hp),
        in_specs=[blk(0), blk(hp), blk(2 * hp), gain, gain,
                  pl.BlockSpec((2, kh, w, kh * w), lambda bi, pi: (pi, 0, 0, 0))],
        out_specs=pl.BlockSpec((t, 2 * dh), lambda bi, pi: (bi, pi)),
        scratch_shapes=[pltpu.VMEM((2, t, 2 * dh), BF16), pltpu.VMEM((t, 2 * dh), BF16),
                        pltpu.VMEM((t, 2 * dh), BF16)],
        compiler_params=pltpu.CompilerParams(dimension_semantics=("parallel", "parallel"),
                                             vmem_limit_bytes=_vmem_limit(est)),
        name="neighbourhood_attention",
    )(z, z, z, q_gain.reshape(1, h * dh).astype(F32), k_gain.reshape(1, h * dh).astype(F32), tab)


def _peer_cells(topk):
    return [(k1, k2) for k1 in range(topk) for k2 in range(topk) if (k1 + 1) * (k2 + 1) <= topk]


W_PITCH_PAD = 8
HIGH_HALF = np.uint32(0xFFFF0000)
HALF_ULP_BF16 = np.uint32(0x8000)


class _RouteScratch(NamedTuple):
    s: object
    ts: object
    ti: object
    c_s: object
    c_a: object
    c_b: object
    best: object
    a_t: object
    b_t: object
    g_t: object


def _route_scores(qp_ref, keys_ref, rs):
    dq = keys_ref.shape[2]
    contract1 = (((1,), (1,)), ((), ()))
    for p in range(2):
        rs.s[p] = lax.dot_general(keys_ref[p], qp_ref[:, p * dq:(p + 1) * dq], contract1,
                                  preferred_element_type=F32)


def _route_first_stage(rs, it0, n_it):
    nkeys, th = rs.s.shape[1:]
    iota = lax.broadcasted_iota(I32, (nkeys, th), 0).astype(F32)
    for p in range(2):
        s = rs.s[p]
        for k in range(n_it):
            m = jnp.max(s, axis=0, keepdims=True)
            idx = jnp.min(jnp.where(s == m, iota, float(nkeys)), axis=0, keepdims=True)
            rs.ts[p, pl.ds(it0 + k, 1), :] = m
            rs.ti[p, pl.ds(it0 + k, 1), :] = idx
            s = jnp.where(iota == idx, -jnp.inf, s)
        rs.s[p] = s


def _route_cells(rs, cells):
    crow, th = rs.c_s.shape
    ncell = len(cells)
    rs.c_s[pl.ds(ncell, crow - ncell), :] = jnp.full((crow - ncell, th), -jnp.inf, F32)
    rs.c_a[pl.ds(ncell, crow - ncell), :] = jnp.zeros((crow - ncell, th), F32)
    rs.c_b[pl.ds(ncell, crow - ncell), :] = jnp.zeros((crow - ncell, th), F32)
    for ci, (k1, k2) in enumerate(cells):
        rs.c_s[pl.ds(ci, 1), :] = rs.ts[0, pl.ds(k1, 1), :] + rs.ts[1, pl.ds(k2, 1), :]
        rs.c_a[pl.ds(ci, 1), :] = rs.ti[0, pl.ds(k1, 1), :]
        rs.c_b[pl.ds(ci, 1), :] = rs.ti[1, pl.ds(k2, 1), :]


def _route_second_stage(rs, it0, n_it, blk0, row0):
    crow, th = rs.c_s.shape
    iota = lax.broadcasted_iota(I32, (crow, th), 0).astype(F32)
    cs, ca, cb = rs.c_s[...], rs.c_a[...], rs.c_b[...]
    for k in range(n_it):
        m = jnp.max(cs, axis=0, keepdims=True)
        pos = jnp.min(jnp.where(cs == m, iota, float(crow)), axis=0, keepdims=True)
        hit = iota == pos
        a_row = jnp.sum(jnp.where(hit, ca, 0.0), axis=0, keepdims=True)
        b_row = jnp.sum(jnp.where(hit, cb, 0.0), axis=0, keepdims=True)
        for c in range(th // V7X_LANES):
            cols = slice(c * V7X_LANES, (c + 1) * V7X_LANES)
            rs.a_t[blk0 + c, pl.ds(row0 + it0 + k, 1), :] = a_row[:, cols]
            rs.b_t[blk0 + c, pl.ds(row0 + it0 + k, 1), :] = b_row[:, cols]
        rs.best[pl.ds(it0 + k, 1), :] = m
        cs = jnp.where(hit, -jnp.inf, cs)
    rs.c_s[...] = cs


def _route_gates(rs, blk0, row0):
    topk, th = rs.best.shape
    best = rs.best[...]
    e = jnp.exp(best - best[0:1, :])
    gates = e / jnp.sum(e, axis=0, keepdims=True)
    for c in range(th // V7X_LANES):
        rs.g_t[blk0 + c, pl.ds(row0, topk), :] = gates[:, c * V7X_LANES:(c + 1) * V7X_LANES]


PEER_SUB = 2 * V7X_LANES


def _peer_pipe_body(hn_ref, qp_ref, keys_ref, ut_ref, v_ref, x_ref, o_ref,
                    wmap, acc, pmat, a_cur, b_cur, g_cur, s_scr, ts, ti, c_s, c_a, c_b, best, a_t, b_t, g_t,
                    *, nk, pitch, unroll, heads, topk, cells, nhalf, nsteps):
    r = pl.program_id(0)
    j = pl.program_id(1)
    tm = hn_ref.shape[0]
    th = qp_ref.shape[0]
    nsub, _, sub = ut_ref.shape
    nout = acc.shape[0]
    half = nk // 2
    ncol = th // V7X_LANES
    it_per = topk // nsub
    nj = heads * topk
    rs = _RouteScratch(s_scr, ts, ti, c_s, c_a, c_b, best, a_t, b_t, g_t)
    contract1 = (((1,), (1,)), ((), ()))

    @pl.when(j == 0)
    def _():
        @pl.when(r == 0)
        def _():
            for ref in (acc, pmat, a_cur, b_cur, g_cur, ts, ti):
                ref[...] = jnp.zeros_like(ref)

        @pl.when(r > 0)
        def _():
            blk0 = ((nsteps - 1) % nhalf) * ncol
            row0 = ((nsteps - 1) // nhalf) * topk
            _route_cells(rs, cells)
            _route_second_stage(rs, 0, topk, blk0, row0)
            _route_gates(rs, blk0, row0)
            for cb in range(tm // V7X_LANES):
                rows = slice(cb * V7X_LANES, (cb + 1) * V7X_LANES)
                a_cur[rows, :] = a_t[cb, 0:nj, :].T
                b_cur[rows, :] = b_t[cb, 0:nj, :].T
                g_cur[rows, :] = g_t[cb, 0:nj, :].T

        irow = lax.broadcasted_iota(I32, (nk, nj), 0)
        row = irow.astype(F32)
        a_of_row = jnp.where(irow < half, 2 * irow, 2 * (irow - half) + 1).astype(F32)

        def build(n, carry):
            ar = a_cur[pl.ds(n, 1), :]
            br = b_cur[pl.ds(n, 1), :]
            gr = g_cur[pl.ds(n, 1), :] * 0.5
            one_a = jnp.where(a_of_row == ar, 1.0, 0.0).astype(BF16)
            gate_b = jnp.where(row == br, gr, 0.0).astype(BF16)
            wn = lax.dot_general(one_a, gate_b, contract1, preferred_element_type=F32)
            lo = pltpu.bitcast(wn[:half], jnp.uint32)
            hi = pltpu.bitcast(wn[half:], jnp.uint32)
            packed = ((hi + HALF_ULP_BF16) & HIGH_HALF) | ((lo + HALF_ULP_BF16) >> 16)
            wmap[pl.ds(pl.multiple_of(n * pitch, V7X_SUBLANES), half), :] = packed
            return carry

        lax.fori_loop(0, tm, build, 0, unroll=unroll)

    ju = jnp.maximum(j - 1, 0)
    c_blk0 = lax.rem(ju, nhalf) * ncol
    c_row0 = pl.multiple_of(jnp.where(j == 0, nj, lax.div(ju, nhalf) * topk), topk)
    _route_cells(rs, cells)
    _route_scores(qp_ref, keys_ref, rs)
    cur = lax.rem(j, 2)
    prev = 1 - cur

    def trip(sb, carry):
        ob = lax.rem(sb, nout)
        k0 = lax.div(sb, nout) * nout
        part = acc[ob]
        for k in range(nout):
            rows = pl.ds(pl.multiple_of((k0 + k) * sub, sub), sub)
            part = part + jnp.dot(pmat[prev * nsub + k0 + k], v_ref[ob, rows, :], preferred_element_type=F32)
        acc[ob] = part
        xv = jnp.dot(hn_ref[...], ut_ref[sb], preferred_element_type=F32)
        act = xv * (1.0 + lax.erf(xv * (1.0 / math.sqrt(2.0))))
        parts = []
        for i in range(sub // (2 * nk)):
            w32 = wmap[pl.ds((j * nsub + sb) * (sub // (2 * nk)) + i, tm, stride=pitch), :]
            parts.append(act[:, 2 * i * nk:(2 * i + 1) * nk] * pltpu.bitcast(w32 << 16, F32))
            parts.append(act[:, (2 * i + 1) * nk:(2 * i + 2) * nk] * pltpu.bitcast(w32 & HIGH_HALF, F32))
        pmat[cur * nsub + sb] = jnp.concatenate(parts, axis=1).astype(BF16)
        _route_first_stage(rs, sb * it_per, it_per)
        _route_second_stage(rs, sb * it_per, it_per, c_blk0, c_row0)
        return carry

    lax.fori_loop(0, nsub, trip, 0)
    _route_gates(rs, c_blk0, c_row0)

    @pl.when(j == 0)
    def _():
        for k in range(nout):
            o_ref[:, k * sub:(k + 1) * sub] = x_ref[:, k * sub:(k + 1) * sub] + acc[k]
        acc[...] = jnp.zeros_like(acc)


def _peer_fused(hn, qp, sub_keys, u3, v4, x2, cfg):
    n, d = x2.shape
    nexp = u3.shape[0] * u3.shape[2]
    nk, heads, topk = cfg.peer_nkeys, cfg.peer_heads, cfg.peer_topk
    dq = sub_keys.shape[2]
    nj = heads * topk
    sub = PEER_SUB
    nout = d // sub
    te = max(d, min(cfg.peer_te, nexp) // d * d)
    nsub = te // sub
    nsteps = nexp // te
    assert nk == V7X_LANES and nexp == nk * nk and sub % (2 * nk) == 0 and d % sub == 0 and nexp % te == 0
    assert te % d == 0 and nj == V7X_LANES and qp.shape[1] == heads * 2 * dq and topk % nsub == 0
    assert nsteps % heads == 0 and nsteps % 2 == 0
    nhalf = nsteps // heads
    tm = min(cfg.peer_tm, n)
    tm = max(tm, nhalf * V7X_LANES)
    assert n % tm == 0 and tm % nhalf == 0
    th = tm // nhalf
    assert th % V7X_LANES == 0
    ntiles = n // tm
    pitch = nk // 2 + W_PITCH_PAD
    cells = _peer_cells(topk)
    crow = -(-(len(cells) + 1) // V7X_SUBLANES) * V7X_SUBLANES
    nblk = tm // V7X_LANES
    est = 2 * (tm * d * 2 + th * 2 * dq * 2 + 2 * d * te * 2 + 2 * tm * d * 4) \
        + tm * pitch * nk * 4 + tm * d * 4 + 2 * tm * te * 2 + 3 * tm * nj * 4 \
        + 3 * nblk * (nj + topk) * V7X_LANES * 4 + (3 * crow + 2 * nk + 5 * topk) * th * 4 + 6 * tm * sub * 4
    tile_in = pl.BlockSpec((tm, d), lambda r, j: (jnp.clip(r - 1, 0, ntiles - 1), 0))
    tile_out = pl.BlockSpec((tm, d), lambda r, j: (jnp.clip(jnp.where(j == 0, r - 2, r - 1), 0, ntiles - 1), 0))
    return pl.pallas_call(
        functools.partial(_peer_pipe_body, nk=nk, pitch=pitch, unroll=min(cfg.peer_unroll, tm), heads=heads,
                          topk=topk, cells=tuple(cells), nhalf=nhalf, nsteps=nsteps),
        out_shape=jax.ShapeDtypeStruct((n, d), F32),
        grid=(ntiles + 2, nsteps),
        in_specs=[tile_in,
                  pl.BlockSpec((th, 2 * dq),
                               lambda r, j: (nhalf * jnp.minimum(r, ntiles - 1) + j % nhalf, j // nhalf)),
                  pl.BlockSpec(sub_keys.shape, lambda r, j: (0, 0, 0)),
                  pl.BlockSpec((nsub, d, sub), lambda r, j: (j, 0, 0)),
                  pl.BlockSpec((nout, te, sub), lambda r, j: (0, (j + nsteps - 1) % nsteps, 0)),
                  tile_out],
        out_specs=tile_out,
        scratch_shapes=[pltpu.VMEM((tm * pitch, nk), jnp.uint32),
                        pltpu.VMEM((nout, tm, sub), F32), pltpu.VMEM((2 * nsub, tm, sub), BF16),
                        pltpu.VMEM((tm, nj), F32), pltpu.VMEM((tm, nj), F32), pltpu.VMEM((tm, nj), F32),
                        pltpu.VMEM((2, nk, th), F32), pltpu.VMEM((2, topk, th), F32), pltpu.VMEM((2, topk, th), F32),
                        pltpu.VMEM((crow, th), F32), pltpu.VMEM((crow, th), F32), pltpu.VMEM((crow, th), F32),
                        pltpu.VMEM((topk, th), F32),
                        pltpu.VMEM((nblk, nj + topk, V7X_LANES), F32), pltpu.VMEM((nblk, nj + topk, V7X_LANES), F32),
                        pltpu.VMEM((nblk, nj + topk, V7X_LANES), F32)],
        compiler_params=pltpu.CompilerParams(dimension_semantics=("arbitrary", "arbitrary"),
                                             vmem_limit_bytes=_vmem_limit(est)),
        name="peer_fused",
    )(hn, qp, sub_keys.astype(BF16), u3, v4, x2)


def _peer(x2, gain, w_q_bf16, sub_keys, u3, v4, cfg):
    nq = w_q_bf16.shape[1]
    qp, hn = _norm_matmul(x2, gain, w_q_bf16, [(0, nq)], [BF16], cfg, emit_hn=True)
    return _peer_fused(hn, qp, sub_keys, u3, v4, x2, cfg)


def _trunk(x, p, cfg):
    b, t, d = x.shape
    x2 = x.reshape(b * t, d)
    depth = p["norm_mix"].shape[0]
    for l in range(depth):
        i = l // 2
        if l % 2 == 0:
            rw = 4 * cfg.ret_heads * cfg.ret_dk
            w_in = p["ab_w_in"][i]
            zq, u = _norm_matmul(x2, p["norm_mix"][l], w_in, [(0, rw), (rw, w_in.shape[1])], [F32, F32], cfg)
            ret = _retention(zq, p["ab_ret_decay"][i], p["ab_ret_gn"][i], b, t, cfg)
            ssm = _s5(u, tuple(p[k][i] for k in ("ab_s5_lam_re", "ab_s5_lam_im", "ab_s5_log_dt", "ab_s5_b_re",
                                                  "ab_s5_b_im", "ab_s5_c_re", "ab_s5_c_im", "ab_s5_d",
                                                  "ab_s5_glu_w", "ab_s5_glu_b")), b, t, cfg)
            w_out = p["ab_w_out"][i]
            nr = ret.shape[1]
            x2 = _matmul_residual([ret, ssm], [w_out[:nr], w_out[nr:]], x2, cfg)
        else:
            w_qkv = p["na_w_qkv"][i]
            (z,) = _norm_matmul(x2, p["norm_mix"][l], w_qkv, [(0, w_qkv.shape[1])], [F32], cfg)
            att = _neighbourhood_attention(z, p["na_q_gain"][i], p["na_k_gain"][i], p["na_rpb"][i], b, t, cfg)
            x2 = _matmul_residual([att], [p["na_w_o"][i]], x2, cfg)
        x2 = _peer(x2, p["norm_ffn"][l], p["peer_w_q"][l], p["peer_sub_keys"][l], p["peer_u3"][l],
                   p["peer_v4"][l], cfg)
    return x2.reshape(b, t, d)


def _prepare(params):
    p = dict(params)
    for k in ("ab_w_in", "ab_w_out", "na_w_qkv", "na_w_o", "peer_w_q"):
        p[k] = params[k].astype(BF16)
    u, v = params["peer_u"].astype(BF16), params["peer_v"].astype(BF16)
    layers, nexp, d = u.shape
    p["peer_u3"] = u.reshape(layers, nexp // PEER_SUB, PEER_SUB, d).transpose(0, 1, 3, 2)
    p["peer_v4"] = v.reshape(layers, nexp, d // PEER_SUB, PEER_SUB).transpose(0, 2, 1, 3)
    return p


def _forward(x_prompt, x_sample, params, cfg=Cfg()):
    p = _prepare(params)
    return _trunk(x_prompt, p, cfg), _trunk(x_sample, p, cfg)


def kernel(x_prompt, x_sample, norm_mix, norm_ffn, ab_w_in, ab_ret_decay, ab_ret_gn, ab_s5_lam_re, ab_s5_lam_im, ab_s5_log_dt, ab_s5_b_re, ab_s5_b_im, ab_s5_c_re, ab_s5_c_im, ab_s5_d, ab_s5_glu_w, ab_s5_glu_b, ab_w_out, na_w_qkv, na_q_gain, na_k_gain, na_rpb, na_w_o, peer_w_q, peer_sub_keys, peer_u, peer_v):
    params = dict(norm_mix=norm_mix, norm_ffn=norm_ffn, ab_w_in=ab_w_in, ab_ret_decay=ab_ret_decay,
                  ab_ret_gn=ab_ret_gn, ab_s5_lam_re=ab_s5_lam_re, ab_s5_lam_im=ab_s5_lam_im,
                  ab_s5_log_dt=ab_s5_log_dt, ab_s5_b_re=ab_s5_b_re, ab_s5_b_im=ab_s5_b_im,
                  ab_s5_c_re=ab_s5_c_re, ab_s5_c_im=ab_s5_c_im, ab_s5_d=ab_s5_d, ab_s5_glu_w=ab_s5_glu_w,
                  ab_s5_glu_b=ab_s5_glu_b, ab_w_out=ab_w_out, na_w_qkv=na_w_qkv, na_q_gain=na_q_gain,
                  na_k_gain=na_k_gain, na_rpb=na_rpb, na_w_o=na_w_o, peer_w_q=peer_w_q,
                  peer_sub_keys=peer_sub_keys, peer_u=peer_u, peer_v=peer_v)
    return _forward(x_prompt, x_sample, params)
```

```python
import functools
import math
from typing import NamedTuple

import numpy as np
import jax
import jax.numpy as jnp
from jax import lax
from jax.experimental import pallas as pl
from jax.experimental.pallas import tpu as pltpu

F32 = jnp.float32
BF16 = jnp.bfloat16
I32 = jnp.int32

V7X_LANES = 128
V7X_SUBLANES = 8
V7X_VMEM_BYTES = 64 * 2**20
VMEM_LIMIT_CAP = V7X_VMEM_BYTES - 8 * 2**20

NEG_BIG = -1e30


class Cfg(NamedTuple):
    eps: float = 1e-6
    grid_w: int = 64
    ret_heads: int = 4
    ret_dk: int = 128
    ret_chunk: int = 128
    rope_base: float = 10000.0
    s5_group: int = 16
    s5_state: int = 64
    s5_chunk: int = 8
    s5_rows: int = 512
    na_heads: int = 16
    na_kh: int = 8
    na_kw: int = 16
    peer_heads: int = 8
    peer_nkeys: int = 128
    peer_topk: int = 16
    tm: int = 512
    peer_tm: int = 512
    peer_unroll: int = 32
    na_unroll: int = 8
    ret_group: int = 8
    peer_te: int = 1024


def _vmem_limit(nbytes):
    return int(min(VMEM_LIMIT_CAP, max(32 * 2**20, nbytes)))


def _gelu(x):
    return 0.5 * x * (1.0 + lax.erf(x * (1.0 / math.sqrt(2.0))))


def _sigmoid(x):
    return 1.0 / (1.0 + jnp.exp(-x))


def _norm_mm_body(x_ref, g_ref, w_ref, *out_refs, splits, emit_hn, eps):
    x = x_ref[...]
    y = x * lax.rsqrt(jnp.mean(x * x, axis=-1, keepdims=True) + eps) * g_ref[...]
    yb = y.astype(BF16)
    z = jnp.dot(yb, w_ref[...], preferred_element_type=F32)
    for r, (s, e) in zip(out_refs, splits):
        r[...] = z[:, s:e].astype(r.dtype)
    if emit_hn:
        out_refs[len(splits)][...] = yb


def _norm_matmul(x2, gain, w_bf16, splits, dtypes, cfg, emit_hn=False):
    n, d = x2.shape
    nout = w_bf16.shape[1]
    tm = min(cfg.tm, n)
    out_shape = [jax.ShapeDtypeStruct((n, e - s), dt) for (s, e), dt in zip(splits, dtypes)]
    out_specs = [pl.BlockSpec((tm, e - s), lambda i: (i, 0)) for (s, e) in splits]
    if emit_hn:
        out_shape.append(jax.ShapeDtypeStruct((n, d), BF16))
        out_specs.append(pl.BlockSpec((tm, d), lambda i: (i, 0)))
    est = 2 * (tm * d * 4 + d * nout * 2 + tm * nout * 4 + tm * d * 2) + 2 * tm * nout * 4
    return pl.pallas_call(
        functools.partial(_norm_mm_body, splits=tuple(splits), emit_hn=emit_hn, eps=cfg.eps),
        out_shape=out_shape,
        grid=(n // tm,),
        in_specs=[pl.BlockSpec((tm, d), lambda i: (i, 0)),
                  pl.BlockSpec((1, d), lambda i: (0, 0)),
                  pl.BlockSpec((d, nout), lambda i: (0, 0))],
        out_specs=out_specs,
        compiler_params=pltpu.CompilerParams(dimension_semantics=("parallel",),
                                             vmem_limit_bytes=_vmem_limit(est)),
        name="norm_matmul",
    )(x2, gain.reshape(1, d).astype(F32), w_bf16)


def _mm_res_body(*refs, n_in):
    acc = refs[2 * n_in][...]
    for a, w in zip(refs[:n_in], refs[n_in:2 * n_in]):
        acc = acc + jnp.dot(a[...].astype(BF16), w[...], preferred_element_type=F32)
    refs[-1][...] = acc


def _matmul_residual(a_list, w_list, res, cfg):
    n, d = res.shape
    tm = min(cfg.tm, n)
    n_in = len(a_list)
    in_specs = ([pl.BlockSpec((tm, a.shape[1]), lambda i: (i, 0)) for a in a_list]
                + [pl.BlockSpec(w.shape, lambda i: (0, 0)) for w in w_list]
                + [pl.BlockSpec((tm, d), lambda i: (i, 0))])
    est = 2 * (sum(tm * a.shape[1] * a.dtype.itemsize for a in a_list) + sum(w.size * 2 for w in w_list)
               + 2 * tm * d * 4) + 2 * tm * d * 4
    return pl.pallas_call(
        functools.partial(_mm_res_body, n_in=n_in),
        out_shape=jax.ShapeDtypeStruct((n, d), F32),
        grid=(n // tm,),
        in_specs=in_specs,
        out_specs=pl.BlockSpec((tm, d), lambda i: (i, 0)),
        compiler_params=pltpu.CompilerParams(dimension_semantics=("parallel",),
                                             vmem_limit_bytes=_vmem_limit(est)),
        name="matmul_residual",
    )(*a_list, *w_list, res)


def _ret_tables(ret_decay, chunk, width):
    lg = -jax.nn.softplus(-ret_decay.astype(F32))
    pos = jnp.arange(chunk, dtype=F32)
    diff = pos[:, None] - pos[None, :]
    d_f = jnp.where(diff >= 0, jnp.exp(lg[0][:, None, None] * jnp.maximum(diff, 0.0)[None]), 0.0)
    d_b = jnp.where(diff < 0, jnp.exp(lg[1][:, None, None] * jnp.maximum(-diff, 0.0)[None]), 0.0)
    cols = [jnp.exp(lg[0][:, None] * (chunk - 1.0 - pos)[None]),
            jnp.exp(lg[1][:, None] * pos[None]),
            jnp.exp(lg[0][:, None] * (pos + 1.0)[None]),
            jnp.exp(lg[1][:, None] * (chunk - pos)[None]),
            jnp.broadcast_to(jnp.exp(lg[0] * chunk)[:, None], (lg.shape[1], chunk)),
            jnp.broadcast_to(jnp.exp(lg[1] * chunk)[:, None], (lg.shape[1], chunk))]
    tab = jnp.stack(cols, axis=1)
    return d_f + d_b, jnp.broadcast_to(tab[..., None], tab.shape + (width,))


def _rope_tables(t, half, base):
    inv = base ** (-jnp.arange(half, dtype=F32) / half)
    ang = jnp.arange(t, dtype=F32)[:, None] * inv[None, :]
    cos, sin = jnp.cos(ang), jnp.sin(ang)
    return jnp.concatenate([cos, cos], axis=1), jnp.concatenate([-sin, sin], axis=1)


def _ret_body(q_ref, k_ref, v_ref, g_ref, cos_ref, sin_ref, d_ref, tab_ref, gn_ref, o_ref,
              qs, ks, sf, sb, *, chunk, nc, kscale, eps, group):
    dk = q_ref.shape[1]
    half = dk // 2
    cos, sin = cos_ref[...], sin_ref[...]
    q = q_ref[...]
    qs[...] = q * cos + pltpu.roll(q, half, 1) * sin
    k = k_ref[...]
    ks[...] = (k * cos + pltpu.roll(k, half, 1) * sin) * kscale
    contract0 = (((0,), (0,)), ((), ()))
    contract1 = (((1,), (1,)), ((), ()))

    k_f, k_b, q_f, q_b = tab_ref[0, 0], tab_ref[0, 1], tab_ref[0, 2], tab_ref[0, 3]
    cd_f, cd_b = tab_ref[0, 4], tab_ref[0, 5]

    def increments(n, carry):
        r0 = pl.multiple_of(n * chunk, chunk)
        kc = ks[pl.ds(r0, chunk), :]
        vc = v_ref[pl.ds(r0, chunk), :].astype(BF16)
        sf[n] = lax.dot_general((kc * k_f).astype(BF16), vc, contract0, preferred_element_type=F32)
        sb[n] = lax.dot_general((kc * k_b).astype(BF16), vc, contract0, preferred_element_type=F32)
        return carry

    lax.fori_loop(0, nc, increments, 0, unroll=group)

    def sweep_f(n, state):
        inc = sf[n]
        sf[n] = state
        return state * cd_f + inc

    def sweep_b(i, state):
        n = nc - 1 - i
        inc = sb[n]
        sb[n] = state
        return state * cd_b + inc

    zero = jnp.zeros((dk, v_ref.shape[1]), F32)
    lax.fori_loop(0, nc, sweep_f, zero)
    lax.fori_loop(0, nc, sweep_b, zero)

    def outputs(gi, carry):
        first = []
        for u in range(group):
            n = gi * group + u
            r0 = pl.multiple_of(n * chunk, chunk)
            qc = qs[pl.ds(r0, chunk), :]
            s = lax.dot_general(qc.astype(BF16), ks[pl.ds(r0, chunk), :].astype(BF16), contract1,
                                preferred_element_type=F32)
            cross = jnp.dot((qc * q_f).astype(BF16), sf[n].astype(BF16), preferred_element_type=F32)
            cross = cross + jnp.dot((qc * q_b).astype(BF16), sb[n].astype(BF16), preferred_element_type=F32)
            first.append((r0, s, cross))
        for r0, s, cross in first:
            vc = v_ref[pl.ds(r0, chunk), :].astype(BF16)
            o = jnp.dot((s * d_ref[0]).astype(BF16), vc, preferred_element_type=F32) + cross
            oc = o - jnp.mean(o, axis=-1, keepdims=True)
            o = oc * lax.rsqrt(jnp.mean(oc * oc, axis=-1, keepdims=True) + eps)
            g = g_ref[pl.ds(r0, chunk), :]
            o_ref[pl.ds(r0, chunk), :] = (o * gn_ref[...] * (g * _sigmoid(g))).astype(o_ref.dtype)
        return carry

    lax.fori_loop(0, nc // group, outputs, 0)


def _retention(zq, ret_decay, ret_gn, b, t, cfg):
    h, dk, c = cfg.ret_heads, cfg.ret_dk, cfg.ret_chunk
    assert t % c == 0 and zq.shape[1] == 4 * h * dk
    nc = t // c
    dmat, tab = _ret_tables(ret_decay, c, dk)
    cos2, sin2 = _rope_tables(t, dk // 2, cfg.rope_base)
    blk = lambda off: pl.BlockSpec((t, dk), lambda bi, hi, off=off: (bi, off + hi))
    est = 2 * (4 * t * dk * 4 + 2 * t * dk * 4 + c * c * 4 + 6 * c * dk * 4 + t * dk * 2) \
        + 2 * t * dk * 4 + 2 * nc * dk * dk * 4 + 8 * t * dk * 4
    return pl.pallas_call(
        functools.partial(_ret_body, chunk=c, nc=nc, kscale=dk ** -0.5, eps=cfg.eps,
                          group=math.gcd(cfg.ret_group, nc)),
        out_shape=jax.ShapeDtypeStruct((b * t, h * dk), BF16),
        grid=(b, h),
        in_specs=[blk(0), blk(h), blk(2 * h), blk(3 * h),
                  pl.BlockSpec((t, dk), lambda bi, hi: (0, 0)),
                  pl.BlockSpec((t, dk), lambda bi, hi: (0, 0)),
                  pl.BlockSpec((1, c, c), lambda bi, hi: (hi, 0, 0)),
                  pl.BlockSpec((1, 6, c, dk), lambda bi, hi: (hi, 0, 0, 0)),
                  pl.BlockSpec((1, dk), lambda bi, hi: (0, hi))],
        out_specs=pl.BlockSpec((t, dk), lambda bi, hi: (bi, hi)),
        scratch_shapes=[pltpu.VMEM((t, dk), F32), pltpu.VMEM((t, dk), F32),
                        pltpu.VMEM((nc, dk, dk), F32), pltpu.VMEM((nc, dk, dk), F32)],
        compiler_params=pltpu.CompilerParams(dimension_semantics=("parallel", "parallel"),
                                             vmem_limit_bytes=_vmem_limit(est)),
        name="retention",
    )(zq, zq, zq, zq, cos2, sin2, dmat, tab, ret_gn.reshape(1, h * dk).astype(F32))


def _s5_tables(lam_re, lam_im, log_dt, b_re, b_im, c_re, c_im, d_skip, glu_w, glu_b, L, nsteps):
    lam = lax.complex(lam_re.astype(F32), lam_im.astype(F32))
    ldt = lam * jnp.exp(log_dt.astype(F32))[..., None]
    lam_bar = jnp.exp(ldt)
    bmat = lax.complex(b_re.astype(F32), b_im.astype(F32))
    b_bar = ((lam_bar - 1.0) / lam)[..., None] * bmat[None]
    cmat = lax.complex(c_re.astype(F32), c_im.astype(F32))
    g, p, cg = bmat.shape
    tau = jnp.arange(L + 1, dtype=F32)
    pw = jnp.exp(ldt[:, :, None, :] * tau[None, None, :, None])
    kern = jnp.real(jnp.einsum('dgop,dgtp,dgpi->dgtoi', cmat, pw[:, :, :L], b_bar))
    li = np.arange(L)
    lag = li[None, :] - li[:, None]
    k_f = jnp.where((lag >= 0)[None, :, :, None, None], kern[0][:, np.clip(lag, 0, L - 1)], 0.0)
    k_b = jnp.where((lag <= 0)[None, :, :, None, None], kern[1][:, np.clip(-lag, 0, L - 1)], 0.0)
    m = (k_f + k_b).transpose(0, 1, 4, 2, 3).reshape(g, L * cg, L * cg)

    def cat(z):
        return jnp.concatenate([jnp.real(z), jnp.imag(z)], axis=-1)

    inc_f = cat(jnp.einsum('glp,gpi->glip', pw[0][:, L - 1 - li], b_bar[0])).reshape(g, L * cg, 2 * p)
    inc_b = cat(jnp.einsum('glp,gpi->glip', pw[1][:, li], b_bar[1])).reshape(g, L * cg, 2 * p)

    def out_mat(z):
        return jnp.concatenate([jnp.real(z), -jnp.imag(z)], axis=1).reshape(g, 2 * p, L * cg)

    out_f = out_mat(jnp.einsum('gop,glp->gplo', cmat[0], pw[0][:, li + 1]))
    out_b = out_mat(jnp.einsum('gop,glp->gplo', cmat[1], pw[1][:, L - li]))
    eye = jnp.eye(L, dtype=F32)
    glu = jnp.einsum('lm,gce->glcme', eye, glu_w.astype(F32)).reshape(g, L * cg, L * cg)
    vecs = jnp.stack([jnp.tile(d_skip.astype(F32), (1, L)), jnp.tile(glu_b.astype(F32), (1, L))], axis=1)
    steps = (2.0 ** jnp.arange(nsteps, dtype=F32)) * L
    a = jnp.exp(ldt[:, :, None, :] * steps[None, None, :, None])
    scan = jnp.stack([jnp.concatenate([jnp.real(a), jnp.real(a)], -1),
                      jnp.concatenate([-jnp.imag(a), jnp.imag(a)], -1)], axis=3)
    scan = scan.transpose(1, 0, 2, 3, 4).reshape(g, 2 * nsteps * 2, 2 * p)
    return (m.astype(BF16), inc_f.astype(BF16), inc_b.astype(BF16), out_f.astype(BF16),
            out_b.astype(BF16), glu.astype(BF16), vecs, scan)


def _s5_body(u_ref, m_ref, incf_ref, incb_ref, outf_ref, outb_ref, glu_ref, vec_ref, scan_ref, o_ref,
             *, ncs, nsteps, L):
    rows = u_ref.shape[0] // L
    lanes = u_ref.shape[1]
    u = jnp.concatenate([u_ref[pl.ds(l, rows, stride=L), :] for l in range(L)], axis=1)
    ub = u.astype(BF16)

    def swap(x):
        h = x.shape[1] // 2
        return jnp.concatenate([x[:, h:], x[:, :h]], axis=1)

    y = jnp.dot(ub, m_ref[0], preferred_element_type=F32)
    xf = jnp.dot(ub, incf_ref[0], preferred_element_type=F32)
    xb = jnp.dot(ub, incb_ref[0], preferred_element_type=F32)
    cidx = lax.rem(lax.broadcasted_iota(I32, (rows, 1), 0), ncs)
    for kk in range(nsteps):
        s = 1 << kk
        a_f, b_f = scan_ref[0, pl.ds(2 * kk, 1), :], scan_ref[0, pl.ds(2 * kk + 1, 1), :]
        a_b = scan_ref[0, pl.ds(2 * nsteps + 2 * kk, 1), :]
        b_b = scan_ref[0, pl.ds(2 * nsteps + 2 * kk + 1, 1), :]
        pf = pltpu.roll(xf, s, 0)
        xf = xf + jnp.where(cidx >= s, a_f * pf + b_f * swap(pf), 0.0)
        pb = pltpu.roll(xb, rows - s, 0)
        xb = xb + jnp.where(cidx < ncs - s, a_b * pb + b_b * swap(pb), 0.0)
    x_prev = jnp.where(cidx >= 1, pltpu.roll(xf, 1, 0), 0.0)
    x_next = jnp.where(cidx < ncs - 1, pltpu.roll(xb, rows - 1, 0), 0.0)
    y = y + jnp.dot(x_prev.astype(BF16), outf_ref[0], preferred_element_type=F32)
    y = y + jnp.dot(x_next.astype(BF16), outb_ref[0], preferred_element_type=F32)
    y = y + u * vec_ref[0, pl.ds(0, 1), :]
    yg = _gelu(y)
    z = jnp.dot(yg.astype(BF16), glu_ref[0], preferred_element_type=F32) + vec_ref[0, pl.ds(1, 1), :]
    out = yg * _sigmoid(z)
    for l in range(L):
        o_ref[pl.ds(l, rows, stride=L), :] = out[:, l * lanes:(l + 1) * lanes]


def _s5_slab_tables(tabs, gs, L, cg):
    m, inc_f, inc_b, out_f, out_b, glu, vecs, scan = tabs
    g = m.shape[0]
    s = g // gs
    p2 = inc_f.shape[2]
    eye = jnp.eye(gs, dtype=m.dtype)

    def tok_tok(x):
        x6 = x.reshape(s, gs, L, cg, L, cg)
        return jnp.einsum('sglcmd,gh->slgcmhd', x6, eye).reshape(s, L * gs * cg, L * gs * cg)

    ps = p2 // 2

    def tok_state(x):
        x6 = x.reshape(s, gs, L, cg, 2, ps)
        return jnp.einsum('sglcrp,gh->slgcrhp', x6, eye).reshape(s, L * gs * cg, gs * p2)

    def state_tok(x):
        x6 = x.reshape(s, gs, 2, ps, L, cg)
        return jnp.einsum('sgrplc,gh->srgplhc', x6, eye).reshape(s, gs * p2, L * gs * cg)

    vecs_s = vecs.reshape(s, gs, 2, L, cg).transpose(0, 2, 3, 1, 4).reshape(s, 2, L * gs * cg)
    nrow = scan.shape[1]
    scan_s = scan.reshape(s, gs, nrow, 2, ps).transpose(0, 2, 3, 1, 4).reshape(s, nrow, gs * p2)
    return (tok_tok(m), tok_state(inc_f), tok_state(inc_b), state_tok(out_f), state_tok(out_b), tok_tok(glu),
            vecs_s, scan_s)


def _s5(u, params, b, t, cfg):
    lam_re, lam_im, log_dt, b_re, b_im, c_re, c_im, d_skip, glu_w, glu_b = params
    L, cg = cfg.s5_chunk, cfg.s5_group
    n, ch = u.shape
    gs = V7X_LANES // cg
    assert t % L == 0 and ch % V7X_LANES == 0 and V7X_LANES % cg == 0
    ncs = t // L
    nsteps = max(1, (ncs - 1).bit_length())
    p2 = 2 * cfg.s5_state
    tabs = _s5_slab_tables(
        _s5_tables(lam_re, lam_im, log_dt, b_re, b_im, c_re, c_im, d_skip, glu_w, glu_b, L, nsteps), gs, L, cg)
    seqs = max(1, min(b, cfg.s5_rows // ncs))
    while b % seqs:
        seqs -= 1
    rows = seqs * ncs
    w = L * V7X_LANES
    ws = gs * p2
    per_s = lambda shape: pl.BlockSpec((1,) + shape, lambda si, ri: (si, 0, 0))
    est = 2 * 2 * rows * w * 4 + 2 * 2 * (2 * w * w + 4 * w * ws) + 10 * rows * w * 4 + 8 * rows * ws * 4
    tok = pl.BlockSpec((rows * L, V7X_LANES), lambda si, ri: (ri, si))
    return pl.pallas_call(
        functools.partial(_s5_body, ncs=ncs, nsteps=nsteps, L=L),
        out_shape=jax.ShapeDtypeStruct((n, ch), F32),
        grid=(ch // V7X_LANES, n // (rows * L)),
        in_specs=[tok, per_s((w, w)), per_s((w, ws)), per_s((w, ws)), per_s((ws, w)), per_s((ws, w)),
                  per_s((w, w)), per_s((2, w)), per_s((4 * nsteps, ws))],
        out_specs=tok,
        compiler_params=pltpu.CompilerParams(dimension_semantics=("parallel", "parallel"),
                                             vmem_limit_bytes=_vmem_limit(est)),
        name="s5",
    )(u, *tabs)


def _na_bias_tables(rpb, cfg, rows):
    kh, kw, w = cfg.na_kh, cfg.na_kw, cfg.grid_w
    j = np.arange(w)
    c_start = np.clip(j - kw // 2, 0, w - kw)
    c = np.arange(w)
    inside = (c[None, :] >= c_start[:, None]) & (c[None, :] < c_start[:, None] + kw)
    by_row = jnp.stack([rpb.astype(F32)[:, kh - 1 - dl:2 * kh - 1 - dl] for dl in range(kh)], axis=1)
    padded = jnp.pad(by_row, ((0, 0), (0, 0), (0, 0), (w - kw, w - kw)))
    bias = jnp.stack([padded[..., w - 1 - jq:2 * w - 1 - jq] for jq in range(w)], axis=2)
    bias = jnp.where(inside[None, None, :, None, :], bias, NEG_BIG)
    return bias.reshape(rpb.shape[0], kh, w, kh * w)


def _na_body(q_ref, k_ref, v_ref, qg_ref, kg_ref, tab_ref, o_ref, qn, kn, vb, *, w, rows, kh, dh, eps,
             unroll):
    lo = lax.broadcasted_iota(I32, (1, 2 * dh), 1) < dh
    same_head = ((lax.broadcasted_iota(I32, (2 * dh, 2 * dh), 0) < dh)
                 == (lax.broadcasted_iota(I32, (2 * dh, 2 * dh), 1) < dh))
    head_mean = jnp.where(same_head, 1.0 / dh, 0.0).astype(BF16)

    def head_norm(x, gain):
        x2 = x * x
        hi = x2.astype(BF16)
        rest = (x2 - hi.astype(F32)).astype(BF16)
        ms = (jnp.dot(hi, head_mean, preferred_element_type=F32)
              + jnp.dot(rest, head_mean, preferred_element_type=F32))
        return x * lax.rsqrt(ms + eps) * gain

    qh = head_norm(q_ref[...], qg_ref[...]) * dh ** -0.5
    qn[0] = jnp.where(lo, qh, 0.0).astype(BF16)
    qn[1] = jnp.where(lo, 0.0, qh).astype(BF16)
    kn[...] = head_norm(k_ref[...], kg_ref[...]).astype(BF16)
    vb[...] = v_ref[...].astype(BF16)
    contract1 = (((1,), (1,)), ((), ()))

    def group(gi, carry):
        chains = []
        for u in range(unroll):
            r = gi * unroll + u
            rs = jnp.clip(r - kh // 2, 0, rows - kh)
            q0 = pl.multiple_of(r * w, w)
            k0 = pl.multiple_of(rs * w, w)
            for hh in range(2):
                s = lax.dot_general(qn[hh, pl.ds(q0, w), :], kn[pl.ds(k0, kh * w), :], contract1,
                                    preferred_element_type=F32)
                chains.append((s + tab_ref[hh, r - rs], k0, q0))
        outs = []
        for s, k0, q0 in chains:
            p = jnp.exp(s - jnp.max(s, axis=-1, keepdims=True))
            p = p / jnp.sum(p, axis=-1, keepdims=True)
            outs.append(jnp.dot(p.astype(BF16), vb[pl.ds(k0, kh * w), :], preferred_element_type=F32))
        for u in range(unroll):
            q0 = chains[2 * u][2]
            o_ref[pl.ds(q0, w), :] = jnp.where(lo, outs[2 * u], outs[2 * u + 1]).astype(o_ref.dtype)
        return carry

    lax.fori_loop(0, rows // unroll, group, 0)


def _neighbourhood_attention(z, q_gain, k_gain, rpb, b, t, cfg):
    h, w, kh = cfg.na_heads, cfg.grid_w, cfg.na_kh
    dh = z.shape[1] // (3 * h)
    rows = t // w
    assert t % w == 0 and rows >= kh and h % 2 == 0 and 2 * dh == V7X_LANES
    tab = _na_bias_tables(rpb, cfg, rows)
    hp = h // 2
    blk = lambda off: pl.BlockSpec((t, 2 * dh), lambda bi, pi, off=off: (bi, off + pi))
    gain = pl.BlockSpec((1, 2 * dh), lambda bi, pi: (0, pi))
    est = 2 * (3 * t * 2 * dh * 4 + 2 * kh * w * kh * w * 4 + t * 2 * dh * 2) + 4 * t * 2 * dh * 2 \
        + 6 * t * 2 * dh * 4
    return pl.pallas_call(
        functools.partial(_na_body, w=w, rows=rows, kh=kh, dh=dh, eps=cfg.eps,
                          unroll=math.gcd(cfg.na_unroll, rows)),
        out_shape=jax.ShapeDtypeStruct((b * t, h * dh), BF16),
        grid=(b, hp),
        in_specs=[blk(0), blk(hp), blk(2 * hp), gain, gain,
                  pl.BlockSpec((2, kh, w, kh * w), lambda bi, pi: (pi, 0, 0, 0))],
        out_specs=pl.BlockSpec((t, 2 * dh), lambda bi, pi: (bi, pi)),
        scratch_shapes=[pltpu.VMEM((2, t, 2 * dh), BF16), pltpu.VMEM((t, 2 * dh), BF16),
                        pltpu.VMEM((t, 2 * dh), BF16)],
        compiler_params=pltpu.CompilerParams(dimension_semantics=("parallel", "parallel"),
                                             vmem_limit_bytes=_vmem_limit(est)),
        name="neighbourhood_attention",
    )(z, z, z, q_gain.reshape(1, h * dh).astype(F32), k_gain.reshape(1, h * dh).astype(F32), tab)


def _peer_cells(topk):
    return [(k1, k2) for k1 in range(topk) for k2 in range(topk) if (k1 + 1) * (k2 + 1) <= topk]


W_PITCH_PAD = 8
HIGH_HALF = np.uint32(0xFFFF0000)
HALF_ULP_BF16 = np.uint32(0x8000)


class _RouteScratch(NamedTuple):
    s: object
    ts: object
    ti: object
    c_s: object
    c_a: object
    c_b: object
    best: object
    a_t: object
    b_t: object
    g_t: object


def _route_scores(qp_ref, keys_ref, rs):
    dq = keys_ref.shape[2]
    contract1 = (((1,), (1,)), ((), ()))
    for p in range(2):
        rs.s[p] = lax.dot_general(keys_ref[p], qp_ref[:, p * dq:(p + 1) * dq], contract1,
                                  preferred_element_type=F32)


def _route_first_stage(rs, it0, n_it):
    nkeys, th = rs.s.shape[1:]
    iota = lax.broadcasted_iota(I32, (nkeys, th), 0).astype(F32)
    for p in range(2):
        s = rs.s[p]
        for k in range(n_it):
            m = jnp.max(s, axis=0, keepdims=True)
            idx = jnp.min(jnp.where(s == m, iota, float(nkeys)), axis=0, keepdims=True)
            rs.ts[p, pl.ds(it0 + k, 1), :] = m
            rs.ti[p, pl.ds(it0 + k, 1), :] = idx
            s = jnp.where(iota == idx, -jnp.inf, s)
        rs.s[p] = s


def _route_cells(rs, cells):
    crow, th = rs.c_s.shape
    ncell = len(cells)
    rs.c_s[pl.ds(ncell, crow - ncell), :] = jnp.full((crow - ncell, th), -jnp.inf, F32)
    rs.c_a[pl.ds(ncell, crow - ncell), :] = jnp.zeros((crow - ncell, th), F32)
    rs.c_b[pl.ds(ncell, crow - ncell), :] = jnp.zeros((crow - ncell, th), F32)
    for ci, (k1, k2) in enumerate(cells):
        rs.c_s[pl.ds(ci, 1), :] = rs.ts[0, pl.ds(k1, 1), :] + rs.ts[1, pl.ds(k2, 1), :]
        rs.c_a[pl.ds(ci, 1), :] = rs.ti[0, pl.ds(k1, 1), :]
        rs.c_b[pl.ds(ci, 1), :] = rs.ti[1, pl.ds(k2, 1), :]


def _route_second_stage(rs, it0, n_it, blk0, row0):
    crow, th = rs.c_s.shape
    iota = lax.broadcasted_iota(I32, (crow, th), 0).astype(F32)
    cs, ca, cb = rs.c_s[...], rs.c_a[...], rs.c_b[...]
    for k in range(n_it):
        m = jnp.max(cs, axis=0, keepdims=True)
        pos = jnp.min(jnp.where(cs == m, iota, float(crow)), axis=0, keepdims=True)
        hit = iota == pos
        a_row = jnp.sum(jnp.where(hit, ca, 0.0), axis=0, keepdims=True)
        b_row = jnp.sum(jnp.where(hit, cb, 0.0), axis=0, keepdims=True)
        for c in range(th // V7X_LANES):
            cols = slice(c * V7X_LANES, (c + 1) * V7X_LANES)
            rs.a_t[blk0 + c, pl.ds(row0 + it0 + k, 1), :] = a_row[:, cols]
            rs.b_t[blk0 + c, pl.ds(row0 + it0 + k, 1), :] = b_row[:, cols]
        rs.best[pl.ds(it0 + k, 1), :] = m
        cs = jnp.where(hit, -jnp.inf, cs)
    rs.c_s[...] = cs


def _route_gates(rs, blk0, row0):
    topk, th = rs.best.shape
    best = rs.best[...]
    e = jnp.exp(best - best[0:1, :])
    gates = e / jnp.sum(e, axis=0, keepdims=True)
    for c in range(th // V7X_LANES):
        rs.g_t[blk0 + c, pl.ds(row0, topk), :] = gates[:, c * V7X_LANES:(c + 1) * V7X_LANES]


PEER_SUB = 2 * V7X_LANES


def _peer_pipe_body(hn_ref, qp_ref, keys_ref, ut_ref, v_ref, x_ref, o_ref,
                    wmap, acc, pmat, a_cur, b_cur, g_cur, s_scr, ts, ti, c_s, c_a, c_b, best, a_t, b_t, g_t,
                    *, nk, pitch, unroll, heads, topk, cells, nhalf, nsteps):
    r = pl.program_id(0)
    j = pl.program_id(1)
    tm = hn_ref.shape[0]
    th = qp_ref.shape[0]
    nsub, _, sub = ut_ref.shape
    nout = acc.shape[0]
    half = nk // 2
    ncol = th // V7X_LANES
    it_per = topk // nsub
    nj = heads * topk
    rs = _RouteScratch(s_scr, ts, ti, c_s, c_a, c_b, best, a_t, b_t, g_t)
    contract1 = (((1,), (1,)), ((), ()))

    @pl.when(j == 0)
    def _():
        @pl.when(r == 0)
        def _():
            for ref in (acc, pmat, a_cur, b_cur, g_cur, ts, ti):
                ref[...] = jnp.zeros_like(ref)

        @pl.when(r > 0)
        def _():
            blk0 = ((nsteps - 1) % nhalf) * ncol
            row0 = ((nsteps - 1) // nhalf) * topk
            _route_cells(rs, cells)
            _route_second_stage(rs, 0, topk, blk0, row0)
            _route_gates(rs, blk0, row0)
            for cb in range(tm // V7X_LANES):
                rows = slice(cb * V7X_LANES, (cb + 1) * V7X_LANES)
                a_cur[rows, :] = a_t[cb, 0:nj, :].T
                b_cur[rows, :] = b_t[cb, 0:nj, :].T
                g_cur[rows, :] = g_t[cb, 0:nj, :].T

        irow = lax.broadcasted_iota(I32, (nk, nj), 0)
        row = irow.astype(F32)
        a_of_row = jnp.where(irow < half, 2 * irow, 2 * (irow - half) + 1).astype(F32)

        def build(n, carry):
            ar = a_cur[pl.ds(n, 1), :]
            br = b_cur[pl.ds(n, 1), :]
            gr = g_cur[pl.ds(n, 1), :] * 0.5
            one_a = jnp.where(a_of_row == ar, 1.0, 0.0).astype(BF16)
            gate_b = jnp.where(row == br, gr, 0.0).astype(BF16)
            wn = lax.dot_general(one_a, gate_b, contract1, preferred_element_type=F32)
            lo = pltpu.bitcast(wn[:half], jnp.uint32)
            hi = pltpu.bitcast(wn[half:], jnp.uint32)
            packed = ((hi + HALF_ULP_BF16) & HIGH_HALF) | ((lo + HALF_ULP_BF16) >> 16)
            wmap[pl.ds(pl.multiple_of(n * pitch, V7X_SUBLANES), half), :] = packed
            return carry

        lax.fori_loop(0, tm, build, 0, unroll=unroll)

    ju = jnp.maximum(j - 1, 0)
    c_blk0 = lax.rem(ju, nhalf) * ncol
    c_row0 = pl.multiple_of(jnp.where(j == 0, nj, lax.div(ju, nhalf) * topk), topk)
    _route_cells(rs, cells)
    _route_scores(qp_ref, keys_ref, rs)
    cur = lax.rem(j, 2)
    prev = 1 - cur

    def trip(sb, carry):
        ob = lax.rem(sb, nout)
        k0 = lax.div(sb, nout) * nout
        part = acc[ob]
        for k in range(nout):
            rows = pl.ds(pl.multiple_of((k0 + k) * sub, sub), sub)
            part = part + jnp.dot(pmat[prev * nsub + k0 + k], v_ref[ob, rows, :], preferred_element_type=F32)
        acc[ob] = part
        xv = jnp.dot(hn_ref[...], ut_ref[sb], preferred_element_type=F32)
        act = xv * (1.0 + lax.erf(xv * (1.0 / math.sqrt(2.0))))
        parts = []
        for i in range(sub // (2 * nk)):
            w32 = wmap[pl.ds((j * nsub + sb) * (sub // (2 * nk)) + i, tm, stride=pitch), :]
            parts.append(act[:, 2 * i * nk:(2 * i + 1) * nk] * pltpu.bitcast(w32 << 16, F32))
            parts.append(act[:, (2 * i + 1) * nk:(2 * i + 2) * nk] * pltpu.bitcast(w32 & HIGH_HALF, F32))
        pmat[cur * nsub + sb] = jnp.concatenate(parts, axis=1).astype(BF16)
        _route_first_stage(rs, sb * it_per, it_per)
        _route_second_stage(rs, sb * it_per, it_per, c_blk0, c_row0)
        return carry

    lax.fori_loop(0, nsub, trip, 0)
    _route_gates(rs, c_blk0, c_row0)

    @pl.when(j == 0)
    def _():
        for k in range(nout):
            o_ref[:, k * sub:(k + 1) * sub] = x_ref[:, k * sub:(k + 1) * sub] + acc[k]
        acc[...] = jnp.zeros_like(acc)


def _peer_fused(hn, qp, sub_keys, u3, v4, x2, cfg):
    n, d = x2.shape
    nexp = u3.shape[0] * u3.shape[2]
    nk, heads, topk = cfg.peer_nkeys, cfg.peer_heads, cfg.peer_topk
    dq = sub_keys.shape[2]
    nj = heads * topk
    sub = PEER_SUB
    nout = d // sub
    te = max(d, min(cfg.peer_te, nexp) // d * d)
    nsub = te // sub
    nsteps = nexp // te
    assert nk == V7X_LANES and nexp == nk * nk and sub % (2 * nk) == 0 and d % sub == 0 and nexp % te == 0
    assert te % d == 0 and nj == V7X_LANES and qp.shape[1] == heads * 2 * dq and topk % nsub == 0
    assert nsteps % heads == 0 and nsteps % 2 == 0
    nhalf = nsteps // heads
    tm = min(cfg.peer_tm, n)
    tm = max(tm, nhalf * V7X_LANES)
    assert n % tm == 0 and tm % nhalf == 0
    th = tm // nhalf
    assert th % V7X_LANES == 0
    ntiles = n // tm
    pitch = nk // 2 + W_PITCH_PAD
    cells = _peer_cells(topk)
    crow = -(-(len(cells) + 1) // V7X_SUBLANES) * V7X_SUBLANES
    nblk = tm // V7X_LANES
    est = 2 * (tm * d * 2 + th * 2 * dq * 2 + 2 * d * te * 2 + 2 * tm * d * 4) \
        + tm * pitch * nk * 4 + tm * d * 4 + 2 * tm * te * 2 + 3 * tm * nj * 4 \
        + 3 * nblk * (nj + topk) * V7X_LANES * 4 + (3 * crow + 2 * nk + 5 * topk) * th * 4 + 6 * tm * sub * 4
    tile_in = pl.BlockSpec((tm, d), lambda r, j: (jnp.clip(r - 1, 0, ntiles - 1), 0))
    tile_out = pl.BlockSpec((tm, d), lambda r, j: (jnp.clip(jnp.where(j == 0, r - 2, r - 1), 0, ntiles - 1), 0))
    return pl.pallas_call(
        functools.partial(_peer_pipe_body, nk=nk, pitch=pitch, unroll=min(cfg.peer_unroll, tm), heads=heads,
                          topk=topk, cells=tuple(cells), nhalf=nhalf, nsteps=nsteps),
        out_shape=jax.ShapeDtypeStruct((n, d), F32),
        grid=(ntiles + 2, nsteps),
        in_specs=[tile_in,
                  pl.BlockSpec((th, 2 * dq),
                               lambda r, j: (nhalf * jnp.minimum(r, ntiles - 1) + j % nhalf, j // nhalf)),
                  pl.BlockSpec(sub_keys.shape, lambda r, j: (0, 0, 0)),
                  pl.BlockSpec((nsub, d, sub), lambda r, j: (j, 0, 0)),
                  pl.BlockSpec((nout, te, sub), lambda r, j: (0, (j + nsteps - 1) % nsteps, 0)),
                  tile_out],
        out_specs=tile_out,
        scratch_shapes=[pltpu.VMEM((tm * pitch, nk), jnp.uint32),
                        pltpu.VMEM((nout, tm, sub), F32), pltpu.VMEM((2 * nsub, tm, sub), BF16),
                        pltpu.VMEM((tm, nj), F32), pltpu.VMEM((tm, nj), F32), pltpu.VMEM((tm, nj), F32),
                        pltpu.VMEM((2, nk, th), F32), pltpu.VMEM((2, topk, th), F32), pltpu.VMEM((2, topk, th), F32),
                        pltpu.VMEM((crow, th), F32), pltpu.VMEM((crow, th), F32), pltpu.VMEM((crow, th), F32),
                        pltpu.VMEM((topk, th), F32),
                        pltpu.VMEM((nblk, nj + topk, V7X_LANES), F32), pltpu.VMEM((nblk, nj + topk, V7X_LANES), F32),
                        pltpu.VMEM((nblk, nj + topk, V7X_LANES), F32)],
        compiler_params=pltpu.CompilerParams(dimension_semantics=("arbitrary", "arbitrary"),
                                             vmem_limit_bytes=_vmem_limit(est)),
        name="peer_fused",
    )(hn, qp, sub_keys.astype(BF16), u3, v4, x2)


def _peer(x2, gain, w_q_bf16, sub_keys, u3, v4, cfg):
    nq = w_q_bf16.shape[1]
    qp, hn = _norm_matmul(x2, gain, w_q_bf16, [(0, nq)], [BF16], cfg, emit_hn=True)
    return _peer_fused(hn, qp, sub_keys, u3, v4, x2, cfg)


def _trunk(x, p, cfg):
    b, t, d = x.shape
    x2 = x.reshape(b * t, d)
    depth = p["norm_mix"].shape[0]
    for l in range(depth):
        i = l // 2
        if l % 2 == 0:
            rw = 4 * cfg.ret_heads * cfg.ret_dk
            w_in = p["ab_w_in"][i]
            zq, u = _norm_matmul(x2, p["norm_mix"][l], w_in, [(0, rw), (rw, w_in.shape[1])], [F32, F32], cfg)
            ret = _retention(zq, p["ab_ret_decay"][i], p["ab_ret_gn"][i], b, t, cfg)
            ssm = _s5(u, tuple(p[k][i] for k in ("ab_s5_lam_re", "ab_s5_lam_im", "ab_s5_log_dt", "ab_s5_b_re",
                                                  "ab_s5_b_im", "ab_s5_c_re", "ab_s5_c_im", "ab_s5_d",
                                                  "ab_s5_glu_w", "ab_s5_glu_b")), b, t, cfg)
            w_out = p["ab_w_out"][i]
            nr = ret.shape[1]
            x2 = _matmul_residual([ret, ssm], [w_out[:nr], w_out[nr:]], x2, cfg)
        else:
            w_qkv = p["na_w_qkv"][i]
            (z,) = _norm_matmul(x2, p["norm_mix"][l], w_qkv, [(0, w_qkv.shape[1])], [F32], cfg)
            att = _neighbourhood_attention(z, p["na_q_gain"][i], p["na_k_gain"][i], p["na_rpb"][i], b, t, cfg)
            x2 = _matmul_residual([att], [p["na_w_o"][i]], x2, cfg)
        x2 = _peer(x2, p["norm_ffn"][l], p["peer_w_q"][l], p["peer_sub_keys"][l], p["peer_u3"][l],
                   p["peer_v4"][l], cfg)
    return x2.reshape(b, t, d)


def _prepare(params):
    p = dict(params)
    for k in ("ab_w_in", "ab_w_out", "na_w_qkv", "na_w_o", "peer_w_q"):
        p[k] = params[k].astype(BF16)
    u, v = params["peer_u"].astype(BF16), params["peer_v"].astype(BF16)
    layers, nexp, d = u.shape
    p["peer_u3"] = u.reshape(layers, nexp // PEER_SUB, PEER_SUB, d).transpose(0, 1, 3, 2)
    p["peer_v4"] = v.reshape(layers, nexp, d // PEER_SUB, PEER_SUB).transpose(0, 2, 1, 3)
    return p


def _forward(x_prompt, x_sample, params, cfg=Cfg()):
    p = _prepare(params)
    return _trunk(x_prompt, p, cfg), _trunk(x_sample, p, cfg)


def kernel(x_prompt, x_sample, norm_mix, norm_ffn, ab_w_in, ab_ret_decay, ab_ret_gn, ab_s5_lam_re, ab_s5_lam_im, ab_s5_log_dt, ab_s5_b_re, ab_s5_b_im, ab_s5_c_re, ab_s5_c_im, ab_s5_d, ab_s5_glu_w, ab_s5_glu_b, ab_w_out, na_w_qkv, na_q_gain, na_k_gain, na_rpb, na_w_o, peer_w_q, peer_sub_keys, peer_u, peer_v):
    params = dict(norm_mix=norm_mix, norm_ffn=norm_ffn, ab_w_in=ab_w_in, ab_ret_decay=ab_ret_decay,
                  ab_ret_gn=ab_ret_gn, ab_s5_lam_re=ab_s5_lam_re, ab_s5_lam_im=ab_s5_lam_im,
                  ab_s5_log_dt=ab_s5_log_dt, ab_s5_b_re=ab_s5_b_re, ab_s5_b_im=ab_s5_b_im,
                  ab_s5_c_re=ab_s5_c_re, ab_s5_c_im=ab_s5_c_im, ab_s5_d=ab_s5_d, ab_s5_glu_w=ab_s5_glu_w,
                  ab_s5_glu_b=ab_s5_glu_b, ab_w_out=ab_w_out, na_w_qkv=na_w_qkv, na_q_gain=na_q_gain,
                  na_k_gain=na_k_gain, na_rpb=na_rpb, na_w_o=na_w_o, peer_w_q=peer_w_q,
                  peer_sub_keys=peer_sub_keys, peer_u=peer_u, peer_v=peer_v)
    return _forward(x_prompt, x_sample, params)
```

```python
import functools
import math
from typing import NamedTuple

import numpy as np
import jax
import jax.numpy as jnp
from jax import lax
from jax.experimental import pallas as pl
from jax.experimental.pallas import tpu as pltpu

F32 = jnp.float32
BF16 = jnp.bfloat16
I32 = jnp.int32

V7X_LANES = 128
V7X_SUBLANES = 8
V7X_VMEM_BYTES = 64 * 2**20
VMEM_LIMIT_CAP = V7X_VMEM_BYTES - 8 * 2**20

NEG_BIG = -1e30


class Cfg(NamedTuple):
    eps: float = 1e-6
    grid_w: int = 64
    ret_heads: int = 4
    ret_dk: int = 128
    ret_chunk: int = 128
    rope_base: float = 10000.0
    s5_group: int = 16
    s5_state: int = 64
    s5_chunk: int = 8
    s5_rows: int = 512
    na_heads: int = 16
    na_kh: int = 8
    na_kw: int = 16
    peer_heads: int = 8
    peer_nkeys: int = 128
    peer_topk: int = 16
    tm: int = 512
    peer_tm: int = 512
    peer_unroll: int = 32
    na_unroll: int = 8
    ret_group: int = 8
    peer_te: int = 1024


def _vmem_limit(nbytes):
    return int(min(VMEM_LIMIT_CAP, max(32 * 2**20, nbytes)))


def _gelu(x):
    return 0.5 * x * (1.0 + lax.erf(x * (1.0 / math.sqrt(2.0))))


def _sigmoid(x):
    return 1.0 / (1.0 + jnp.exp(-x))


def _norm_mm_body(x_ref, g_ref, w_ref, *out_refs, splits, emit_hn, eps):
    x = x_ref[...]
    y = x * lax.rsqrt(jnp.mean(x * x, axis=-1, keepdims=True) + eps) * g_ref[...]
    yb = y.astype(BF16)
    z = jnp.dot(yb, w_ref[...], preferred_element_type=F32)
    for r, (s, e) in zip(out_refs, splits):
        r[...] = z[:, s:e].astype(r.dtype)
    if emit_hn:
        out_refs[len(splits)][...] = yb


def _norm_matmul(x2, gain, w_bf16, splits, dtypes, cfg, emit_hn=False):
    n, d = x2.shape
    nout = w_bf16.shape[1]
    tm = min(cfg.tm, n)
    out_shape = [jax.ShapeDtypeStruct((n, e - s), dt) for (s, e), dt in zip(splits, dtypes)]
    out_specs = [pl.BlockSpec((tm, e - s), lambda i: (i, 0)) for (s, e) in splits]
    if emit_hn:
        out_shape.append(jax.ShapeDtypeStruct((n, d), BF16))
        out_specs.append(pl.BlockSpec((tm, d), lambda i: (i, 0)))
    est = 2 * (tm * d * 4 + d * nout * 2 + tm * nout * 4 + tm * d * 2) + 2 * tm * nout * 4
    return pl.pallas_call(
        functools.partial(_norm_mm_body, splits=tuple(splits), emit_hn=emit_hn, eps=cfg.eps),
        out_shape=out_shape,
        grid=(n // tm,),
        in_specs=[pl.BlockSpec((tm, d), lambda i: (i, 0)),
                  pl.BlockSpec((1, d), lambda i: (0, 0)),
                  pl.BlockSpec((d, nout), lambda i: (0, 0))],
        out_specs=out_specs,
        compiler_params=pltpu.CompilerParams(dimension_semantics=("parallel",),
                                             vmem_limit_bytes=_vmem_limit(est)),
        name="norm_matmul",
    )(x2, gain.reshape(1, d).astype(F32), w_bf16)


def _mm_res_body(*refs, n_in):
    acc = refs[2 * n_in][...]
    for a, w in zip(refs[:n_in], refs[n_in:2 * n_in]):
        acc = acc + jnp.dot(a[...].astype(BF16), w[...], preferred_element_type=F32)
    refs[-1][...] = acc


def _matmul_residual(a_list, w_list, res, cfg):
    n, d = res.shape
    tm = min(cfg.tm, n)
    n_in = len(a_list)
    in_specs = ([pl.BlockSpec((tm, a.shape[1]), lambda i: (i, 0)) for a in a_list]
                + [pl.BlockSpec(w.shape, lambda i: (0, 0)) for w in w_list]
                + [pl.BlockSpec((tm, d), lambda i: (i, 0))])
    est = 2 * (sum(tm * a.shape[1] * a.dtype.itemsize for a in a_list) + sum(w.size * 2 for w in w_list)
               + 2 * tm * d * 4) + 2 * tm * d * 4
    return pl.pallas_call(
        functools.partial(_mm_res_body, n_in=n_in),
        out_shape=jax.ShapeDtypeStruct((n, d), F32),
        grid=(n // tm,),
        in_specs=in_specs,
        out_specs=pl.BlockSpec((tm, d), lambda i: (i, 0)),
        compiler_params=pltpu.CompilerParams(dimension_semantics=("parallel",),
                                             vmem_limit_bytes=_vmem_limit(est)),
        name="matmul_residual",
    )(*a_list, *w_list, res)


def _ret_tables(ret_decay, chunk, width):
    lg = -jax.nn.softplus(-ret_decay.astype(F32))
    pos = jnp.arange(chunk, dtype=F32)
    diff = pos[:, None] - pos[None, :]
    d_f = jnp.where(diff >= 0, jnp.exp(lg[0][:, None, None] * jnp.maximum(diff, 0.0)[None]), 0.0)
    d_b = jnp.where(diff < 0, jnp.exp(lg[1][:, None, None] * jnp.maximum(-diff, 0.0)[None]), 0.0)
    cols = [jnp.exp(lg[0][:, None] * (chunk - 1.0 - pos)[None]),
            jnp.exp(lg[1][:, None] * pos[None]),
            jnp.exp(lg[0][:, None] * (pos + 1.0)[None]),
            jnp.exp(lg[1][:, None] * (chunk - pos)[None]),
            jnp.broadcast_to(jnp.exp(lg[0] * chunk)[:, None], (lg.shape[1], chunk)),
            jnp.broadcast_to(jnp.exp(lg[1] * chunk)[:, None], (lg.shape[1], chunk))]
    tab = jnp.stack(cols, axis=1)
    return d_f + d_b, jnp.broadcast_to(tab[..., None], tab.shape + (width,))


def _rope_tables(t, half, base):
    inv = base ** (-jnp.arange(half, dtype=F32) / half)
    ang = jnp.arange(t, dtype=F32)[:, None] * inv[None, :]
    cos, sin = jnp.cos(ang), jnp.sin(ang)
    return jnp.concatenate([cos, cos], axis=1), jnp.concatenate([-sin, sin], axis=1)


def _ret_body(q_ref, k_ref, v_ref, g_ref, cos_ref, sin_ref, d_ref, tab_ref, gn_ref, o_ref,
              qs, ks, sf, sb, *, chunk, nc, kscale, eps, group):
    dk = q_ref.shape[1]
    half = dk // 2
    cos, sin = cos_ref[...], sin_ref[...]
    q = q_ref[...]
    qs[...] = q * cos + pltpu.roll(q, half, 1) * sin
    k = k_ref[...]
    ks[...] = (k * cos + pltpu.roll(k, half, 1) * sin) * kscale
    contract0 = (((0,), (0,)), ((), ()))
    contract1 = (((1,), (1,)), ((), ()))

    k_f, k_b, q_f, q_b = tab_ref[0, 0], tab_ref[0, 1], tab_ref[0, 2], tab_ref[0, 3]
    cd_f, cd_b = tab_ref[0, 4], tab_ref[0, 5]

    def increments(n, carry):
        r0 = pl.multiple_of(n * chunk, chunk)
        kc = ks[pl.ds(r0, chunk), :]
        vc = v_ref[pl.ds(r0, chunk), :].astype(BF16)
        sf[n] = lax.dot_general((kc * k_f).astype(BF16), vc, contract0, preferred_element_type=F32)
        sb[n] = lax.dot_general((kc * k_b).astype(BF16), vc, contract0, preferred_element_type=F32)
        return carry

    lax.fori_loop(0, nc, increments, 0, unroll=group)

    def sweep_f(n, state):
        inc = sf[n]
        sf[n] = state
        return state * cd_f + inc

    def sweep_b(i, state):
        n = nc - 1 - i
        inc = sb[n]
        sb[n] = state
        return state * cd_b + inc

    zero = jnp.zeros((dk, v_ref.shape[1]), F32)
    lax.fori_loop(0, nc, sweep_f, zero)
    lax.fori_loop(0, nc, sweep_b, zero)

    def outputs(gi, carry):
        first = []
        for u in range(group):
            n = gi * group + u
            r0 = pl.multiple_of(n * chunk, chunk)
            qc = qs[pl.ds(r0, chunk), :]
            s = lax.dot_general(qc.astype(BF16), ks[pl.ds(r0, chunk), :].astype(BF16), contract1,
                                preferred_element_type=F32)
            cross = jnp.dot((qc * q_f).astype(BF16), sf[n].astype(BF16), preferred_element_type=F32)
            cross = cross + jnp.dot((qc * q_b).astype(BF16), sb[n].astype(BF16), preferred_element_type=F32)
            first.append((r0, s, cross))
        for r0, s, cross in first:
            vc = v_ref[pl.ds(r0, chunk), :].astype(BF16)
            o = jnp.dot((s * d_ref[0]).astype(BF16), vc, preferred_element_type=F32) + cross
            oc = o - jnp.mean(o, axis=-1, keepdims=True)
            o = oc * lax.rsqrt(jnp.mean(oc * oc, axis=-1, keepdims=True) + eps)
            g = g_ref[pl.ds(r0, chunk), :]
            o_ref[pl.ds(r0, chunk), :] = (o * gn_ref[...] * (g * _sigmoid(g))).astype(o_ref.dtype)
        return carry

    lax.fori_loop(0, nc // group, outputs, 0)


def _retention(zq, ret_decay, ret_gn, b, t, cfg):
    h, dk, c = cfg.ret_heads, cfg.ret_dk, cfg.ret_chunk
    assert t % c == 0 and zq.shape[1] == 4 * h * dk
    nc = t // c
    dmat, tab = _ret_tables(ret_decay, c, dk)
    cos2, sin2 = _rope_tables(t, dk // 2, cfg.rope_base)
    blk = lambda off: pl.BlockSpec((t, dk), lambda bi, hi, off=off: (bi, off + hi))
    est = 2 * (4 * t * dk * 4 + 2 * t * dk * 4 + c * c * 4 + 6 * c * dk * 4 + t * dk * 2) \
        + 2 * t * dk * 4 + 2 * nc * dk * dk * 4 + 8 * t * dk * 4
    return pl.pallas_call(
        functools.partial(_ret_body, chunk=c, nc=nc, kscale=dk ** -0.5, eps=cfg.eps,
                          group=math.gcd(cfg.ret_group, nc)),
        out_shape=jax.ShapeDtypeStruct((b * t, h * dk), BF16),
        grid=(b, h),
        in_specs=[blk(0), blk(h), blk(2 * h), blk(3 * h),
                  pl.BlockSpec((t, dk), lambda bi, hi: (0, 0)),
                  pl.BlockSpec((t, dk), lambda bi, hi: (0, 0)),
                  pl.BlockSpec((1, c, c), lambda bi, hi: (hi, 0, 0)),
                  pl.BlockSpec((1, 6, c, dk), lambda bi, hi: (hi, 0, 0, 0)),
                  pl.BlockSpec((1, dk), lambda bi, hi: (0, hi))],
        out_specs=pl.BlockSpec((t, dk), lambda bi, hi: (bi, hi)),
        scratch_shapes=[pltpu.VMEM((t, dk), F32), pltpu.VMEM((t, dk), F32),
                        pltpu.VMEM((nc, dk, dk), F32), pltpu.VMEM((nc, dk, dk), F32)],
        compiler_params=pltpu.CompilerParams(dimension_semantics=("parallel", "parallel"),
                                             vmem_limit_bytes=_vmem_limit(est)),
        name="retention",
    )(zq, zq, zq, zq, cos2, sin2, dmat, tab, ret_gn.reshape(1, h * dk).astype(F32))


def _s5_tables(lam_re, lam_im, log_dt, b_re, b_im, c_re, c_im, d_skip, glu_w, glu_b, L, nsteps):
    lam = lax.complex(lam_re.astype(F32), lam_im.astype(F32))
    ldt = lam * jnp.exp(log_dt.astype(F32))[..., None]
    lam_bar = jnp.exp(ldt)
    bmat = lax.complex(b_re.astype(F32), b_im.astype(F32))
    b_bar = ((lam_bar - 1.0) / lam)[..., None] * bmat[None]
    cmat = lax.complex(c_re.astype(F32), c_im.astype(F32))
    g, p, cg = bmat.shape
    tau = jnp.arange(L + 1, dtype=F32)
    pw = jnp.exp(ldt[:, :, None, :] * tau[None, None, :, None])
    kern = jnp.real(jnp.einsum('dgop,dgtp,dgpi->dgtoi', cmat, pw[:, :, :L], b_bar))
    li = np.arange(L)
    lag = li[None, :] - li[:, None]
    k_f = jnp.where((lag >= 0)[None, :, :, None, None], kern[0][:, np.clip(lag, 0, L - 1)], 0.0)
    k_b = jnp.where((lag <= 0)[None, :, :, None, None], kern[1][:, np.clip(-lag, 0, L - 1)], 0.0)
    m = (k_f + k_b).transpose(0, 1, 4, 2, 3).reshape(g, L * cg, L * cg)

    def cat(z):
        return jnp.concatenate([jnp.real(z), jnp.imag(z)], axis=-1)

    inc_f = cat(jnp.einsum('glp,gpi->glip', pw[0][:, L - 1 - li], b_bar[0])).reshape(g, L * cg, 2 * p)
    inc_b = cat(jnp.einsum('glp,gpi->glip', pw[1][:, li], b_bar[1])).reshape(g, L * cg, 2 * p)

    def out_mat(z):
        return jnp.concatenate([jnp.real(z), -jnp.imag(z)], axis=1).reshape(g, 2 * p, L * cg)

    out_f = out_mat(jnp.einsum('gop,glp->gplo', cmat[0], pw[0][:, li + 1]))
    out_b = out_mat(jnp.einsum('gop,glp->gplo', cmat[1], pw[1][:, L - li]))
    eye = jnp.eye(L, dtype=F32)
    glu = jnp.einsum('lm,gce->glcme', eye, glu_w.astype(F32)).reshape(g, L * cg, L * cg)
    vecs = jnp.stack([jnp.tile(d_skip.astype(F32), (1, L)), jnp.tile(glu_b.astype(F32), (1, L))], axis=1)
    steps = (2.0 ** jnp.arange(nsteps, dtype=F32)) * L
    a = jnp.exp(ldt[:, :, None, :] * steps[None, None, :, None])
    scan = jnp.stack([jnp.concatenate([jnp.real(a), jnp.real(a)], -1),
                      jnp.concatenate([-jnp.imag(a), jnp.imag(a)], -1)], axis=3)
    scan = scan.transpose(1, 0, 2, 3, 4).reshape(g, 2 * nsteps * 2, 2 * p)
    return (m.astype(BF16), inc_f.astype(BF16), inc_b.astype(BF16), out_f.astype(BF16),
            out_b.astype(BF16), glu.astype(BF16), vecs, scan)


def _s5_body(u_ref, m_ref, incf_ref, incb_ref, outf_ref, outb_ref, glu_ref, vec_ref, scan_ref, o_ref,
             *, ncs, nsteps, L):
    rows = u_ref.shape[0] // L
    lanes = u_ref.shape[1]
    u = jnp.concatenate([u_ref[pl.ds(l, rows, stride=L), :] for l in range(L)], axis=1)
    ub = u.astype(BF16)

    def swap(x):
        h = x.shape[1] // 2
        return jnp.concatenate([x[:, h:], x[:, :h]], axis=1)

    y = jnp.dot(ub, m_ref[0], preferred_element_type=F32)
    xf = jnp.dot(ub, incf_ref[0], preferred_element_type=F32)
    xb = jnp.dot(ub, incb_ref[0], preferred_element_type=F32)
    cidx = lax.rem(lax.broadcasted_iota(I32, (rows, 1), 0), ncs)
    for kk in range(nsteps):
        s = 1 << kk
        a_f, b_f = scan_ref[0, pl.ds(2 * kk, 1), :], scan_ref[0, pl.ds(2 * kk + 1, 1), :]
        a_b = scan_ref[0, pl.ds(2 * nsteps + 2 * kk, 1), :]
        b_b = scan_ref[0, pl.ds(2 * nsteps + 2 * kk + 1, 1), :]
        pf = pltpu.roll(xf, s, 0)
        xf = xf + jnp.where(cidx >= s, a_f * pf + b_f * swap(pf), 0.0)
        pb = pltpu.roll(xb, rows - s, 0)
        xb = xb + jnp.where(cidx < ncs - s, a_b * pb + b_b * swap(pb), 0.0)
    x_prev = jnp.where(cidx >= 1, pltpu.roll(xf, 1, 0), 0.0)
    x_next = jnp.where(cidx < ncs - 1, pltpu.roll(xb, rows - 1, 0), 0.0)
    y = y + jnp.dot(x_prev.astype(BF16), outf_ref[0], preferred_element_type=F32)
    y = y + jnp.dot(x_next.astype(BF16), outb_ref[0], preferred_element_type=F32)
    y = y + u * vec_ref[0, pl.ds(0, 1), :]
    yg = _gelu(y)
    z = jnp.dot(yg.astype(BF16), glu_ref[0], preferred_element_type=F32) + vec_ref[0, pl.ds(1, 1), :]
    out = yg * _sigmoid(z)
    for l in range(L):
        o_ref[pl.ds(l, rows, stride=L), :] = out[:, l * lanes:(l + 1) * lanes]


def _s5_slab_tables(tabs, gs, L, cg):
    m, inc_f, inc_b, out_f, out_b, glu, vecs, scan = tabs
    g = m.shape[0]
    s = g // gs
    p2 = inc_f.shape[2]
    eye = jnp.eye(gs, dtype=m.dtype)

    def tok_tok(x):
        x6 = x.reshape(s, gs, L, cg, L, cg)
        return jnp.einsum('sglcmd,gh->slgcmhd', x6, eye).reshape(s, L * gs * cg, L * gs * cg)

    ps = p2 // 2

    def tok_state(x):
        x6 = x.reshape(s, gs, L, cg, 2, ps)
        return jnp.einsum('sglcrp,gh->slgcrhp', x6, eye).reshape(s, L * gs * cg, gs * p2)

    def state_tok(x):
        x6 = x.reshape(s, gs, 2, ps, L, cg)
        return jnp.einsum('sgrplc,gh->srgplhc', x6, eye).reshape(s, gs * p2, L * gs * cg)

    vecs_s = vecs.reshape(s, gs, 2, L, cg).transpose(0, 2, 3, 1, 4).reshape(s, 2, L * gs * cg)
    nrow = scan.shape[1]
    scan_s = scan.reshape(s, gs, nrow, 2, ps).transpose(0, 2, 3, 1, 4).reshape(s, nrow, gs * p2)
    return (tok_tok(m), tok_state(inc_f), tok_state(inc_b), state_tok(out_f), state_tok(out_b), tok_tok(glu),
            vecs_s, scan_s)


def _s5(u, params, b, t, cfg):
    lam_re, lam_im, log_dt, b_re, b_im, c_re, c_im, d_skip, glu_w, glu_b = params
    L, cg = cfg.s5_chunk, cfg.s5_group
    n, ch = u.shape
    gs = V7X_LANES // cg
    assert t % L == 0 and ch % V7X_LANES == 0 and V7X_LANES % cg == 0
    ncs = t // L
    nsteps = max(1, (ncs - 1).bit_length())
    p2 = 2 * cfg.s5_state
    tabs = _s5_slab_tables(
        _s5_tables(lam_re, lam_im, log_dt, b_re, b_im, c_re, c_im, d_skip, glu_w, glu_b, L, nsteps), gs, L, cg)
    seqs = max(1, min(b, cfg.s5_rows // ncs))
    while b % seqs:
        seqs -= 1
    rows = seqs * ncs
    w = L * V7X_LANES
    ws = gs * p2
    per_s = lambda shape: pl.BlockSpec((1,) + shape, lambda si, ri: (si, 0, 0))
    est = 2 * 2 * rows * w * 4 + 2 * 2 * (2 * w * w + 4 * w * ws) + 10 * rows * w * 4 + 8 * rows * ws * 4
    tok = pl.BlockSpec((rows * L, V7X_LANES), lambda si, ri: (ri, si))
    return pl.pallas_call(
        functools.partial(_s5_body, ncs=ncs, nsteps=nsteps, L=L),
        out_shape=jax.ShapeDtypeStruct((n, ch), F32),
        grid=(ch // V7X_LANES, n // (rows * L)),
        in_specs=[tok, per_s((w, w)), per_s((w, ws)), per_s((w, ws)), per_s((ws, w)), per_s((ws, w)),
                  per_s((w, w)), per_s((2, w)), per_s((4 * nsteps, ws))],
        out_specs=tok,
        compiler_params=pltpu.CompilerParams(dimension_semantics=("parallel", "parallel"),
                                             vmem_limit_bytes=_vmem_limit(est)),
        name="s5",
    )(u, *tabs)


def _na_bias_tables(rpb, cfg, rows):
    kh, kw, w = cfg.na_kh, cfg.na_kw, cfg.grid_w
    j = np.arange(w)
    c_start = np.clip(j - kw // 2, 0, w - kw)
    c = np.arange(w)
    inside = (c[None, :] >= c_start[:, None]) & (c[None, :] < c_start[:, None] + kw)
    by_row = jnp.stack([rpb.astype(F32)[:, kh - 1 - dl:2 * kh - 1 - dl] for dl in range(kh)], axis=1)
    padded = jnp.pad(by_row, ((0, 0), (0, 0), (0, 0), (w - kw, w - kw)))
    bias = jnp.stack([padded[..., w - 1 - jq:2 * w - 1 - jq] for jq in range(w)], axis=2)
    bias = jnp.where(inside[None, None, :, None, :], bias, NEG_BIG)
    return bias.reshape(rpb.shape[0], kh, w, kh * w)


def _na_body(q_ref, kn, vb, tab_ref, o_ref, qn, *, w, rows, kh, dh, unroll):
    lo = lax.broadcasted_iota(I32, (1, 2 * dh), 1) < dh
    qh = q_ref[...]
    qn[0] = jnp.where(lo, qh, jnp.zeros_like(qh))
    qn[1] = jnp.where(lo, jnp.zeros_like(qh), qh)
    contract1 = (((1,), (1,)), ((), ()))

    def group(gi, carry):
        chains = []
        for u in range(unroll):
            r = gi * unroll + u
            rs = jnp.clip(r - kh // 2, 0, rows - kh)
            q0 = pl.multiple_of(r * w, w)
            k0 = pl.multiple_of(rs * w, w)
            for hh in range(2):
                s = lax.dot_general(qn[hh, pl.ds(q0, w), :], kn[pl.ds(k0, kh * w), :], contract1,
                                    preferred_element_type=F32)
                chains.append((s + tab_ref[hh, r - rs], k0, q0))
        outs = []
        for s, k0, q0 in chains:
            p = jnp.exp(s - jnp.max(s, axis=-1, keepdims=True))
            p = p / jnp.sum(p, axis=-1, keepdims=True)
            outs.append(jnp.dot(p.astype(BF16), vb[pl.ds(k0, kh * w), :], preferred_element_type=F32))
        for u in range(unroll):
            q0 = chains[2 * u][2]
            o_ref[pl.ds(q0, w), :] = jnp.where(lo, outs[2 * u], outs[2 * u + 1]).astype(o_ref.dtype)
        return carry

    lax.fori_loop(0, rows // unroll, group, 0)


def _qkv_body(x_ref, g_ref, w_ref, qg_ref, kg_ref, o_ref, *, eps, dh, width):
    x = x_ref[...]
    y = x * lax.rsqrt(jnp.mean(x * x, axis=-1, keepdims=True) + eps) * g_ref[...]
    z = jnp.dot(y.astype(BF16), w_ref[...], preferred_element_type=F32)
    lanes = V7X_LANES
    same_head = ((lax.broadcasted_iota(I32, (lanes, lanes), 0) < dh)
                 == (lax.broadcasted_iota(I32, (lanes, lanes), 1) < dh))
    head_mean = jnp.where(same_head, 1.0 / dh, 0.0).astype(BF16)

    def head_norm(c0, gain_ref, scale):
        for c in range(0, width, lanes):
            xc = z[:, c0 + c:c0 + c + lanes]
            x2 = xc * xc
            hi = x2.astype(BF16)
            rest = (x2 - hi.astype(F32)).astype(BF16)
            ms = (jnp.dot(hi, head_mean, preferred_element_type=F32)
                  + jnp.dot(rest, head_mean, preferred_element_type=F32))
            out = xc * lax.rsqrt(ms + eps) * gain_ref[:, c:c + lanes]
            o_ref[:, c0 + c:c0 + c + lanes] = (out * scale if scale != 1.0 else out).astype(o_ref.dtype)

    head_norm(0, qg_ref, dh ** -0.5)
    head_norm(width, kg_ref, 1.0)
    o_ref[:, 2 * width:] = z[:, 2 * width:].astype(o_ref.dtype)


def _qkv_projection(x2, gain, w_bf16, q_gain, k_gain, cfg):
    n, d = x2.shape
    nout = w_bf16.shape[1]
    width = nout // 3
    dh = width // cfg.na_heads
    tm = min(cfg.tm, n)
    assert 2 * dh == V7X_LANES and width % V7X_LANES == 0
    est = 2 * (tm * d * 4 + d * nout * 2 + tm * nout * 2) + 3 * tm * nout * 4
    row = lambda a: a.reshape(1, width).astype(F32)
    return pl.pallas_call(
        functools.partial(_qkv_body, eps=cfg.eps, dh=dh, width=width),
        out_shape=jax.ShapeDtypeStruct((n, nout), BF16),
        grid=(n // tm,),
        in_specs=[pl.BlockSpec((tm, d), lambda i: (i, 0)),
                  pl.BlockSpec((1, d), lambda i: (0, 0)),
                  pl.BlockSpec((d, nout), lambda i: (0, 0)),
                  pl.BlockSpec((1, width), lambda i: (0, 0)),
                  pl.BlockSpec((1, width), lambda i: (0, 0))],
        out_specs=pl.BlockSpec((tm, nout), lambda i: (i, 0)),
        compiler_params=pltpu.CompilerParams(dimension_semantics=("parallel",),
                                             vmem_limit_bytes=_vmem_limit(est)),
        name="qkv_projection",
    )(x2, gain.reshape(1, d).astype(F32), w_bf16, row(q_gain), row(k_gain))


def _neighbourhood_attention(z, rpb, b, t, cfg):
    h, w, kh = cfg.na_heads, cfg.grid_w, cfg.na_kh
    dh = z.shape[1] // (3 * h)
    rows = t // w
    assert t % w == 0 and rows >= kh and h % 2 == 0 and 2 * dh == V7X_LANES
    tab = _na_bias_tables(rpb, cfg, rows)
    hp = h // 2
    blk = lambda off: pl.BlockSpec((t, 2 * dh), lambda bi, pi, off=off: (bi, off + pi))
    est = 2 * (3 * t * 2 * dh * 2 + 2 * kh * w * kh * w * 4 + t * 2 * dh * 2) + 2 * t * 2 * dh * 2 \
        + 6 * t * 2 * dh * 4
    return pl.pallas_call(
        functools.partial(_na_body, w=w, rows=rows, kh=kh, dh=dh, unroll=math.gcd(cfg.na_unroll, rows)),
        out_shape=jax.ShapeDtypeStruct((b * t, h * dh), BF16),
        grid=(b, hp),
        in_specs=[blk(0), blk(hp), blk(2 * hp),
                  pl.BlockSpec((2, kh, w, kh * w), lambda bi, pi: (pi, 0, 0, 0))],
        out_specs=pl.BlockSpec((t, 2 * dh), lambda bi, pi: (bi, pi)),
        scratch_shapes=[pltpu.VMEM((2, t, 2 * dh), BF16)],
        compiler_params=pltpu.CompilerParams(dimension_semantics=("parallel", "parallel"),
                                             vmem_limit_bytes=_vmem_limit(est)),
        name="neighbourhood_attention",
    )(z, z, z, tab)


def _peer_cells(topk):
    return [(k1, k2) for k1 in range(topk) for k2 in range(topk) if (k1 + 1) * (k2 + 1) <= topk]


W_PITCH_PAD = 8
HIGH_HALF = np.uint32(0xFFFF0000)
HALF_ULP_BF16 = np.uint32(0x8000)


class _RouteScratch(NamedTuple):
    s: object
    ts: object
    ti: object
    c_s: object
    c_a: object
    c_b: object
    best: object
    a_t: object
    b_t: object
    g_t: object


def _route_scores(qp_ref, keys_ref, rs):
    dq = keys_ref.shape[2]
    contract1 = (((1,), (1,)), ((), ()))
    for p in range(2):
        rs.s[p] = lax.dot_general(keys_ref[p], qp_ref[:, p * dq:(p + 1) * dq], contract1,
                                  preferred_element_type=F32)


def _route_first_stage(rs, it0, n_it):
    nkeys, th = rs.s.shape[1:]
    iota = lax.broadcasted_iota(I32, (nkeys, th), 0).astype(F32)
    for p in range(2):
        s = rs.s[p]
        for k in range(n_it):
            m = jnp.max(s, axis=0, keepdims=True)
            idx = jnp.min(jnp.where(s == m, iota, float(nkeys)), axis=0, keepdims=True)
            rs.ts[p, pl.ds(it0 + k, 1), :] = m
            rs.ti[p, pl.ds(it0 + k, 1), :] = idx
            s = jnp.where(iota == idx, -jnp.inf, s)
        rs.s[p] = s


def _route_cells(rs, cells):
    crow, th = rs.c_s.shape
    ncell = len(cells)
    rs.c_s[pl.ds(ncell, crow - ncell), :] = jnp.full((crow - ncell, th), -jnp.inf, F32)
    rs.c_a[pl.ds(ncell, crow - ncell), :] = jnp.zeros((crow - ncell, th), F32)
    rs.c_b[pl.ds(ncell, crow - ncell), :] = jnp.zeros((crow - ncell, th), F32)
    for ci, (k1, k2) in enumerate(cells):
        rs.c_s[pl.ds(ci, 1), :] = rs.ts[0, pl.ds(k1, 1), :] + rs.ts[1, pl.ds(k2, 1), :]
        rs.c_a[pl.ds(ci, 1), :] = rs.ti[0, pl.ds(k1, 1), :]
        rs.c_b[pl.ds(ci, 1), :] = rs.ti[1, pl.ds(k2, 1), :]


def _route_second_stage(rs, it0, n_it, blk0, row0):
    crow, th = rs.c_s.shape
    iota = lax.broadcasted_iota(I32, (crow, th), 0).astype(F32)
    cs, ca, cb = rs.c_s[...], rs.c_a[...], rs.c_b[...]
    for k in range(n_it):
        m = jnp.max(cs, axis=0, keepdims=True)
        pos = jnp.min(jnp.where(cs == m, iota, float(crow)), axis=0, keepdims=True)
        hit = iota == pos
        a_row = jnp.sum(jnp.where(hit, ca, 0.0), axis=0, keepdims=True)
        b_row = jnp.sum(jnp.where(hit, cb, 0.0), axis=0, keepdims=True)
        for c in range(th // V7X_LANES):
            cols = slice(c * V7X_LANES, (c + 1) * V7X_LANES)
            rs.a_t[blk0 + c, pl.ds(row0 + it0 + k, 1), :] = a_row[:, cols]
            rs.b_t[blk0 + c, pl.ds(row0 + it0 + k, 1), :] = b_row[:, cols]
        rs.best[pl.ds(it0 + k, 1), :] = m
        cs = jnp.where(hit, -jnp.inf, cs)
    rs.c_s[...] = cs


def _route_gates(rs, blk0, row0):
    topk, th = rs.best.shape
    best = rs.best[...]
    e = jnp.exp(best - best[0:1, :])
    gates = e / jnp.sum(e, axis=0, keepdims=True)
    for c in range(th // V7X_LANES):
        rs.g_t[blk0 + c, pl.ds(row0, topk), :] = gates[:, c * V7X_LANES:(c + 1) * V7X_LANES]


PEER_SUB = 2 * V7X_LANES


def _peer_pipe_body(hn_ref, qp_ref, keys_ref, ut_ref, v_ref, x_ref, o_ref,
                    wmap, acc, pmat, a_cur, b_cur, g_cur, s_scr, ts, ti, c_s, c_a, c_b, best, a_t, b_t, g_t,
                    *, nk, pitch, unroll, heads, topk, cells, nhalf, nsteps):
    r = pl.program_id(0)
    j = pl.program_id(1)
    tm = hn_ref.shape[0]
    th = qp_ref.shape[0]
    nsub, _, sub = ut_ref.shape
    nout = acc.shape[0]
    half = nk // 2
    ncol = th // V7X_LANES
    it_per = topk // nsub
    nj = heads * topk
    rs = _RouteScratch(s_scr, ts, ti, c_s, c_a, c_b, best, a_t, b_t, g_t)
    contract1 = (((1,), (1,)), ((), ()))

    @pl.when(j == 0)
    def _():
        @pl.when(r == 0)
        def _():
            for ref in (acc, pmat, a_cur, b_cur, g_cur, ts, ti):
                ref[...] = jnp.zeros_like(ref)

        @pl.when(r > 0)
        def _():
            blk0 = ((nsteps - 1) % nhalf) * ncol
            row0 = ((nsteps - 1) // nhalf) * topk
            _route_cells(rs, cells)
            _route_second_stage(rs, 0, topk, blk0, row0)
            _route_gates(rs, blk0, row0)
            for cb in range(tm // V7X_LANES):
                rows = slice(cb * V7X_LANES, (cb + 1) * V7X_LANES)
                a_cur[rows, :] = a_t[cb, 0:nj, :].T
                b_cur[rows, :] = b_t[cb, 0:nj, :].T
                g_cur[rows, :] = g_t[cb, 0:nj, :].T

        irow = lax.broadcasted_iota(I32, (nk, nj), 0)
        row = irow.astype(F32)
        a_of_row = jnp.where(irow < half, 2 * irow, 2 * (irow - half) + 1).astype(F32)

        def build(n, carry):
            ar = a_cur[pl.ds(n, 1), :]
            br = b_cur[pl.ds(n, 1), :]
            gr = g_cur[pl.ds(n, 1), :] * 0.5
            one_a = jnp.where(a_of_row == ar, 1.0, 0.0).astype(BF16)
            gate_b = jnp.where(row == br, gr, 0.0).astype(BF16)
            wn = lax.dot_general(one_a, gate_b, contract1, preferred_element_type=F32)
            lo = pltpu.bitcast(wn[:half], jnp.uint32)
            hi = pltpu.bitcast(wn[half:], jnp.uint32)
            packed = ((hi + HALF_ULP_BF16) & HIGH_HALF) | ((lo + HALF_ULP_BF16) >> 16)
            wmap[pl.ds(pl.multiple_of(n * pitch, V7X_SUBLANES), half), :] = packed
            return carry

        lax.fori_loop(0, tm, build, 0, unroll=unroll)

    ju = jnp.maximum(j - 1, 0)
    c_blk0 = lax.rem(ju, nhalf) * ncol
    c_row0 = pl.multiple_of(jnp.where(j == 0, nj, lax.div(ju, nhalf) * topk), topk)
    _route_scores(qp_ref, keys_ref, rs)
    _route_cells(rs, cells)
    cur = lax.rem(j, 2)
    prev = 1 - cur

    def trip(sb, carry):
        ob = lax.rem(sb, nout)
        k0 = lax.div(sb, nout) * nout
        part = acc[ob]
        for k in range(nout):
            rows = pl.ds(pl.multiple_of((k0 + k) * sub, sub), sub)
            part = part + jnp.dot(pmat[prev * nsub + k0 + k], v_ref[ob, rows, :], preferred_element_type=F32)
        acc[ob] = part
        xv = jnp.dot(hn_ref[...], ut_ref[sb], preferred_element_type=F32)
        act = xv * (1.0 + lax.erf(xv * (1.0 / math.sqrt(2.0))))
        parts = []
        for i in range(sub // (2 * nk)):
            w32 = wmap[pl.ds((j * nsub + sb) * (sub // (2 * nk)) + i, tm, stride=pitch), :]
            parts.append(act[:, 2 * i * nk:(2 * i + 1) * nk] * pltpu.bitcast(w32 << 16, F32))
            parts.append(act[:, (2 * i + 1) * nk:(2 * i + 2) * nk] * pltpu.bitcast(w32 & HIGH_HALF, F32))
        pmat[cur * nsub + sb] = jnp.concatenate(parts, axis=1).astype(BF16)
        _route_first_stage(rs, sb * it_per, it_per)
        _route_second_stage(rs, sb * it_per, it_per, c_blk0, c_row0)
        return carry

    lax.fori_loop(0, nsub, trip, 0)
    _route_gates(rs, c_blk0, c_row0)

    @pl.when(j == 0)
    def _():
        for k in range(nout):
            o_ref[:, k * sub:(k + 1) * sub] = x_ref[:, k * sub:(k + 1) * sub] + acc[k]
        acc[...] = jnp.zeros_like(acc)


def _peer_fused(hn, qp, sub_keys, u3, v4, x2, cfg):
    n, d = x2.shape
    nexp = u3.shape[0] * u3.shape[2]
    nk, heads, topk = cfg.peer_nkeys, cfg.peer_heads, cfg.peer_topk
    dq = sub_keys.shape[2]
    nj = heads * topk
    sub = PEER_SUB
    nout = d // sub
    te = max(d, min(cfg.peer_te, nexp) // d * d)
    nsub = te // sub
    nsteps = nexp // te
    assert nk == V7X_LANES and nexp == nk * nk and sub % (2 * nk) == 0 and d % sub == 0 and nexp % te == 0
    assert te % d == 0 and nj == V7X_LANES and qp.shape[1] == heads * 2 * dq and topk % nsub == 0
    assert nsteps % heads == 0 and nsteps % 2 == 0
    nhalf = nsteps // heads
    tm = min(cfg.peer_tm, n)
    tm = max(tm, nhalf * V7X_LANES)
    assert n % tm == 0 and tm % nhalf == 0
    th = tm // nhalf
    assert th % V7X_LANES == 0
    ntiles = n // tm
    pitch = nk // 2 + W_PITCH_PAD
    cells = _peer_cells(topk)
    crow = -(-(len(cells) + 1) // V7X_SUBLANES) * V7X_SUBLANES
    nblk = tm // V7X_LANES
    est = 2 * (tm * d * 2 + th * 2 * dq * 2 + 2 * d * te * 2 + 2 * tm * d * 4) \
        + tm * pitch * nk * 4 + tm * d * 4 + 2 * tm * te * 2 + 3 * tm * nj * 4 \
        + 3 * nblk * (nj + topk) * V7X_LANES * 4 + (3 * crow + 2 * nk + 5 * topk) * th * 4 + 6 * tm * sub * 4
    tile_in = pl.BlockSpec((tm, d), lambda r, j: (jnp.clip(r - 1, 0, ntiles - 1), 0))
    tile_out = pl.BlockSpec((tm, d), lambda r, j: (jnp.clip(jnp.where(j == 0, r - 2, r - 1), 0, ntiles - 1), 0))
    return pl.pallas_call(
        functools.partial(_peer_pipe_body, nk=nk, pitch=pitch, unroll=min(cfg.peer_unroll, tm), heads=heads,
                          topk=topk, cells=tuple(cells), nhalf=nhalf, nsteps=nsteps),
        out_shape=jax.ShapeDtypeStruct((n, d), F32),
        grid=(ntiles + 2, nsteps),
        in_specs=[tile_in,
                  pl.BlockSpec((th, 2 * dq),
                               lambda r, j: (nhalf * jnp.minimum(r, ntiles - 1) + j % nhalf, j // nhalf)),
                  pl.BlockSpec(sub_keys.shape, lambda r, j: (0, 0, 0)),
                  pl.BlockSpec((nsub, d, sub), lambda r, j: (j, 0, 0)),
                  pl.BlockSpec((nout, te, sub), lambda r, j: (0, (j + nsteps - 1) % nsteps, 0)),
                  tile_out],
        out_specs=tile_out,
        scratch_shapes=[pltpu.VMEM((tm * pitch, nk), jnp.uint32),
                        pltpu.VMEM((nout, tm, sub), F32), pltpu.VMEM((2 * nsub, tm, sub), BF16),
                        pltpu.VMEM((tm, nj), F32), pltpu.VMEM((tm, nj), F32), pltpu.VMEM((tm, nj), F32),
                        pltpu.VMEM((2, nk, th), F32), pltpu.VMEM((2, topk, th), F32), pltpu.VMEM((2, topk, th), F32),
                        pltpu.VMEM((crow, th), F32), pltpu.VMEM((crow, th), F32), pltpu.VMEM((crow, th), F32),
                        pltpu.VMEM((topk, th), F32),
                        pltpu.VMEM((nblk, nj + topk, V7X_LANES), F32), pltpu.VMEM((nblk, nj + topk, V7X_LANES), F32),
                        pltpu.VMEM((nblk, nj + topk, V7X_LANES), F32)],
        compiler_params=pltpu.CompilerParams(dimension_semantics=("arbitrary", "arbitrary"),
                                             vmem_limit_bytes=_vmem_limit(est)),
        name="peer_fused",
    )(hn, qp, sub_keys.astype(BF16), u3, v4, x2)


def _peer(x2, gain, w_q_bf16, sub_keys, u3, v4, cfg):
    nq = w_q_bf16.shape[1]
    qp, hn = _norm_matmul(x2, gain, w_q_bf16, [(0, nq)], [BF16], cfg, emit_hn=True)
    return _peer_fused(hn, qp, sub_keys, u3, v4, x2, cfg)


def _trunk(x, p, cfg):
    b, t, d = x.shape
    x2 = x.reshape(b * t, d)
    depth = p["norm_mix"].shape[0]
    for l in range(depth):
        i = l // 2
        if l % 2 == 0:
            rw = 4 * cfg.ret_heads * cfg.ret_dk
            w_in = p["ab_w_in"][i]
            zq, u = _norm_matmul(x2, p["norm_mix"][l], w_in, [(0, rw), (rw, w_in.shape[1])], [F32, F32], cfg)
            ret = _retention(zq, p["ab_ret_decay"][i], p["ab_ret_gn"][i], b, t, cfg)
            ssm = _s5(u, tuple(p[k][i] for k in ("ab_s5_lam_re", "ab_s5_lam_im", "ab_s5_log_dt", "ab_s5_b_re",
                                                  "ab_s5_b_im", "ab_s5_c_re", "ab_s5_c_im", "ab_s5_d",
                                                  "ab_s5_glu_w", "ab_s5_glu_b")), b, t, cfg)
            w_out = p["ab_w_out"][i]
            nr = ret.shape[1]
            x2 = _matmul_residual([ret, ssm], [w_out[:nr], w_out[nr:]], x2, cfg)
        else:
            z = _qkv_projection(x2, p["norm_mix"][l], p["na_w_qkv"][i], p["na_q_gain"][i], p["na_k_gain"][i], cfg)
            att = _neighbourhood_attention(z, p["na_rpb"][i], b, t, cfg)
            x2 = _matmul_residual([att], [p["na_w_o"][i]], x2, cfg)
        x2 = _peer(x2, p["norm_ffn"][l], p["peer_w_q"][l], p["peer_sub_keys"][l], p["peer_u3"][l],
                   p["peer_v4"][l], cfg)
    return x2.reshape(b, t, d)


def _prepare(params):
    p = dict(params)
    for k in ("ab_w_in", "ab_w_out", "na_w_qkv", "na_w_o", "peer_w_q"):
        p[k] = params[k].astype(BF16)
    u, v = params["peer_u"].astype(BF16), params["peer_v"].astype(BF16)
    layers, nexp, d = u.shape
    p["peer_u3"] = u.reshape(layers, nexp // PEER_SUB, PEER_SUB, d).transpose(0, 1, 3, 2)
    p["peer_v4"] = v.reshape(layers, nexp, d // PEER_SUB, PEER_SUB).transpose(0, 2, 1, 3)
    return p


def _forward(x_prompt, x_sample, params, cfg=Cfg()):
    p = _prepare(params)
    return _trunk(x_prompt, p, cfg), _trunk(x_sample, p, cfg)


def kernel(x_prompt, x_sample, norm_mix, norm_ffn, ab_w_in, ab_ret_decay, ab_ret_gn, ab_s5_lam_re, ab_s5_lam_im, ab_s5_log_dt, ab_s5_b_re, ab_s5_b_im, ab_s5_c_re, ab_s5_c_im, ab_s5_d, ab_s5_glu_w, ab_s5_glu_b, ab_w_out, na_w_qkv, na_q_gain, na_k_gain, na_rpb, na_w_o, peer_w_q, peer_sub_keys, peer_u, peer_v):
    params = dict(norm_mix=norm_mix, norm_ffn=norm_ffn, ab_w_in=ab_w_in, ab_ret_decay=ab_ret_decay,
                  ab_ret_gn=ab_ret_gn, ab_s5_lam_re=ab_s5_lam_re, ab_s5_lam_im=ab_s5_lam_im,
                  ab_s5_log_dt=ab_s5_log_dt, ab_s5_b_re=ab_s5_b_re, ab_s5_b_im=ab_s5_b_im,
                  ab_s5_c_re=ab_s5_c_re, ab_s5_c_im=ab_s5_c_im, ab_s5_d=ab_s5_d, ab_s5_glu_w=ab_s5_glu_w,
                  ab_s5_glu_b=ab_s5_glu_b, ab_w_out=ab_w_out, na_w_qkv=na_w_qkv, na_q_gain=na_q_gain,
                  na_k_gain=na_k_gain, na_rpb=na_rpb, na_w_o=na_w_o, peer_w_q=peer_w_q,
                  peer_sub_keys=peer_sub_keys, peer_u=peer_u, peer_v=peer_v)
    return _forward(x_prompt, x_sample, params)
```

```python
import functools
import math
from typing import NamedTuple

import numpy as np
import jax
import jax.numpy as jnp
from jax import lax
from jax.experimental import pallas as pl
from jax.experimental.pallas import tpu as pltpu

F32 = jnp.float32
BF16 = jnp.bfloat16
I32 = jnp.int32

V7X_LANES = 128
V7X_SUBLANES = 8
V7X_MXU_WIDTH = 256
V7X_VMEM_BYTES = 64 * 2**20
VMEM_LIMIT_CAP = V7X_VMEM_BYTES - 8 * 2**20

NEG_BIG = -1e30


class Cfg(NamedTuple):
    eps: float = 1e-6
    grid_w: int = 64
    ret_heads: int = 4
    ret_dk: int = 128
    ret_chunk: int = 128
    rope_base: float = 10000.0
    s5_group: int = 16
    s5_state: int = 64
    s5_chunk: int = 8
    s5_rows: int = 512
    na_heads: int = 16
    na_kh: int = 8
    na_kw: int = 16
    peer_heads: int = 8
    peer_nkeys: int = 128
    peer_topk: int = 16
    tm: int = 512
    peer_tm: int = 512
    peer_unroll: int = 64
    na_unroll: int = 8
    ret_group: int = 8
    peer_te: int = 1024


def _vmem_limit(nbytes):
    return int(min(VMEM_LIMIT_CAP, max(32 * 2**20, nbytes)))


def _gelu(x):
    return 0.5 * x * (1.0 + lax.erf(x * (1.0 / math.sqrt(2.0))))


def _sigmoid(x):
    return 1.0 / (1.0 + jnp.exp(-x))


def _norm_mm_body(x_ref, g_ref, w_ref, *out_refs, splits, emit_hn, eps):
    x = x_ref[...]
    y = x * lax.rsqrt(jnp.mean(x * x, axis=-1, keepdims=True) + eps) * g_ref[...]
    yb = y.astype(BF16)
    z = jnp.dot(yb, w_ref[...], preferred_element_type=F32)
    for r, (s, e) in zip(out_refs, splits):
        r[...] = z[:, s:e].astype(r.dtype)
    if emit_hn:
        out_refs[len(splits)][...] = yb


def _norm_matmul(x2, gain, w_bf16, splits, dtypes, cfg, emit_hn=False):
    n, d = x2.shape
    nout = w_bf16.shape[1]
    tm = min(cfg.tm, n)
    out_shape = [jax.ShapeDtypeStruct((n, e - s), dt) for (s, e), dt in zip(splits, dtypes)]
    out_specs = [pl.BlockSpec((tm, e - s), lambda i: (i, 0)) for (s, e) in splits]
    if emit_hn:
        out_shape.append(jax.ShapeDtypeStruct((n, d), BF16))
        out_specs.append(pl.BlockSpec((tm, d), lambda i: (i, 0)))
    est = 2 * (tm * d * 4 + d * nout * 2 + tm * nout * 4 + tm * d * 2) + 2 * tm * nout * 4
    return pl.pallas_call(
        functools.partial(_norm_mm_body, splits=tuple(splits), emit_hn=emit_hn, eps=cfg.eps),
        out_shape=out_shape,
        grid=(n // tm,),
        in_specs=[pl.BlockSpec((tm, d), lambda i: (i, 0)),
                  pl.BlockSpec((1, d), lambda i: (0, 0)),
                  pl.BlockSpec((d, nout), lambda i: (0, 0))],
        out_specs=out_specs,
        compiler_params=pltpu.CompilerParams(dimension_semantics=("parallel",),
                                             vmem_limit_bytes=_vmem_limit(est)),
        name="norm_matmul",
    )(x2, gain.reshape(1, d).astype(F32), w_bf16)


def _mm_res_body(*refs, n_in):
    acc = refs[2 * n_in][...]
    for a, w in zip(refs[:n_in], refs[n_in:2 * n_in]):
        acc = acc + jnp.dot(a[...].astype(BF16), w[...], preferred_element_type=F32)
    refs[-1][...] = acc


def _matmul_residual(a_list, w_list, res, cfg):
    n, d = res.shape
    tm = min(cfg.tm, n)
    n_in = len(a_list)
    in_specs = ([pl.BlockSpec((tm, a.shape[1]), lambda i: (i, 0)) for a in a_list]
                + [pl.BlockSpec(w.shape, lambda i: (0, 0)) for w in w_list]
                + [pl.BlockSpec((tm, d), lambda i: (i, 0))])
    est = 2 * (sum(tm * a.shape[1] * a.dtype.itemsize for a in a_list) + sum(w.size * 2 for w in w_list)
               + 2 * tm * d * 4) + 2 * tm * d * 4
    return pl.pallas_call(
        functools.partial(_mm_res_body, n_in=n_in),
        out_shape=jax.ShapeDtypeStruct((n, d), F32),
        grid=(n // tm,),
        in_specs=in_specs,
        out_specs=pl.BlockSpec((tm, d), lambda i: (i, 0)),
        compiler_params=pltpu.CompilerParams(dimension_semantics=("parallel",),
                                             vmem_limit_bytes=_vmem_limit(est)),
        name="matmul_residual",
    )(*a_list, *w_list, res)


def _ret_tables(ret_decay, chunk, width):
    lg = -jax.nn.softplus(-ret_decay.astype(F32))
    pos = jnp.arange(chunk, dtype=F32)
    diff = pos[:, None] - pos[None, :]
    d_f = jnp.where(diff >= 0, jnp.exp(lg[0][:, None, None] * jnp.maximum(diff, 0.0)[None]), 0.0)
    d_b = jnp.where(diff < 0, jnp.exp(lg[1][:, None, None] * jnp.maximum(-diff, 0.0)[None]), 0.0)
    cols = [jnp.exp(lg[0][:, None] * (chunk - 1.0 - pos)[None]),
            jnp.exp(lg[1][:, None] * pos[None]),
            jnp.exp(lg[0][:, None] * (pos + 1.0)[None]),
            jnp.exp(lg[1][:, None] * (chunk - pos)[None]),
            jnp.broadcast_to(jnp.exp(lg[0] * chunk)[:, None], (lg.shape[1], chunk)),
            jnp.broadcast_to(jnp.exp(lg[1] * chunk)[:, None], (lg.shape[1], chunk))]
    tab = jnp.stack(cols, axis=1)
    return d_f + d_b, jnp.broadcast_to(tab[..., None], tab.shape + (width,))


def _rope_tables(t, half, base):
    inv = base ** (-jnp.arange(half, dtype=F32) / half)
    ang = jnp.arange(t, dtype=F32)[:, None] * inv[None, :]
    cos, sin = jnp.cos(ang), jnp.sin(ang)
    return jnp.concatenate([cos, cos], axis=1), jnp.concatenate([-sin, sin], axis=1)


def _ret_body(q_ref, k_ref, v_ref, g_ref, cos_ref, sin_ref, d_ref, tab_ref, gn_ref, o_ref,
              qs, ks, sf, sb, *, chunk, nc, kscale, eps, group):
    dk = q_ref.shape[1]
    half = dk // 2
    cos, sin = cos_ref[...], sin_ref[...]
    q = q_ref[...]
    qs[...] = q * cos + pltpu.roll(q, half, 1) * sin
    k = k_ref[...]
    ks[...] = (k * cos + pltpu.roll(k, half, 1) * sin) * kscale
    contract0 = (((0,), (0,)), ((), ()))
    contract1 = (((1,), (1,)), ((), ()))

    k_f, k_b, q_f, q_b = tab_ref[0, 0], tab_ref[0, 1], tab_ref[0, 2], tab_ref[0, 3]
    cd_f, cd_b = tab_ref[0, 4], tab_ref[0, 5]

    def increments(n, carry):
        r0 = pl.multiple_of(n * chunk, chunk)
        kc = ks[pl.ds(r0, chunk), :]
        vc = v_ref[pl.ds(r0, chunk), :].astype(BF16)
        sf[n] = lax.dot_general((kc * k_f).astype(BF16), vc, contract0, preferred_element_type=F32)
        sb[n] = lax.dot_general((kc * k_b).astype(BF16), vc, contract0, preferred_element_type=F32)
        return carry

    lax.fori_loop(0, nc, increments, 0, unroll=group)

    def sweep_f(n, state):
        inc = sf[n]
        sf[n] = state
        return state * cd_f + inc

    def sweep_b(i, state):
        n = nc - 1 - i
        inc = sb[n]
        sb[n] = state
        return state * cd_b + inc

    zero = jnp.zeros((dk, v_ref.shape[1]), F32)
    lax.fori_loop(0, nc, sweep_f, zero)
    lax.fori_loop(0, nc, sweep_b, zero)

    def outputs(gi, carry):
        first = []
        for u in range(group):
            n = gi * group + u
            r0 = pl.multiple_of(n * chunk, chunk)
            qc = qs[pl.ds(r0, chunk), :]
            s = lax.dot_general(qc.astype(BF16), ks[pl.ds(r0, chunk), :].astype(BF16), contract1,
                                preferred_element_type=F32)
            cross = jnp.dot((qc * q_f).astype(BF16), sf[n].astype(BF16), preferred_element_type=F32)
            cross = cross + jnp.dot((qc * q_b).astype(BF16), sb[n].astype(BF16), preferred_element_type=F32)
            first.append((r0, s, cross))
        for r0, s, cross in first:
            vc = v_ref[pl.ds(r0, chunk), :].astype(BF16)
            o = jnp.dot((s * d_ref[0]).astype(BF16), vc, preferred_element_type=F32) + cross
            oc = o - jnp.mean(o, axis=-1, keepdims=True)
            o = oc * lax.rsqrt(jnp.mean(oc * oc, axis=-1, keepdims=True) + eps)
            g = g_ref[pl.ds(r0, chunk), :]
            o_ref[pl.ds(r0, chunk), :] = (o * gn_ref[...] * (g * _sigmoid(g))).astype(o_ref.dtype)
        return carry

    lax.fori_loop(0, nc // group, outputs, 0)


def _retention(zq, ret_decay, ret_gn, b, t, cfg):
    h, dk, c = cfg.ret_heads, cfg.ret_dk, cfg.ret_chunk
    assert t % c == 0 and zq.shape[1] == 4 * h * dk
    nc = t // c
    dmat, tab = _ret_tables(ret_decay, c, dk)
    cos2, sin2 = _rope_tables(t, dk // 2, cfg.rope_base)
    blk = lambda off: pl.BlockSpec((t, dk), lambda bi, hi, off=off: (bi, off + hi))
    est = 2 * (4 * t * dk * 4 + 2 * t * dk * 4 + c * c * 4 + 6 * c * dk * 4 + t * dk * 2) \
        + 2 * t * dk * 4 + 2 * nc * dk * dk * 4 + 8 * t * dk * 4
    return pl.pallas_call(
        functools.partial(_ret_body, chunk=c, nc=nc, kscale=dk ** -0.5, eps=cfg.eps,
                          group=math.gcd(cfg.ret_group, nc)),
        out_shape=jax.ShapeDtypeStruct((b * t, h * dk), BF16),
        grid=(b, h),
        in_specs=[blk(0), blk(h), blk(2 * h), blk(3 * h),
                  pl.BlockSpec((t, dk), lambda bi, hi: (0, 0)),
                  pl.BlockSpec((t, dk), lambda bi, hi: (0, 0)),
                  pl.BlockSpec((1, c, c), lambda bi, hi: (hi, 0, 0)),
                  pl.BlockSpec((1, 6, c, dk), lambda bi, hi: (hi, 0, 0, 0)),
                  pl.BlockSpec((1, dk), lambda bi, hi: (0, hi))],
        out_specs=pl.BlockSpec((t, dk), lambda bi, hi: (bi, hi)),
        scratch_shapes=[pltpu.VMEM((t, dk), F32), pltpu.VMEM((t, dk), F32),
                        pltpu.VMEM((nc, dk, dk), F32), pltpu.VMEM((nc, dk, dk), F32)],
        compiler_params=pltpu.CompilerParams(dimension_semantics=("parallel", "parallel"),
                                             vmem_limit_bytes=_vmem_limit(est)),
        name="retention",
    )(zq, zq, zq, zq, cos2, sin2, dmat, tab, ret_gn.reshape(1, h * dk).astype(F32))


def _s5_tables(lam_re, lam_im, log_dt, b_re, b_im, c_re, c_im, d_skip, glu_w, glu_b, L, nsteps):
    lam = lax.complex(lam_re.astype(F32), lam_im.astype(F32))
    ldt = lam * jnp.exp(log_dt.astype(F32))[..., None]
    lam_bar = jnp.exp(ldt)
    bmat = lax.complex(b_re.astype(F32), b_im.astype(F32))
    b_bar = ((lam_bar - 1.0) / lam)[..., None] * bmat[None]
    cmat = lax.complex(c_re.astype(F32), c_im.astype(F32))
    g, p, cg = bmat.shape
    tau = jnp.arange(L + 1, dtype=F32)
    pw = jnp.exp(ldt[:, :, None, :] * tau[None, None, :, None])
    kern = jnp.real(jnp.einsum('dgop,dgtp,dgpi->dgtoi', cmat, pw[:, :, :L], b_bar))
    li = np.arange(L)
    lag = li[None, :] - li[:, None]
    k_f = jnp.where((lag >= 0)[None, :, :, None, None], kern[0][:, np.clip(lag, 0, L - 1)], 0.0)
    k_b = jnp.where((lag <= 0)[None, :, :, None, None], kern[1][:, np.clip(-lag, 0, L - 1)], 0.0)
    m = (k_f + k_b).transpose(0, 1, 4, 2, 3).reshape(g, L * cg, L * cg)

    def cat(z):
        return jnp.concatenate([jnp.real(z), jnp.imag(z)], axis=-1)

    inc_f = cat(jnp.einsum('glp,gpi->glip', pw[0][:, L - 1 - li], b_bar[0])).reshape(g, L * cg, 2 * p)
    inc_b = cat(jnp.einsum('glp,gpi->glip', pw[1][:, li], b_bar[1])).reshape(g, L * cg, 2 * p)

    def out_mat(z):
        return jnp.concatenate([jnp.real(z), -jnp.imag(z)], axis=1).reshape(g, 2 * p, L * cg)

    out_f = out_mat(jnp.einsum('gop,glp->gplo', cmat[0], pw[0][:, li + 1]))
    out_b = out_mat(jnp.einsum('gop,glp->gplo', cmat[1], pw[1][:, L - li]))
    eye = jnp.eye(L, dtype=F32)
    glu = jnp.einsum('lm,gce->glcme', eye, glu_w.astype(F32)).reshape(g, L * cg, L * cg)
    vecs = jnp.stack([jnp.tile(d_skip.astype(F32), (1, L)), jnp.tile(glu_b.astype(F32), (1, L))], axis=1)
    steps = (2.0 ** jnp.arange(nsteps, dtype=F32)) * L
    a = jnp.exp(ldt[:, :, None, :] * steps[None, None, :, None])
    scan = jnp.stack([jnp.concatenate([jnp.real(a), jnp.real(a)], -1),
                      jnp.concatenate([-jnp.imag(a), jnp.imag(a)], -1)], axis=3)
    scan = scan.transpose(1, 0, 2, 3, 4).reshape(g, 2 * nsteps * 2, 2 * p)
    return (m.astype(BF16), inc_f.astype(BF16), inc_b.astype(BF16), out_f.astype(BF16),
            out_b.astype(BF16), glu.astype(BF16), vecs, scan)


def _s5_body(u_ref, m_ref, incf_ref, incb_ref, outf_ref, outb_ref, glu_ref, vec_ref, scan_ref, o_ref,
             *, ncs, nsteps, L):
    rows = u_ref.shape[0] // L
    lanes = u_ref.shape[1]
    u = jnp.concatenate([u_ref[pl.ds(l, rows, stride=L), :] for l in range(L)], axis=1)
    ub = u.astype(BF16)

    def swap(x):
        h = x.shape[1] // 2
        return jnp.concatenate([x[:, h:], x[:, :h]], axis=1)

    y = jnp.dot(ub, m_ref[0], preferred_element_type=F32)
    xf = jnp.dot(ub, incf_ref[0], preferred_element_type=F32)
    xb = jnp.dot(ub, incb_ref[0], preferred_element_type=F32)
    cidx = lax.rem(lax.broadcasted_iota(I32, (rows, 1), 0), ncs)
    for kk in range(nsteps):
        s = 1 << kk
        a_f, b_f = scan_ref[0, pl.ds(2 * kk, 1), :], scan_ref[0, pl.ds(2 * kk + 1, 1), :]
        a_b = scan_ref[0, pl.ds(2 * nsteps + 2 * kk, 1), :]
        b_b = scan_ref[0, pl.ds(2 * nsteps + 2 * kk + 1, 1), :]
        pf = pltpu.roll(xf, s, 0)
        xf = xf + jnp.where(cidx >= s, a_f * pf + b_f * swap(pf), 0.0)
        pb = pltpu.roll(xb, rows - s, 0)
        xb = xb + jnp.where(cidx < ncs - s, a_b * pb + b_b * swap(pb), 0.0)
    x_prev = jnp.where(cidx >= 1, pltpu.roll(xf, 1, 0), 0.0)
    x_next = jnp.where(cidx < ncs - 1, pltpu.roll(xb, rows - 1, 0), 0.0)
    y = y + jnp.dot(x_prev.astype(BF16), outf_ref[0], preferred_element_type=F32)
    y = y + jnp.dot(x_next.astype(BF16), outb_ref[0], preferred_element_type=F32)
    y = y + u * vec_ref[0, pl.ds(0, 1), :]
    yg = _gelu(y)
    z = jnp.dot(yg.astype(BF16), glu_ref[0], preferred_element_type=F32) + vec_ref[0, pl.ds(1, 1), :]
    out = yg * _sigmoid(z)
    for l in range(L):
        o_ref[pl.ds(l, rows, stride=L), :] = out[:, l * lanes:(l + 1) * lanes]


def _s5_slab_tables(tabs, gs, L, cg):
    m, inc_f, inc_b, out_f, out_b, glu, vecs, scan = tabs
    g = m.shape[0]
    s = g // gs
    p2 = inc_f.shape[2]
    eye = jnp.eye(gs, dtype=m.dtype)

    def tok_tok(x):
        x6 = x.reshape(s, gs, L, cg, L, cg)
        return jnp.einsum('sglcmd,gh->slgcmhd', x6, eye).reshape(s, L * gs * cg, L * gs * cg)

    ps = p2 // 2

    def tok_state(x):
        x6 = x.reshape(s, gs, L, cg, 2, ps)
        return jnp.einsum('sglcrp,gh->slgcrhp', x6, eye).reshape(s, L * gs * cg, gs * p2)

    def state_tok(x):
        x6 = x.reshape(s, gs, 2, ps, L, cg)
        return jnp.einsum('sgrplc,gh->srgplhc', x6, eye).reshape(s, gs * p2, L * gs * cg)

    vecs_s = vecs.reshape(s, gs, 2, L, cg).transpose(0, 2, 3, 1, 4).reshape(s, 2, L * gs * cg)
    nrow = scan.shape[1]
    scan_s = scan.reshape(s, gs, nrow, 2, ps).transpose(0, 2, 3, 1, 4).reshape(s, nrow, gs * p2)
    return (tok_tok(m), tok_state(inc_f), tok_state(inc_b), state_tok(out_f), state_tok(out_b), tok_tok(glu),
            vecs_s, scan_s)


def _s5(u, params, b, t, cfg):
    lam_re, lam_im, log_dt, b_re, b_im, c_re, c_im, d_skip, glu_w, glu_b = params
    L, cg = cfg.s5_chunk, cfg.s5_group
    n, ch = u.shape
    gs = V7X_LANES // cg
    assert t % L == 0 and ch % V7X_LANES == 0 and V7X_LANES % cg == 0
    ncs = t // L
    nsteps = max(1, (ncs - 1).bit_length())
    p2 = 2 * cfg.s5_state
    tabs = _s5_slab_tables(
        _s5_tables(lam_re, lam_im, log_dt, b_re, b_im, c_re, c_im, d_skip, glu_w, glu_b, L, nsteps), gs, L, cg)
    seqs = max(1, min(b, cfg.s5_rows // ncs))
    while b % seqs:
        seqs -= 1
    rows = seqs * ncs
    w = L * V7X_LANES
    ws = gs * p2
    per_s = lambda shape: pl.BlockSpec((1,) + shape, lambda si, ri: (si, 0, 0))
    est = 2 * 2 * rows * w * 4 + 2 * 2 * (2 * w * w + 4 * w * ws) + 10 * rows * w * 4 + 8 * rows * ws * 4
    tok = pl.BlockSpec((rows * L, V7X_LANES), lambda si, ri: (ri, si))
    return pl.pallas_call(
        functools.partial(_s5_body, ncs=ncs, nsteps=nsteps, L=L),
        out_shape=jax.ShapeDtypeStruct((n, ch), F32),
        grid=(ch // V7X_LANES, n // (rows * L)),
        in_specs=[tok, per_s((w, w)), per_s((w, ws)), per_s((w, ws)), per_s((ws, w)), per_s((ws, w)),
                  per_s((w, w)), per_s((2, w)), per_s((4 * nsteps, ws))],
        out_specs=tok,
        compiler_params=pltpu.CompilerParams(dimension_semantics=("parallel", "parallel"),
                                             vmem_limit_bytes=_vmem_limit(est)),
        name="s5",
    )(u, *tabs)


def _na_bias_tables(rpb, cfg, rows):
    kh, kw, w = cfg.na_kh, cfg.na_kw, cfg.grid_w
    j = np.arange(w)
    c_start = np.clip(j - kw // 2, 0, w - kw)
    c = np.arange(w)
    inside = (c[None, :] >= c_start[:, None]) & (c[None, :] < c_start[:, None] + kw)
    by_row = jnp.stack([rpb.astype(F32)[:, kh - 1 - dl:2 * kh - 1 - dl] for dl in range(kh)], axis=1)
    padded = jnp.pad(by_row, ((0, 0), (0, 0), (0, 0), (w - kw, w - kw)))
    bias = jnp.stack([padded[..., w - 1 - jq:2 * w - 1 - jq] for jq in range(w)], axis=2)
    bias = jnp.where(inside[None, None, :, None, :], bias, NEG_BIG)
    return bias.reshape(rpb.shape[0], kh, w, kh * w)


def _na_body(q_ref, kn, vb, tab_ref, o_ref, qn, *, w, rows, kh, dh, unroll):
    lo = lax.broadcasted_iota(I32, (1, 2 * dh), 1) < dh
    qh = q_ref[...]
    qn[0] = jnp.where(lo, qh, jnp.zeros_like(qh))
    qn[1] = jnp.where(lo, jnp.zeros_like(qh), qh)
    contract1 = (((1,), (1,)), ((), ()))

    def group(gi, carry):
        chains = []
        for u in range(unroll):
            r = gi * unroll + u
            rs = jnp.clip(r - kh // 2, 0, rows - kh)
            q0 = pl.multiple_of(r * w, w)
            k0 = pl.multiple_of(rs * w, w)
            for hh in range(2):
                s = lax.dot_general(qn[hh, pl.ds(q0, w), :], kn[pl.ds(k0, kh * w), :], contract1,
                                    preferred_element_type=F32)
                chains.append((s + tab_ref[hh, r - rs], k0, q0))
        outs = []
        for s, k0, q0 in chains:
            p = jnp.exp(s - jnp.max(s, axis=-1, keepdims=True))
            p = p / jnp.sum(p, axis=-1, keepdims=True)
            outs.append(jnp.dot(p.astype(BF16), vb[pl.ds(k0, kh * w), :], preferred_element_type=F32))
        for u in range(unroll):
            q0 = chains[2 * u][2]
            o_ref[pl.ds(q0, w), :] = jnp.where(lo, outs[2 * u], outs[2 * u + 1]).astype(o_ref.dtype)
        return carry

    lax.fori_loop(0, rows // unroll, group, 0)


def _qkv_body(x_ref, g_ref, w_ref, qg_ref, kg_ref, o_ref, *, eps, dh, width):
    x = x_ref[...]
    y = x * lax.rsqrt(jnp.mean(x * x, axis=-1, keepdims=True) + eps) * g_ref[...]
    z = jnp.dot(y.astype(BF16), w_ref[...], preferred_element_type=F32)
    lanes = V7X_MXU_WIDTH
    shift = dh.bit_length() - 1
    same_head = (lax.shift_right_logical(lax.broadcasted_iota(I32, (lanes, lanes), 0), shift)
                 == lax.shift_right_logical(lax.broadcasted_iota(I32, (lanes, lanes), 1), shift))
    head_mean = jnp.where(same_head, 1.0 / dh, 0.0).astype(BF16)

    def head_norm(c0, gain_ref, scale):
        for c in range(0, width, lanes):
            xc = z[:, c0 + c:c0 + c + lanes]
            x2 = xc * xc
            hi = x2.astype(BF16)
            rest = (x2 - hi.astype(F32)).astype(BF16)
            ms = (jnp.dot(hi, head_mean, preferred_element_type=F32)
                  + jnp.dot(rest, head_mean, preferred_element_type=F32))
            out = xc * lax.rsqrt(ms + eps) * gain_ref[:, c:c + lanes]
            o_ref[:, c0 + c:c0 + c + lanes] = (out * scale if scale != 1.0 else out).astype(o_ref.dtype)

    head_norm(0, qg_ref, dh ** -0.5)
    head_norm(width, kg_ref, 1.0)
    o_ref[:, 2 * width:] = z[:, 2 * width:].astype(o_ref.dtype)


def _qkv_projection(x2, gain, w_bf16, q_gain, k_gain, cfg):
    n, d = x2.shape
    nout = w_bf16.shape[1]
    width = nout // 3
    dh = width // cfg.na_heads
    tm = min(cfg.tm, n)
    assert dh & (dh - 1) == 0 and V7X_MXU_WIDTH % dh == 0 and width % V7X_MXU_WIDTH == 0
    est = 2 * (tm * d * 4 + d * nout * 2 + tm * nout * 2) + 3 * tm * nout * 4
    row = lambda a: a.reshape(1, width).astype(F32)
    return pl.pallas_call(
        functools.partial(_qkv_body, eps=cfg.eps, dh=dh, width=width),
        out_shape=jax.ShapeDtypeStruct((n, nout), BF16),
        grid=(n // tm,),
        in_specs=[pl.BlockSpec((tm, d), lambda i: (i, 0)),
                  pl.BlockSpec((1, d), lambda i: (0, 0)),
                  pl.BlockSpec((d, nout), lambda i: (0, 0)),
                  pl.BlockSpec((1, width), lambda i: (0, 0)),
                  pl.BlockSpec((1, width), lambda i: (0, 0))],
        out_specs=pl.BlockSpec((tm, nout), lambda i: (i, 0)),
        compiler_params=pltpu.CompilerParams(dimension_semantics=("parallel",),
                                             vmem_limit_bytes=_vmem_limit(est)),
        name="qkv_projection",
    )(x2, gain.reshape(1, d).astype(F32), w_bf16, row(q_gain), row(k_gain))


def _neighbourhood_attention(z, rpb, b, t, cfg):
    h, w, kh = cfg.na_heads, cfg.grid_w, cfg.na_kh
    dh = z.shape[1] // (3 * h)
    rows = t // w
    assert t % w == 0 and rows >= kh and h % 2 == 0 and 2 * dh == V7X_LANES
    tab = _na_bias_tables(rpb, cfg, rows)
    hp = h // 2
    blk = lambda off: pl.BlockSpec((t, 2 * dh), lambda bi, pi, off=off: (bi, off + pi))
    est = 2 * (3 * t * 2 * dh * 2 + 2 * kh * w * kh * w * 4 + t * 2 * dh * 2) + 2 * t * 2 * dh * 2 \
        + 6 * t * 2 * dh * 4
    return pl.pallas_call(
        functools.partial(_na_body, w=w, rows=rows, kh=kh, dh=dh, unroll=math.gcd(cfg.na_unroll, rows)),
        out_shape=jax.ShapeDtypeStruct((b * t, h * dh), BF16),
        grid=(b, hp),
        in_specs=[blk(0), blk(hp), blk(2 * hp),
                  pl.BlockSpec((2, kh, w, kh * w), lambda bi, pi: (pi, 0, 0, 0))],
        out_specs=pl.BlockSpec((t, 2 * dh), lambda bi, pi: (bi, pi)),
        scratch_shapes=[pltpu.VMEM((2, t, 2 * dh), BF16)],
        compiler_params=pltpu.CompilerParams(dimension_semantics=("parallel", "parallel"),
                                             vmem_limit_bytes=_vmem_limit(est)),
        name="neighbourhood_attention",
    )(z, z, z, tab)


def _peer_cells(topk):
    return [(k1, k2) for k1 in range(topk) for k2 in range(topk) if (k1 + 1) * (k2 + 1) <= topk]


W_PITCH_PAD = 8
HIGH_HALF = np.uint32(0xFFFF0000)
HALF_ULP_BF16 = np.uint32(0x8000)


class _RouteScratch(NamedTuple):
    s: object
    ts: object
    ti: object
    c_s: object
    c_e: object
    best: object
    e_t: object
    g_t: object


def _route_scores(qp_ref, keys_ref, rs):
    dq = keys_ref.shape[2]
    contract1 = (((1,), (1,)), ((), ()))
    for p in range(2):
        rs.s[p] = lax.dot_general(keys_ref[p], qp_ref[:, p * dq:(p + 1) * dq], contract1,
                                  preferred_element_type=F32)


def _route_first_stage(rs, it0, n_it):
    nkeys, th = rs.s.shape[1:]
    iota = lax.broadcasted_iota(I32, (nkeys, th), 0).astype(F32)
    for p in range(2):
        s = rs.s[p]
        for k in range(n_it):
            m = jnp.max(s, axis=0, keepdims=True)
            idx = jnp.min(jnp.where(s == m, iota, float(nkeys)), axis=0, keepdims=True)
            rs.ts[p, pl.ds(it0 + k, 1), :] = m
            rs.ti[p, pl.ds(it0 + k, 1), :] = idx
            s = jnp.where(iota == idx, -jnp.inf, s)
        rs.s[p] = s


def _route_cells(rs, cells):
    crow, th = rs.c_s.shape
    nkeys = rs.s.shape[1]
    ncell = len(cells)
    rs.c_s[pl.ds(ncell, crow - ncell), :] = jnp.full((crow - ncell, th), -jnp.inf, F32)
    rs.c_e[pl.ds(ncell, crow - ncell), :] = jnp.zeros((crow - ncell, th), F32)
    for ci, (k1, k2) in enumerate(cells):
        rs.c_s[pl.ds(ci, 1), :] = rs.ts[0, pl.ds(k1, 1), :] + rs.ts[1, pl.ds(k2, 1), :]
        rs.c_e[pl.ds(ci, 1), :] = rs.ti[0, pl.ds(k1, 1), :] * float(nkeys) + rs.ti[1, pl.ds(k2, 1), :]


def _route_second_stage(rs, it0, n_it, blk0, row0):
    crow, th = rs.c_s.shape
    iota = lax.broadcasted_iota(I32, (crow, th), 0).astype(F32)
    cs, ce = rs.c_s[...], rs.c_e[...]
    for k in range(n_it):
        m = jnp.max(cs, axis=0, keepdims=True)
        pos = jnp.min(jnp.where(cs == m, iota, float(crow)), axis=0, keepdims=True)
        hit = iota == pos
        e_row = jnp.sum(jnp.where(hit, ce, 0.0), axis=0, keepdims=True)
        for c in range(th // V7X_LANES):
            rs.e_t[blk0 + c, pl.ds(row0 + it0 + k, 1), :] = e_row[:, c * V7X_LANES:(c + 1) * V7X_LANES]
        rs.best[pl.ds(it0 + k, 1), :] = m
        cs = jnp.where(hit, -jnp.inf, cs)
    rs.c_s[...] = cs


def _route_gates(rs, blk0, row0):
    topk, th = rs.best.shape
    best = rs.best[...]
    e = jnp.exp(best - best[0:1, :])
    gates = e / jnp.sum(e, axis=0, keepdims=True)
    for c in range(th // V7X_LANES):
        rs.g_t[blk0 + c, pl.ds(row0, topk), :] = gates[:, c * V7X_LANES:(c + 1) * V7X_LANES]


PEER_SUB = 2 * V7X_LANES


def _peer_pipe_body(hn_ref, qp_ref, keys_ref, ut_ref, v_ref, x_ref, o_ref,
                    wmap, acc, pmat, e_cur, g_cur, s_scr, ts, ti, c_s, c_e, best, e_t, g_t,
                    *, nk, pitch, unroll, heads, topk, cells, nhalf, nsteps):
    r = pl.program_id(0)
    j = pl.program_id(1)
    tm = hn_ref.shape[0]
    th = qp_ref.shape[0]
    nsub, _, sub = ut_ref.shape
    nout = acc.shape[0]
    half = nk // 2
    ncol = th // V7X_LANES
    it_per = topk // nsub
    nj = heads * topk
    rs = _RouteScratch(s_scr, ts, ti, c_s, c_e, best, e_t, g_t)
    contract1 = (((1,), (1,)), ((), ()))

    @pl.when(j == 0)
    def _():
        @pl.when(r == 0)
        def _():
            for ref in (acc, pmat, e_cur, g_cur, ts, ti):
                ref[...] = jnp.zeros_like(ref)

        @pl.when(r > 0)
        def _():
            blk0 = ((nsteps - 1) % nhalf) * ncol
            row0 = ((nsteps - 1) // nhalf) * topk
            _route_cells(rs, cells)
            _route_second_stage(rs, 0, topk, blk0, row0)
            _route_gates(rs, blk0, row0)
            for cb in range(tm // V7X_LANES):
                rows = slice(cb * V7X_LANES, (cb + 1) * V7X_LANES)
                e_cur[rows, :] = e_t[cb, 0:nj, :].T
                g_cur[rows, :] = g_t[cb, 0:nj, :].T

        irow = lax.broadcasted_iota(I32, (nk, nj), 0)
        row = irow.astype(F32)
        a_of_row = jnp.where(irow < half, 2 * irow, 2 * (irow - half) + 1).astype(F32)

        def build(n, carry):
            er = e_cur[pl.ds(n, 1), :]
            ar = jnp.floor(er * (1.0 / nk))
            br = er - ar * nk
            gr = g_cur[pl.ds(n, 1), :] * 0.5
            one_a = jnp.where(a_of_row == ar, 1.0, 0.0).astype(BF16)
            gate_b = jnp.where(row == br, gr, 0.0).astype(BF16)
            wn = lax.dot_general(one_a, gate_b, contract1, preferred_element_type=F32)
            lo = pltpu.bitcast(wn[:half], jnp.uint32)
            hi = pltpu.bitcast(wn[half:], jnp.uint32)
            packed = ((hi + HALF_ULP_BF16) & HIGH_HALF) | ((lo + HALF_ULP_BF16) >> 16)
            wmap[pl.ds(pl.multiple_of(n * pitch, V7X_SUBLANES), half), :] = packed
            return carry

        lax.fori_loop(0, tm, build, 0, unroll=unroll)

    ju = jnp.maximum(j - 1, 0)
    c_blk0 = lax.rem(ju, nhalf) * ncol
    c_row0 = pl.multiple_of(jnp.where(j == 0, nj, lax.div(ju, nhalf) * topk), topk)
    _route_scores(qp_ref, keys_ref, rs)
    _route_cells(rs, cells)
    cur = lax.rem(j, 2)
    prev = 1 - cur

    def trip(sb, carry):
        ob = lax.rem(sb, nout)
        k0 = lax.div(sb, nout) * nout
        part = acc[ob]
        for k in range(nout):
            rows = pl.ds(pl.multiple_of((k0 + k) * sub, sub), sub)
            part = part + jnp.dot(pmat[prev * nsub + k0 + k], v_ref[ob, rows, :], preferred_element_type=F32)
        acc[ob] = part
        xv = jnp.dot(hn_ref[...], ut_ref[sb], preferred_element_type=F32)
        act = xv * (1.0 + lax.erf(xv * (1.0 / math.sqrt(2.0))))
        parts = []
        for i in range(sub // (2 * nk)):
            w32 = wmap[pl.ds((j * nsub + sb) * (sub // (2 * nk)) + i, tm, stride=pitch), :]
            parts.append(act[:, 2 * i * nk:(2 * i + 1) * nk] * pltpu.bitcast(w32 << 16, F32))
            parts.append(act[:, (2 * i + 1) * nk:(2 * i + 2) * nk] * pltpu.bitcast(w32 & HIGH_HALF, F32))
        pmat[cur * nsub + sb] = jnp.concatenate(parts, axis=1).astype(BF16)
        _route_first_stage(rs, sb * it_per, it_per)
        _route_second_stage(rs, sb * it_per, it_per, c_blk0, c_row0)
        return carry

    lax.fori_loop(0, nsub, trip, 0)
    _route_gates(rs, c_blk0, c_row0)

    @pl.when(j == 0)
    def _():
        for k in range(nout):
            o_ref[:, k * sub:(k + 1) * sub] = x_ref[:, k * sub:(k + 1) * sub] + acc[k]
        acc[...] = jnp.zeros_like(acc)


def _peer_fused(hn, qp, sub_keys, u3, v4, x2, cfg):
    n, d = x2.shape
    nexp = u3.shape[0] * u3.shape[2]
    nk, heads, topk = cfg.peer_nkeys, cfg.peer_heads, cfg.peer_topk
    dq = sub_keys.shape[2]
    nj = heads * topk
    sub = PEER_SUB
    nout = d // sub
    te = max(d, min(cfg.peer_te, nexp) // d * d)
    nsub = te // sub
    nsteps = nexp // te
    assert nk == V7X_LANES and nexp == nk * nk and sub % (2 * nk) == 0 and d % sub == 0 and nexp % te == 0
    assert te % d == 0 and nj == V7X_LANES and qp.shape[1] == heads * 2 * dq and topk % nsub == 0
    assert nsteps % heads == 0 and nsteps % 2 == 0
    nhalf = nsteps // heads
    tm = min(cfg.peer_tm, n)
    tm = max(tm, nhalf * V7X_LANES)
    assert n % tm == 0 and tm % nhalf == 0
    th = tm // nhalf
    assert th % V7X_LANES == 0
    ntiles = n // tm
    pitch = nk // 2 + W_PITCH_PAD
    cells = _peer_cells(topk)
    crow = -(-(len(cells) + 1) // V7X_SUBLANES) * V7X_SUBLANES
    nblk = tm // V7X_LANES
    est = 2 * (tm * d * 2 + th * 2 * dq * 2 + 2 * d * te * 2 + 2 * tm * d * 4) \
        + tm * pitch * nk * 4 + tm * d * 4 + 2 * tm * te * 2 + 3 * tm * nj * 4 \
        + 3 * nblk * (nj + topk) * V7X_LANES * 4 + (3 * crow + 2 * nk + 5 * topk) * th * 4 + 6 * tm * sub * 4
    tile_in = pl.BlockSpec((tm, d), lambda r, j: (jnp.clip(r - 1, 0, ntiles - 1), 0))
    tile_out = pl.BlockSpec((tm, d), lambda r, j: (jnp.clip(jnp.where(j == 0, r - 2, r - 1), 0, ntiles - 1), 0))
    return pl.pallas_call(
        functools.partial(_peer_pipe_body, nk=nk, pitch=pitch, unroll=min(cfg.peer_unroll, tm), heads=heads,
                          topk=topk, cells=tuple(cells), nhalf=nhalf, nsteps=nsteps),
        out_shape=jax.ShapeDtypeStruct((n, d), F32),
        grid=(ntiles + 2, nsteps),
        in_specs=[tile_in,
                  pl.BlockSpec((th, 2 * dq),
                               lambda r, j: (nhalf * jnp.minimum(r, ntiles - 1) + j % nhalf, j // nhalf)),
                  pl.BlockSpec(sub_keys.shape, lambda r, j: (0, 0, 0)),
                  pl.BlockSpec((nsub, d, sub), lambda r, j: (j, 0, 0)),
                  pl.BlockSpec((nout, te, sub), lambda r, j: (0, (j + nsteps - 1) % nsteps, 0)),
                  tile_out],
        out_specs=tile_out,
        scratch_shapes=[pltpu.VMEM((tm * pitch, nk), jnp.uint32),
                        pltpu.VMEM((nout, tm, sub), F32), pltpu.VMEM((2 * nsub, tm, sub), BF16),
                        pltpu.VMEM((tm, nj), F32), pltpu.VMEM((tm, nj), F32),
                        pltpu.VMEM((2, nk, th), F32), pltpu.VMEM((2, topk, th), F32), pltpu.VMEM((2, topk, th), F32),
                        pltpu.VMEM((crow, th), F32), pltpu.VMEM((crow, th), F32),
                        pltpu.VMEM((topk, th), F32),
                        pltpu.VMEM((nblk, nj + topk, V7X_LANES), F32), pltpu.VMEM((nblk, nj + topk, V7X_LANES), F32)],
        compiler_params=pltpu.CompilerParams(dimension_semantics=("arbitrary", "arbitrary"),
                                             vmem_limit_bytes=_vmem_limit(est)),
        name="peer_fused",
    )(hn, qp, sub_keys.astype(BF16), u3, v4, x2)


def _peer(x2, gain, w_q_bf16, sub_keys, u3, v4, cfg):
    nq = w_q_bf16.shape[1]
    qp, hn = _norm_matmul(x2, gain, w_q_bf16, [(0, nq)], [BF16], cfg, emit_hn=True)
    return _peer_fused(hn, qp, sub_keys, u3, v4, x2, cfg)


def _trunk(x, p, cfg):
    b, t, d = x.shape
    x2 = x.reshape(b * t, d)
    depth = p["norm_mix"].shape[0]
    for l in range(depth):
        i = l // 2
        if l % 2 == 0:
            rw = 4 * cfg.ret_heads * cfg.ret_dk
            w_in = p["ab_w_in"][i]
            zq, u = _norm_matmul(x2, p["norm_mix"][l], w_in, [(0, rw), (rw, w_in.shape[1])], [F32, F32], cfg)
            ret = _retention(zq, p["ab_ret_decay"][i], p["ab_ret_gn"][i], b, t, cfg)
            ssm = _s5(u, tuple(p[k][i] for k in ("ab_s5_lam_re", "ab_s5_lam_im", "ab_s5_log_dt", "ab_s5_b_re",
                                                  "ab_s5_b_im", "ab_s5_c_re", "ab_s5_c_im", "ab_s5_d",
                                                  "ab_s5_glu_w", "ab_s5_glu_b")), b, t, cfg)
            w_out = p["ab_w_out"][i]
            nr = ret.shape[1]
            x2 = _matmul_residual([ret, ssm], [w_out[:nr], w_out[nr:]], x2, cfg)
        else:
            z = _qkv_projection(x2, p["norm_mix"][l], p["na_w_qkv"][i], p["na_q_gain"][i], p["na_k_gain"][i], cfg)
            att = _neighbourhood_attention(z, p["na_rpb"][i], b, t, cfg)
            x2 = _matmul_residual([att], [p["na_w_o"][i]], x2, cfg)
        x2 = _peer(x2, p["norm_ffn"][l], p["peer_w_q"][l], p["peer_sub_keys"][l], p["peer_u3"][l],
                   p["peer_v4"][l], cfg)
    return x2.reshape(b, t, d)


def _prepare(params):
    p = dict(params)
    for k in ("ab_w_in", "ab_w_out", "na_w_qkv", "na_w_o", "peer_w_q"):
        p[k] = params[k].astype(BF16)
    u, v = params["peer_u"].astype(BF16), params["peer_v"].astype(BF16)
    layers, nexp, d = u.shape
    p["peer_u3"] = u.reshape(layers, nexp // PEER_SUB, PEER_SUB, d).transpose(0, 1, 3, 2)
    p["peer_v4"] = v.reshape(layers, nexp, d // PEER_SUB, PEER_SUB).transpose(0, 2, 1, 3)
    return p


def _forward(x_prompt, x_sample, params, cfg=Cfg()):
    p = _prepare(params)
    return _trunk(x_prompt, p, cfg), _trunk(x_sample, p, cfg)


def kernel(x_prompt, x_sample, norm_mix, norm_ffn, ab_w_in, ab_ret_decay, ab_ret_gn, ab_s5_lam_re, ab_s5_lam_im, ab_s5_log_dt, ab_s5_b_re, ab_s5_b_im, ab_s5_c_re, ab_s5_c_im, ab_s5_d, ab_s5_glu_w, ab_s5_glu_b, ab_w_out, na_w_qkv, na_q_gain, na_k_gain, na_rpb, na_w_o, peer_w_q, peer_sub_keys, peer_u, peer_v):
    params = dict(norm_mix=norm_mix, norm_ffn=norm_ffn, ab_w_in=ab_w_in, ab_ret_decay=ab_ret_decay,
                  ab_ret_gn=ab_ret_gn, ab_s5_lam_re=ab_s5_lam_re, ab_s5_lam_im=ab_s5_lam_im,
                  ab_s5_log_dt=ab_s5_log_dt, ab_s5_b_re=ab_s5_b_re, ab_s5_b_im=ab_s5_b_im,
                  ab_s5_c_re=ab_s5_c_re, ab_s5_c_im=ab_s5_c_im, ab_s5_d=ab_s5_d, ab_s5_glu_w=ab_s5_glu_w,
                  ab_s5_glu_b=ab_s5_glu_b, ab_w_out=ab_w_out, na_w_qkv=na_w_qkv, na_q_gain=na_q_gain,
                  na_k_gain=na_k_gain, na_rpb=na_rpb, na_w_o=na_w_o, peer_w_q=peer_w_q,
                  peer_sub_keys=peer_sub_keys, peer_u=peer_u, peer_v=peer_v)
    return _forward(x_prompt, x_sample, params)
```

```python
import functools
import math
from typing import NamedTuple

import numpy as np
import jax
import jax.numpy as jnp
from jax import lax
from jax.experimental import pallas as pl
from jax.experimental.pallas import tpu as pltpu

F32 = jnp.float32
BF16 = jnp.bfloat16
I32 = jnp.int32

V7X_LANES = 128
V7X_SUBLANES = 8
V7X_MXU_WIDTH = 256
V7X_VMEM_BYTES = 64 * 2**20
VMEM_LIMIT_CAP = V7X_VMEM_BYTES - 8 * 2**20

NEG_BIG = -1e30


class Cfg(NamedTuple):
    eps: float = 1e-6
    grid_w: int = 64
    ret_heads: int = 4
    ret_dk: int = 128
    ret_chunk: int = 128
    rope_base: float = 10000.0
    s5_group: int = 16
    s5_state: int = 64
    s5_chunk: int = 8
    s5_rows: int = 512
    na_heads: int = 16
    na_kh: int = 8
    na_kw: int = 16
    peer_heads: int = 8
    peer_nkeys: int = 128
    peer_topk: int = 16
    tm: int = 512
    peer_tm: int = 512
    peer_unroll: int = 128
    na_unroll: int = 8
    ret_group: int = 8
    peer_te: int = 1024


def _vmem_limit(nbytes):
    return int(min(VMEM_LIMIT_CAP, max(32 * 2**20, nbytes)))


def _gelu(x):
    return 0.5 * x * (1.0 + lax.erf(x * (1.0 / math.sqrt(2.0))))


def _sigmoid(x):
    return 1.0 / (1.0 + jnp.exp(-x))


def _norm_mm_body(x_ref, g_ref, w_ref, *out_refs, splits, emit_hn, eps):
    x = x_ref[...]
    y = x * lax.rsqrt(jnp.mean(x * x, axis=-1, keepdims=True) + eps) * g_ref[...]
    yb = y.astype(BF16)
    z = jnp.dot(yb, w_ref[...], preferred_element_type=F32)
    for r, (s, e) in zip(out_refs, splits):
        r[...] = z[:, s:e].astype(r.dtype)
    if emit_hn:
        out_refs[len(splits)][...] = yb


def _norm_matmul(x2, gain, w_bf16, splits, dtypes, cfg, emit_hn=False):
    n, d = x2.shape
    nout = w_bf16.shape[1]
    tm = min(cfg.tm, n)
    out_shape = [jax.ShapeDtypeStruct((n, e - s), dt) for (s, e), dt in zip(splits, dtypes)]
    out_specs = [pl.BlockSpec((tm, e - s), lambda i: (i, 0)) for (s, e) in splits]
    if emit_hn:
        out_shape.append(jax.ShapeDtypeStruct((n, d), BF16))
        out_specs.append(pl.BlockSpec((tm, d), lambda i: (i, 0)))
    est = 2 * (tm * d * 4 + d * nout * 2 + tm * nout * 4 + tm * d * 2) + 2 * tm * nout * 4
    return pl.pallas_call(
        functools.partial(_norm_mm_body, splits=tuple(splits), emit_hn=emit_hn, eps=cfg.eps),
        out_shape=out_shape,
        grid=(n // tm,),
        in_specs=[pl.BlockSpec((tm, d), lambda i: (i, 0)),
                  pl.BlockSpec((1, d), lambda i: (0, 0)),
                  pl.BlockSpec((d, nout), lambda i: (0, 0))],
        out_specs=out_specs,
        compiler_params=pltpu.CompilerParams(dimension_semantics=("parallel",),
                                             vmem_limit_bytes=_vmem_limit(est)),
        name="norm_matmul",
    )(x2, gain.reshape(1, d).astype(F32), w_bf16)


def _mm_res_body(*refs, n_in):
    acc = refs[2 * n_in][...]
    for a, w in zip(refs[:n_in], refs[n_in:2 * n_in]):
        acc = acc + jnp.dot(a[...].astype(BF16), w[...], preferred_element_type=F32)
    refs[-1][...] = acc


def _matmul_residual(a_list, w_list, res, cfg):
    n, d = res.shape
    tm = min(cfg.tm, n)
    n_in = len(a_list)
    in_specs = ([pl.BlockSpec((tm, a.shape[1]), lambda i: (i, 0)) for a in a_list]
                + [pl.BlockSpec(w.shape, lambda i: (0, 0)) for w in w_list]
                + [pl.BlockSpec((tm, d), lambda i: (i, 0))])
    est = 2 * (sum(tm * a.shape[1] * a.dtype.itemsize for a in a_list) + sum(w.size * 2 for w in w_list)
               + 2 * tm * d * 4) + 2 * tm * d * 4
    return pl.pallas_call(
        functools.partial(_mm_res_body, n_in=n_in),
        out_shape=jax.ShapeDtypeStruct((n, d), F32),
        grid=(n // tm,),
        in_specs=in_specs,
        out_specs=pl.BlockSpec((tm, d), lambda i: (i, 0)),
        compiler_params=pltpu.CompilerParams(dimension_semantics=("parallel",),
                                             vmem_limit_bytes=_vmem_limit(est)),
        name="matmul_residual",
    )(*a_list, *w_list, res)


def _ret_tables(ret_decay, chunk, width):
    lg = -jax.nn.softplus(-ret_decay.astype(F32))
    pos = jnp.arange(chunk, dtype=F32)
    diff = pos[:, None] - pos[None, :]
    d_f = jnp.where(diff >= 0, jnp.exp(lg[0][:, None, None] * jnp.maximum(diff, 0.0)[None]), 0.0)
    d_b = jnp.where(diff < 0, jnp.exp(lg[1][:, None, None] * jnp.maximum(-diff, 0.0)[None]), 0.0)
    cols = [jnp.exp(lg[0][:, None] * (chunk - 1.0 - pos)[None]),
            jnp.exp(lg[1][:, None] * pos[None]),
            jnp.exp(lg[0][:, None] * (pos + 1.0)[None]),
            jnp.exp(lg[1][:, None] * (chunk - pos)[None]),
            jnp.broadcast_to(jnp.exp(lg[0] * chunk)[:, None], (lg.shape[1], chunk)),
            jnp.broadcast_to(jnp.exp(lg[1] * chunk)[:, None], (lg.shape[1], chunk))]
    tab = jnp.stack(cols, axis=1)
    return d_f + d_b, jnp.broadcast_to(tab[..., None], tab.shape + (width,))


def _rope_tables(t, half, base):
    inv = base ** (-jnp.arange(half, dtype=F32) / half)
    ang = jnp.arange(t, dtype=F32)[:, None] * inv[None, :]
    cos, sin = jnp.cos(ang), jnp.sin(ang)
    return jnp.concatenate([cos, cos], axis=1), jnp.concatenate([-sin, sin], axis=1)


def _ret_body(q_ref, k_ref, v_ref, g_ref, cos_ref, sin_ref, d_ref, tab_ref, gn_ref, o_ref,
              qs, ks, sf, sb, *, chunk, nc, kscale, eps, group):
    dk = q_ref.shape[1]
    half = dk // 2
    cos, sin = cos_ref[...], sin_ref[...]
    q = q_ref[...]
    qs[...] = q * cos + pltpu.roll(q, half, 1) * sin
    k = k_ref[...]
    ks[...] = (k * cos + pltpu.roll(k, half, 1) * sin) * kscale
    contract0 = (((0,), (0,)), ((), ()))
    contract1 = (((1,), (1,)), ((), ()))

    k_f, k_b, q_f, q_b = tab_ref[0, 0], tab_ref[0, 1], tab_ref[0, 2], tab_ref[0, 3]
    cd_f, cd_b = tab_ref[0, 4], tab_ref[0, 5]

    def increments(n, carry):
        r0 = pl.multiple_of(n * chunk, chunk)
        kc = ks[pl.ds(r0, chunk), :]
        vc = v_ref[pl.ds(r0, chunk), :].astype(BF16)
        sf[n] = lax.dot_general((kc * k_f).astype(BF16), vc, contract0, preferred_element_type=F32)
        sb[n] = lax.dot_general((kc * k_b).astype(BF16), vc, contract0, preferred_element_type=F32)
        return carry

    lax.fori_loop(0, nc, increments, 0, unroll=group)

    def sweep_f(n, state):
        inc = sf[n]
        sf[n] = state
        return state * cd_f + inc

    def sweep_b(i, state):
        n = nc - 1 - i
        inc = sb[n]
        sb[n] = state
        return state * cd_b + inc

    zero = jnp.zeros((dk, v_ref.shape[1]), F32)
    lax.fori_loop(0, nc, sweep_f, zero)
    lax.fori_loop(0, nc, sweep_b, zero)

    def outputs(gi, carry):
        first = []
        for u in range(group):
            n = gi * group + u
            r0 = pl.multiple_of(n * chunk, chunk)
            qc = qs[pl.ds(r0, chunk), :]
            s = lax.dot_general(qc.astype(BF16), ks[pl.ds(r0, chunk), :].astype(BF16), contract1,
                                preferred_element_type=F32)
            cross = jnp.dot((qc * q_f).astype(BF16), sf[n].astype(BF16), preferred_element_type=F32)
            cross = cross + jnp.dot((qc * q_b).astype(BF16), sb[n].astype(BF16), preferred_element_type=F32)
            first.append((r0, s, cross))
        for r0, s, cross in first:
            vc = v_ref[pl.ds(r0, chunk), :].astype(BF16)
            o = jnp.dot((s * d_ref[0]).astype(BF16), vc, preferred_element_type=F32) + cross
            oc = o - jnp.mean(o, axis=-1, keepdims=True)
            o = oc * lax.rsqrt(jnp.mean(oc * oc, axis=-1, keepdims=True) + eps)
            g = g_ref[pl.ds(r0, chunk), :]
            o_ref[pl.ds(r0, chunk), :] = (o * gn_ref[...] * (g * _sigmoid(g))).astype(o_ref.dtype)
        return carry

    lax.fori_loop(0, nc // group, outputs, 0)


def _retention(zq, ret_decay, ret_gn, b, t, cfg):
    h, dk, c = cfg.ret_heads, cfg.ret_dk, cfg.ret_chunk
    assert t % c == 0 and zq.shape[1] == 4 * h * dk
    nc = t // c
    dmat, tab = _ret_tables(ret_decay, c, dk)
    cos2, sin2 = _rope_tables(t, dk // 2, cfg.rope_base)
    blk = lambda off: pl.BlockSpec((t, dk), lambda bi, hi, off=off: (bi, off + hi))
    est = 2 * (4 * t * dk * 4 + 2 * t * dk * 4 + c * c * 4 + 6 * c * dk * 4 + t * dk * 2) \
        + 2 * t * dk * 4 + 2 * nc * dk * dk * 4 + 8 * t * dk * 4
    return pl.pallas_call(
        functools.partial(_ret_body, chunk=c, nc=nc, kscale=dk ** -0.5, eps=cfg.eps,
                          group=math.gcd(cfg.ret_group, nc)),
        out_shape=jax.ShapeDtypeStruct((b * t, h * dk), BF16),
        grid=(b, h),
        in_specs=[blk(0), blk(h), blk(2 * h), blk(3 * h),
                  pl.BlockSpec((t, dk), lambda bi, hi: (0, 0)),
                  pl.BlockSpec((t, dk), lambda bi, hi: (0, 0)),
                  pl.BlockSpec((1, c, c), lambda bi, hi: (hi, 0, 0)),
                  pl.BlockSpec((1, 6, c, dk), lambda bi, hi: (hi, 0, 0, 0)),
                  pl.BlockSpec((1, dk), lambda bi, hi: (0, hi))],
        out_specs=pl.BlockSpec((t, dk), lambda bi, hi: (bi, hi)),
        scratch_shapes=[pltpu.VMEM((t, dk), F32), pltpu.VMEM((t, dk), F32),
                        pltpu.VMEM((nc, dk, dk), F32), pltpu.VMEM((nc, dk, dk), F32)],
        compiler_params=pltpu.CompilerParams(dimension_semantics=("parallel", "parallel"),
                                             vmem_limit_bytes=_vmem_limit(est)),
        name="retention",
    )(zq, zq, zq, zq, cos2, sin2, dmat, tab, ret_gn.reshape(1, h * dk).astype(F32))


def _s5_tables(lam_re, lam_im, log_dt, b_re, b_im, c_re, c_im, d_skip, glu_w, glu_b, L, nsteps):
    lam = lax.complex(lam_re.astype(F32), lam_im.astype(F32))
    ldt = lam * jnp.exp(log_dt.astype(F32))[..., None]
    lam_bar = jnp.exp(ldt)
    bmat = lax.complex(b_re.astype(F32), b_im.astype(F32))
    b_bar = ((lam_bar - 1.0) / lam)[..., None] * bmat[None]
    cmat = lax.complex(c_re.astype(F32), c_im.astype(F32))
    g, p, cg = bmat.shape
    tau = jnp.arange(L + 1, dtype=F32)
    pw = jnp.exp(ldt[:, :, None, :] * tau[None, None, :, None])
    kern = jnp.real(jnp.einsum('dgop,dgtp,dgpi->dgtoi', cmat, pw[:, :, :L], b_bar))
    li = np.arange(L)
    lag = li[None, :] - li[:, None]
    k_f = jnp.where((lag >= 0)[None, :, :, None, None], kern[0][:, np.clip(lag, 0, L - 1)], 0.0)
    k_b = jnp.where((lag <= 0)[None, :, :, None, None], kern[1][:, np.clip(-lag, 0, L - 1)], 0.0)
    m = (k_f + k_b).transpose(0, 1, 4, 2, 3).reshape(g, L * cg, L * cg)

    def cat(z):
        return jnp.concatenate([jnp.real(z), jnp.imag(z)], axis=-1)

    inc_f = cat(jnp.einsum('glp,gpi->glip', pw[0][:, L - 1 - li], b_bar[0])).reshape(g, L * cg, 2 * p)
    inc_b = cat(jnp.einsum('glp,gpi->glip', pw[1][:, li], b_bar[1])).reshape(g, L * cg, 2 * p)

    def out_mat(z):
        return jnp.concatenate([jnp.real(z), -jnp.imag(z)], axis=1).reshape(g, 2 * p, L * cg)

    out_f = out_mat(jnp.einsum('gop,glp->gplo', cmat[0], pw[0][:, li + 1]))
    out_b = out_mat(jnp.einsum('gop,glp->gplo', cmat[1], pw[1][:, L - li]))
    eye = jnp.eye(L, dtype=F32)
    glu = jnp.einsum('lm,gce->glcme', eye, glu_w.astype(F32)).reshape(g, L * cg, L * cg)
    vecs = jnp.stack([jnp.tile(d_skip.astype(F32), (1, L)), jnp.tile(glu_b.astype(F32), (1, L))], axis=1)
    steps = (2.0 ** jnp.arange(nsteps, dtype=F32)) * L
    a = jnp.exp(ldt[:, :, None, :] * steps[None, None, :, None])
    scan = jnp.stack([jnp.concatenate([jnp.real(a), jnp.real(a)], -1),
                      jnp.concatenate([-jnp.imag(a), jnp.imag(a)], -1)], axis=3)
    scan = scan.transpose(1, 0, 2, 3, 4).reshape(g, 2 * nsteps * 2, 2 * p)
    return (m.astype(BF16), inc_f.astype(BF16), inc_b.astype(BF16), out_f.astype(BF16),
            out_b.astype(BF16), glu.astype(BF16), vecs, scan)


def _s5_body(u_ref, m_ref, incf_ref, incb_ref, outf_ref, outb_ref, glu_ref, vec_ref, scan_ref, o_ref,
             *, ncs, nsteps, L):
    rows = u_ref.shape[0] // L
    lanes = u_ref.shape[1]
    u = jnp.concatenate([u_ref[pl.ds(l, rows, stride=L), :] for l in range(L)], axis=1)
    ub = u.astype(BF16)

    def swap(x):
        h = x.shape[1] // 2
        return jnp.concatenate([x[:, h:], x[:, :h]], axis=1)

    y = jnp.dot(ub, m_ref[0], preferred_element_type=F32)
    xf = jnp.dot(ub, incf_ref[0], preferred_element_type=F32)
    xb = jnp.dot(ub, incb_ref[0], preferred_element_type=F32)
    cidx = lax.rem(lax.broadcasted_iota(I32, (rows, 1), 0), ncs)
    for kk in range(nsteps):
        s = 1 << kk
        a_f, b_f = scan_ref[0, pl.ds(2 * kk, 1), :], scan_ref[0, pl.ds(2 * kk + 1, 1), :]
        a_b = scan_ref[0, pl.ds(2 * nsteps + 2 * kk, 1), :]
        b_b = scan_ref[0, pl.ds(2 * nsteps + 2 * kk + 1, 1), :]
        pf = pltpu.roll(xf, s, 0)
        xf = xf + jnp.where(cidx >= s, a_f * pf + b_f * swap(pf), 0.0)
        pb = pltpu.roll(xb, rows - s, 0)
        xb = xb + jnp.where(cidx < ncs - s, a_b * pb + b_b * swap(pb), 0.0)
    x_prev = jnp.where(cidx >= 1, pltpu.roll(xf, 1, 0), 0.0)
    x_next = jnp.where(cidx < ncs - 1, pltpu.roll(xb, rows - 1, 0), 0.0)
    y = y + jnp.dot(x_prev.astype(BF16), outf_ref[0], preferred_element_type=F32)
    y = y + jnp.dot(x_next.astype(BF16), outb_ref[0], preferred_element_type=F32)
    y = y + u * vec_ref[0, pl.ds(0, 1), :]
    yg = _gelu(y)
    z = jnp.dot(yg.astype(BF16), glu_ref[0], preferred_element_type=F32) + vec_ref[0, pl.ds(1, 1), :]
    out = yg * _sigmoid(z)
    for l in range(L):
        o_ref[pl.ds(l, rows, stride=L), :] = out[:, l * lanes:(l + 1) * lanes]


def _s5_slab_tables(tabs, gs, L, cg):
    m, inc_f, inc_b, out_f, out_b, glu, vecs, scan = tabs
    g = m.shape[0]
    s = g // gs
    p2 = inc_f.shape[2]
    eye = jnp.eye(gs, dtype=m.dtype)

    def tok_tok(x):
        x6 = x.reshape(s, gs, L, cg, L, cg)
        return jnp.einsum('sglcmd,gh->slgcmhd', x6, eye).reshape(s, L * gs * cg, L * gs * cg)

    ps = p2 // 2

    def tok_state(x):
        x6 = x.reshape(s, gs, L, cg, 2, ps)
        return jnp.einsum('sglcrp,gh->slgcrhp', x6, eye).reshape(s, L * gs * cg, gs * p2)

    def state_tok(x):
        x6 = x.reshape(s, gs, 2, ps, L, cg)
        return jnp.einsum('sgrplc,gh->srgplhc', x6, eye).reshape(s, gs * p2, L * gs * cg)

    vecs_s = vecs.reshape(s, gs, 2, L, cg).transpose(0, 2, 3, 1, 4).reshape(s, 2, L * gs * cg)
    nrow = scan.shape[1]
    scan_s = scan.reshape(s, gs, nrow, 2, ps).transpose(0, 2, 3, 1, 4).reshape(s, nrow, gs * p2)
    return (tok_tok(m), tok_state(inc_f), tok_state(inc_b), state_tok(out_f), state_tok(out_b), tok_tok(glu),
            vecs_s, scan_s)


def _s5(u, params, b, t, cfg):
    lam_re, lam_im, log_dt, b_re, b_im, c_re, c_im, d_skip, glu_w, glu_b = params
    L, cg = cfg.s5_chunk, cfg.s5_group
    n, ch = u.shape
    gs = V7X_LANES // cg
    assert t % L == 0 and ch % V7X_LANES == 0 and V7X_LANES % cg == 0
    ncs = t // L
    nsteps = max(1, (ncs - 1).bit_length())
    p2 = 2 * cfg.s5_state
    tabs = _s5_slab_tables(
        _s5_tables(lam_re, lam_im, log_dt, b_re, b_im, c_re, c_im, d_skip, glu_w, glu_b, L, nsteps), gs, L, cg)
    seqs = max(1, min(b, cfg.s5_rows // ncs))
    while b % seqs:
        seqs -= 1
    rows = seqs * ncs
    w = L * V7X_LANES
    ws = gs * p2
    per_s = lambda shape: pl.BlockSpec((1,) + shape, lambda si, ri: (si, 0, 0))
    est = 2 * 2 * rows * w * 4 + 2 * 2 * (2 * w * w + 4 * w * ws) + 10 * rows * w * 4 + 8 * rows * ws * 4
    tok = pl.BlockSpec((rows * L, V7X_LANES), lambda si, ri: (ri, si))
    return pl.pallas_call(
        functools.partial(_s5_body, ncs=ncs, nsteps=nsteps, L=L),
        out_shape=jax.ShapeDtypeStruct((n, ch), F32),
        grid=(ch // V7X_LANES, n // (rows * L)),
        in_specs=[tok, per_s((w, w)), per_s((w, ws)), per_s((w, ws)), per_s((ws, w)), per_s((ws, w)),
                  per_s((w, w)), per_s((2, w)), per_s((4 * nsteps, ws))],
        out_specs=tok,
        compiler_params=pltpu.CompilerParams(dimension_semantics=("parallel", "parallel"),
                                             vmem_limit_bytes=_vmem_limit(est)),
        name="s5",
    )(u, *tabs)


def _na_bias_tables(rpb, cfg, rows):
    kh, kw, w = cfg.na_kh, cfg.na_kw, cfg.grid_w
    j = np.arange(w)
    c_start = np.clip(j - kw // 2, 0, w - kw)
    c = np.arange(w)
    inside = (c[None, :] >= c_start[:, None]) & (c[None, :] < c_start[:, None] + kw)
    by_row = jnp.stack([rpb.astype(F32)[:, kh - 1 - dl:2 * kh - 1 - dl] for dl in range(kh)], axis=1)
    padded = jnp.pad(by_row, ((0, 0), (0, 0), (0, 0), (w - kw, w - kw)))
    bias = jnp.stack([padded[..., w - 1 - jq:2 * w - 1 - jq] for jq in range(w)], axis=2)
    bias = jnp.where(inside[None, None, :, None, :], bias, NEG_BIG)
    return bias.reshape(rpb.shape[0], kh, w, kh * w)


def _na_body(q_ref, kn, vb, tab_ref, o_ref, qn, *, w, rows, kh, dh, unroll):
    lo = lax.broadcasted_iota(I32, (1, 2 * dh), 1) < dh
    qh = q_ref[...]
    qn[0] = jnp.where(lo, qh, jnp.zeros_like(qh))
    qn[1] = jnp.where(lo, jnp.zeros_like(qh), qh)
    contract1 = (((1,), (1,)), ((), ()))

    def group(gi, carry):
        chains = []
        for u in range(unroll):
            r = gi * unroll + u
            rs = jnp.clip(r - kh // 2, 0, rows - kh)
            q0 = pl.multiple_of(r * w, w)
            k0 = pl.multiple_of(rs * w, w)
            for hh in range(2):
                s = lax.dot_general(qn[hh, pl.ds(q0, w), :], kn[pl.ds(k0, kh * w), :], contract1,
                                    preferred_element_type=F32)
                chains.append((s + tab_ref[hh, r - rs], k0, q0))
        outs = []
        for s, k0, q0 in chains:
            p = jnp.exp(s - jnp.max(s, axis=-1, keepdims=True))
            p = p / jnp.sum(p, axis=-1, keepdims=True)
            outs.append(jnp.dot(p.astype(BF16), vb[pl.ds(k0, kh * w), :], preferred_element_type=F32))
        for u in range(unroll):
            q0 = chains[2 * u][2]
            o_ref[pl.ds(q0, w), :] = jnp.where(lo, outs[2 * u], outs[2 * u + 1]).astype(o_ref.dtype)
        return carry

    lax.fori_loop(0, rows // unroll, group, 0)


def _qkv_body(x_ref, g_ref, w_ref, qg_ref, kg_ref, o_ref, *, eps, dh, width):
    x = x_ref[...]
    y = x * lax.rsqrt(jnp.mean(x * x, axis=-1, keepdims=True) + eps) * g_ref[...]
    z = jnp.dot(y.astype(BF16), w_ref[...], preferred_element_type=F32)
    lanes = V7X_MXU_WIDTH
    shift = dh.bit_length() - 1
    same_head = (lax.shift_right_logical(lax.broadcasted_iota(I32, (lanes, lanes), 0), shift)
                 == lax.shift_right_logical(lax.broadcasted_iota(I32, (lanes, lanes), 1), shift))
    head_mean = jnp.where(same_head, 1.0 / dh, 0.0).astype(BF16)

    def head_norm(c0, gain_ref, scale):
        for c in range(0, width, lanes):
            xc = z[:, c0 + c:c0 + c + lanes]
            x2 = xc * xc
            hi = x2.astype(BF16)
            rest = (x2 - hi.astype(F32)).astype(BF16)
            ms = (jnp.dot(hi, head_mean, preferred_element_type=F32)
                  + jnp.dot(rest, head_mean, preferred_element_type=F32))
            out = xc * lax.rsqrt(ms + eps) * gain_ref[:, c:c + lanes]
            o_ref[:, c0 + c:c0 + c + lanes] = (out * scale if scale != 1.0 else out).astype(o_ref.dtype)

    head_norm(0, qg_ref, dh ** -0.5)
    head_norm(width, kg_ref, 1.0)
    o_ref[:, 2 * width:] = z[:, 2 * width:].astype(o_ref.dtype)


def _qkv_projection(x2, gain, w_bf16, q_gain, k_gain, cfg):
    n, d = x2.shape
    nout = w_bf16.shape[1]
    width = nout // 3
    dh = width // cfg.na_heads
    tm = min(cfg.tm, n)
    assert dh & (dh - 1) == 0 and V7X_MXU_WIDTH % dh == 0 and width % V7X_MXU_WIDTH == 0
    est = 2 * (tm * d * 4 + d * nout * 2 + tm * nout * 2) + 3 * tm * nout * 4
    row = lambda a: a.reshape(1, width).astype(F32)
    return pl.pallas_call(
        functools.partial(_qkv_body, eps=cfg.eps, dh=dh, width=width),
        out_shape=jax.ShapeDtypeStruct((n, nout), BF16),
        grid=(n // tm,),
        in_specs=[pl.BlockSpec((tm, d), lambda i: (i, 0)),
                  pl.BlockSpec((1, d), lambda i: (0, 0)),
                  pl.BlockSpec((d, nout), lambda i: (0, 0)),
                  pl.BlockSpec((1, width), lambda i: (0, 0)),
                  pl.BlockSpec((1, width), lambda i: (0, 0))],
        out_specs=pl.BlockSpec((tm, nout), lambda i: (i, 0)),
        compiler_params=pltpu.CompilerParams(dimension_semantics=("parallel",),
                                             vmem_limit_bytes=_vmem_limit(est)),
        name="qkv_projection",
    )(x2, gain.reshape(1, d).astype(F32), w_bf16, row(q_gain), row(k_gain))


def _neighbourhood_attention(z, rpb, b, t, cfg):
    h, w, kh = cfg.na_heads, cfg.grid_w, cfg.na_kh
    dh = z.shape[1] // (3 * h)
    rows = t // w
    assert t % w == 0 and rows >= kh and h % 2 == 0 and 2 * dh == V7X_LANES
    tab = _na_bias_tables(rpb, cfg, rows)
    hp = h // 2
    blk = lambda off: pl.BlockSpec((t, 2 * dh), lambda bi, pi, off=off: (bi, off + pi))
    est = 2 * (3 * t * 2 * dh * 2 + 2 * kh * w * kh * w * 4 + t * 2 * dh * 2) + 2 * t * 2 * dh * 2 \
        + 6 * t * 2 * dh * 4
    return pl.pallas_call(
        functools.partial(_na_body, w=w, rows=rows, kh=kh, dh=dh, unroll=math.gcd(cfg.na_unroll, rows)),
        out_shape=jax.ShapeDtypeStruct((b * t, h * dh), BF16),
        grid=(b, hp),
        in_specs=[blk(0), blk(hp), blk(2 * hp),
                  pl.BlockSpec((2, kh, w, kh * w), lambda bi, pi: (pi, 0, 0, 0))],
        out_specs=pl.BlockSpec((t, 2 * dh), lambda bi, pi: (bi, pi)),
        scratch_shapes=[pltpu.VMEM((2, t, 2 * dh), BF16)],
        compiler_params=pltpu.CompilerParams(dimension_semantics=("parallel", "parallel"),
                                             vmem_limit_bytes=_vmem_limit(est)),
        name="neighbourhood_attention",
    )(z, z, z, tab)


def _peer_cells(topk):
    return [(k1, k2) for k1 in range(topk) for k2 in range(topk) if (k1 + 1) * (k2 + 1) <= topk]


W_PITCH_PAD = 8
HIGH_HALF = np.uint32(0xFFFF0000)
HALF_ULP_BF16 = np.uint32(0x8000)


class _RouteScratch(NamedTuple):
    s: object
    ts: object
    ti: object
    c_s: object
    c_e: object
    best: object
    e_t: object
    g_t: object


def _route_scores(qp_ref, keys_ref, rs):
    dq = keys_ref.shape[2]
    contract1 = (((1,), (1,)), ((), ()))
    for p in range(2):
        rs.s[p] = lax.dot_general(keys_ref[p], qp_ref[:, p * dq:(p + 1) * dq], contract1,
                                  preferred_element_type=F32)


def _route_first_stage(rs, it0, n_it):
    nkeys, th = rs.s.shape[1:]
    iota = lax.broadcasted_iota(I32, (nkeys, th), 0).astype(F32)
    for p in range(2):
        s = rs.s[p]
        for k in range(n_it):
            m = jnp.max(s, axis=0, keepdims=True)
            idx = jnp.min(jnp.where(s == m, iota, float(nkeys)), axis=0, keepdims=True)
            rs.ts[p, pl.ds(it0 + k, 1), :] = m
            rs.ti[p, pl.ds(it0 + k, 1), :] = idx
            s = jnp.where(iota == idx, -jnp.inf, s)
        rs.s[p] = s


def _route_cells(rs, cells):
    crow, th = rs.c_s.shape
    nkeys = rs.s.shape[1]
    ncell = len(cells)
    rs.c_s[pl.ds(ncell, crow - ncell), :] = jnp.full((crow - ncell, th), -jnp.inf, F32)
    rs.c_e[pl.ds(ncell, crow - ncell), :] = jnp.zeros((crow - ncell, th), F32)
    for ci, (k1, k2) in enumerate(cells):
        rs.c_s[pl.ds(ci, 1), :] = rs.ts[0, pl.ds(k1, 1), :] + rs.ts[1, pl.ds(k2, 1), :]
        rs.c_e[pl.ds(ci, 1), :] = rs.ti[0, pl.ds(k1, 1), :] * float(nkeys) + rs.ti[1, pl.ds(k2, 1), :]


def _route_second_stage(rs, it0, n_it, blk0, row0):
    crow, th = rs.c_s.shape
    iota = lax.broadcasted_iota(I32, (crow, th), 0).astype(F32)
    cs, ce = rs.c_s[...], rs.c_e[...]
    for k in range(n_it):
        m = jnp.max(cs, axis=0, keepdims=True)
        pos = jnp.min(jnp.where(cs == m, iota, float(crow)), axis=0, keepdims=True)
        hit = iota == pos
        e_row = jnp.sum(jnp.where(hit, ce, 0.0), axis=0, keepdims=True)
        for c in range(th // V7X_LANES):
            rs.e_t[blk0 + c, pl.ds(row0 + it0 + k, 1), :] = e_row[:, c * V7X_LANES:(c + 1) * V7X_LANES]
        rs.best[pl.ds(it0 + k, 1), :] = m
        cs = jnp.where(hit, -jnp.inf, cs)
    rs.c_s[...] = cs


def _route_gates(rs, blk0, row0):
    topk, th = rs.best.shape
    best = rs.best[...]
    e = jnp.exp(best - best[0:1, :])
    gates = e / jnp.sum(e, axis=0, keepdims=True)
    for c in range(th // V7X_LANES):
        rs.g_t[blk0 + c, pl.ds(row0, topk), :] = gates[:, c * V7X_LANES:(c + 1) * V7X_LANES]


PEER_SUB = 2 * V7X_LANES


def _peer_pipe_body(hn_ref, qp_ref, keys_ref, ut_ref, v_ref, x_ref, o_ref,
                    wmap, acc, pmat, e_cur, g_cur, s_scr, ts, ti, c_s, c_e, best, e_t, g_t,
                    *, nk, pitch, unroll, heads, topk, cells, nhalf, nsteps):
    r = pl.program_id(0)
    j = pl.program_id(1)
    tm = hn_ref.shape[0]
    th = qp_ref.shape[0]
    nsub, _, sub = ut_ref.shape
    nout = acc.shape[0]
    half = nk // 2
    ncol = th // V7X_LANES
    it_per = topk // nsub
    nj = heads * topk
    rs = _RouteScratch(s_scr, ts, ti, c_s, c_e, best, e_t, g_t)
    contract1 = (((1,), (1,)), ((), ()))

    @pl.when(j == 0)
    def _():
        @pl.when(r == 0)
        def _():
            for ref in (acc, pmat, e_cur, g_cur, ts, ti):
                ref[...] = jnp.zeros_like(ref)

        @pl.when(r > 0)
        def _():
            blk0 = ((nsteps - 1) % nhalf) * ncol
            row0 = ((nsteps - 1) // nhalf) * topk
            _route_cells(rs, cells)
            _route_second_stage(rs, 0, topk, blk0, row0)
            _route_gates(rs, blk0, row0)
            for cb in range(tm // V7X_LANES):
                rows = slice(cb * V7X_LANES, (cb + 1) * V7X_LANES)
                e_cur[rows, :] = e_t[cb, 0:nj, :].T
                g_cur[rows, :] = g_t[cb, 0:nj, :].T

        irow = lax.broadcasted_iota(I32, (nk, nj), 0)
        row = irow.astype(F32)
        a_of_row = jnp.where(irow < half, 2 * irow, 2 * (irow - half) + 1).astype(F32)

        def build(n, carry):
            er = e_cur[pl.ds(n, 1), :]
            ar = jnp.floor(er * (1.0 / nk))
            br = er - ar * nk
            gr = g_cur[pl.ds(n, 1), :] * 0.5
            one_a = jnp.where(a_of_row == ar, 1.0, 0.0).astype(BF16)
            gate_b = jnp.where(row == br, gr, 0.0).astype(BF16)
            wn = lax.dot_general(one_a, gate_b, contract1, preferred_element_type=F32)
            lo = pltpu.bitcast(wn[:half], jnp.uint32)
            hi = pltpu.bitcast(wn[half:], jnp.uint32)
            packed = ((hi + HALF_ULP_BF16) & HIGH_HALF) | ((lo + HALF_ULP_BF16) >> 16)
            wmap[pl.ds(pl.multiple_of(n * pitch, V7X_SUBLANES), half), :] = packed
            return carry

        lax.fori_loop(0, tm, build, 0, unroll=unroll)

    ju = jnp.maximum(j - 1, 0)
    c_blk0 = lax.rem(ju, nhalf) * ncol
    c_row0 = pl.multiple_of(jnp.where(j == 0, nj, lax.div(ju, nhalf) * topk), topk)
    _route_scores(qp_ref, keys_ref, rs)
    _route_cells(rs, cells)
    cur = lax.rem(j, 2)
    prev = 1 - cur

    def trip(sb, carry):
        ob = lax.rem(sb, nout)
        k0 = lax.div(sb, nout) * nout
        part = acc[ob]
        for k in range(nout):
            rows = pl.ds(pl.multiple_of((k0 + k) * sub, sub), sub)
            part = part + jnp.dot(pmat[prev * nsub + k0 + k], v_ref[ob, rows, :], preferred_element_type=F32)
        acc[ob] = part
        xv = jnp.dot(hn_ref[...], ut_ref[sb], preferred_element_type=F32)
        act = xv * (1.0 + lax.erf(xv * (1.0 / math.sqrt(2.0))))
        parts = []
        for i in range(sub // (2 * nk)):
            w32 = wmap[pl.ds((j * nsub + sb) * (sub // (2 * nk)) + i, tm, stride=pitch), :]
            parts.append(act[:, 2 * i * nk:(2 * i + 1) * nk] * pltpu.bitcast(w32 << 16, F32))
            parts.append(act[:, (2 * i + 1) * nk:(2 * i + 2) * nk] * pltpu.bitcast(w32 & HIGH_HALF, F32))
        pmat[cur * nsub + sb] = jnp.concatenate(parts, axis=1).astype(BF16)
        _route_first_stage(rs, sb * it_per, it_per)
        _route_second_stage(rs, sb * it_per, it_per, c_blk0, c_row0)
        return carry

    lax.fori_loop(0, nsub, trip, 0)
    _route_gates(rs, c_blk0, c_row0)

    @pl.when(j == 0)
    def _():
        for k in range(nout):
            o_ref[:, k * sub:(k + 1) * sub] = x_ref[:, k * sub:(k + 1) * sub] + acc[k]
        acc[...] = jnp.zeros_like(acc)


def _peer_fused(hn, qp, sub_keys, u3, v4, x2, cfg):
    n, d = x2.shape
    nexp = u3.shape[0] * u3.shape[2]
    nk, heads, topk = cfg.peer_nkeys, cfg.peer_heads, cfg.peer_topk
    dq = sub_keys.shape[2]
    nj = heads * topk
    sub = PEER_SUB
    nout = d // sub
    te = max(d, min(cfg.peer_te, nexp) // d * d)
    nsub = te // sub
    nsteps = nexp // te
    assert nk == V7X_LANES and nexp == nk * nk and sub % (2 * nk) == 0 and d % sub == 0 and nexp % te == 0
    assert te % d == 0 and nj == V7X_LANES and qp.shape[1] == heads * 2 * dq and topk % nsub == 0
    assert nsteps % heads == 0 and nsteps % 2 == 0
    nhalf = nsteps // heads
    tm = min(cfg.peer_tm, n)
    tm = max(tm, nhalf * V7X_LANES)
    assert n % tm == 0 and tm % nhalf == 0
    th = tm // nhalf
    assert th % V7X_LANES == 0
    ntiles = n // tm
    pitch = nk // 2 + W_PITCH_PAD
    cells = _peer_cells(topk)
    crow = -(-(len(cells) + 1) // V7X_SUBLANES) * V7X_SUBLANES
    nblk = tm // V7X_LANES
    est = 2 * (tm * d * 2 + th * 2 * dq * 2 + 2 * d * te * 2 + 2 * tm * d * 4) \
        + tm * pitch * nk * 4 + tm * d * 4 + 2 * tm * te * 2 + 3 * tm * nj * 4 \
        + 3 * nblk * (nj + topk) * V7X_LANES * 4 + (3 * crow + 2 * nk + 5 * topk) * th * 4 + 6 * tm * sub * 4
    tile_in = pl.BlockSpec((tm, d), lambda r, j: (jnp.clip(r - 1, 0, ntiles - 1), 0))
    tile_out = pl.BlockSpec((tm, d), lambda r, j: (jnp.clip(jnp.where(j == 0, r - 2, r - 1), 0, ntiles - 1), 0))
    return pl.pallas_call(
        functools.partial(_peer_pipe_body, nk=nk, pitch=pitch, unroll=min(cfg.peer_unroll, tm), heads=heads,
                          topk=topk, cells=tuple(cells), nhalf=nhalf, nsteps=nsteps),
        out_shape=jax.ShapeDtypeStruct((n, d), F32),
        grid=(ntiles + 2, nsteps),
        in_specs=[tile_in,
                  pl.BlockSpec((th, 2 * dq),
                               lambda r, j: (nhalf * jnp.minimum(r, ntiles - 1) + j % nhalf, j // nhalf)),
                  pl.BlockSpec(sub_keys.shape, lambda r, j: (0, 0, 0)),
                  pl.BlockSpec((nsub, d, sub), lambda r, j: (j, 0, 0)),
                  pl.BlockSpec((nout, te, sub), lambda r, j: (0, (j + nsteps - 1) % nsteps, 0)),
                  tile_out],
        out_specs=tile_out,
        scratch_shapes=[pltpu.VMEM((tm * pitch, nk), jnp.uint32),
                        pltpu.VMEM((nout, tm, sub), F32), pltpu.VMEM((2 * nsub, tm, sub), BF16),
                        pltpu.VMEM((tm, nj), F32), pltpu.VMEM((tm, nj), F32),
                        pltpu.VMEM((2, nk, th), F32), pltpu.VMEM((2, topk, th), F32), pltpu.VMEM((2, topk, th), F32),
                        pltpu.VMEM((crow, th), F32), pltpu.VMEM((crow, th), F32),
                        pltpu.VMEM((topk, th), F32),
                        pltpu.VMEM((nblk, nj + topk, V7X_LANES), F32), pltpu.VMEM((nblk, nj + topk, V7X_LANES), F32)],
        compiler_params=pltpu.CompilerParams(dimension_semantics=("arbitrary", "arbitrary"),
                                             vmem_limit_bytes=_vmem_limit(est)),
        name="peer_fused",
    )(hn, qp, sub_keys.astype(BF16), u3, v4, x2)


def _peer(x2, gain, w_q_bf16, sub_keys, u3, v4, cfg):
    nq = w_q_bf16.shape[1]
    qp, hn = _norm_matmul(x2, gain, w_q_bf16, [(0, nq)], [BF16], cfg, emit_hn=True)
    return _peer_fused(hn, qp, sub_keys, u3, v4, x2, cfg)


def _trunk(x, p, cfg):
    b, t, d = x.shape
    x2 = x.reshape(b * t, d)
    depth = p["norm_mix"].shape[0]
    for l in range(depth):
        i = l // 2
        if l % 2 == 0:
            rw = 4 * cfg.ret_heads * cfg.ret_dk
            w_in = p["ab_w_in"][i]
            zq, u = _norm_matmul(x2, p["norm_mix"][l], w_in, [(0, rw), (rw, w_in.shape[1])], [F32, F32], cfg)
            ret = _retention(zq, p["ab_ret_decay"][i], p["ab_ret_gn"][i], b, t, cfg)
            ssm = _s5(u, tuple(p[k][i] for k in ("ab_s5_lam_re", "ab_s5_lam_im", "ab_s5_log_dt", "ab_s5_b_re",
                                                  "ab_s5_b_im", "ab_s5_c_re", "ab_s5_c_im", "ab_s5_d",
                                                  "ab_s5_glu_w", "ab_s5_glu_b")), b, t, cfg)
            w_out = p["ab_w_out"][i]
            nr = ret.shape[1]
            x2 = _matmul_residual([ret, ssm], [w_out[:nr], w_out[nr:]], x2, cfg)
        else:
            z = _qkv_projection(x2, p["norm_mix"][l], p["na_w_qkv"][i], p["na_q_gain"][i], p["na_k_gain"][i], cfg)
            att = _neighbourhood_attention(z, p["na_rpb"][i], b, t, cfg)
            x2 = _matmul_residual([att], [p["na_w_o"][i]], x2, cfg)
        x2 = _peer(x2, p["norm_ffn"][l], p["peer_w_q"][l], p["peer_sub_keys"][l], p["peer_u3"][l],
                   p["peer_v4"][l], cfg)
    return x2.reshape(b, t, d)


def _prepare(params):
    p = dict(params)
    for k in ("ab_w_in", "ab_w_out", "na_w_qkv", "na_w_o", "peer_w_q"):
        p[k] = params[k].astype(BF16)
    u, v = params["peer_u"].astype(BF16), params["peer_v"].astype(BF16)
    layers, nexp, d = u.shape
    p["peer_u3"] = u.reshape(layers, nexp // PEER_SUB, PEER_SUB, d).transpose(0, 1, 3, 2)
    p["peer_v4"] = v.reshape(layers, nexp, d // PEER_SUB, PEER_SUB).transpose(0, 2, 1, 3)
    return p


def _forward(x_prompt, x_sample, params, cfg=Cfg()):
    p = _prepare(params)
    return _trunk(x_prompt, p, cfg), _trunk(x_sample, p, cfg)


def kernel(x_prompt, x_sample, norm_mix, norm_ffn, ab_w_in, ab_ret_decay, ab_ret_gn, ab_s5_lam_re, ab_s5_lam_im, ab_s5_log_dt, ab_s5_b_re, ab_s5_b_im, ab_s5_c_re, ab_s5_c_im, ab_s5_d, ab_s5_glu_w, ab_s5_glu_b, ab_w_out, na_w_qkv, na_q_gain, na_k_gain, na_rpb, na_w_o, peer_w_q, peer_sub_keys, peer_u, peer_v):
    params = dict(norm_mix=norm_mix, norm_ffn=norm_ffn, ab_w_in=ab_w_in, ab_ret_decay=ab_ret_decay,
                  ab_ret_gn=ab_ret_gn, ab_s5_lam_re=ab_s5_lam_re, ab_s5_lam_im=ab_s5_lam_im,
                  ab_s5_log_dt=ab_s5_log_dt, ab_s5_b_re=ab_s5_b_re, ab_s5_b_im=ab_s5_b_im,
                  ab_s5_c_re=ab_s5_c_re, ab_s5_c_im=ab_s5_c_im, ab_s5_d=ab_s5_d, ab_s5_glu_w=ab_s5_glu_w,
                  ab_s5_glu_b=ab_s5_glu_b, ab_w_out=ab_w_out, na_w_qkv=na_w_qkv, na_q_gain=na_q_gain,
                  na_k_gain=na_k_gain, na_rpb=na_rpb, na_w_o=na_w_o, peer_w_q=peer_w_q,
                  peer_sub_keys=peer_sub_keys, peer_u=peer_u, peer_v=peer_v)
    return _forward(x_prompt, x_sample, params)
```

```python
import functools
import math
from typing import NamedTuple

import numpy as np
import jax
import jax.numpy as jnp
from jax import lax
from jax.experimental import pallas as pl
from jax.experimental.pallas import tpu as pltpu

F32 = jnp.float32
BF16 = jnp.bfloat16
I32 = jnp.int32

V7X_LANES = 128
V7X_SUBLANES = 8
V7X_MXU_WIDTH = 256
V7X_VMEM_BYTES = 64 * 2**20
VMEM_LIMIT_CAP = V7X_VMEM_BYTES - 8 * 2**20

NEG_BIG = -1e30


class Cfg(NamedTuple):
    eps: float = 1e-6
    grid_w: int = 64
    ret_heads: int = 4
    ret_dk: int = 128
    ret_chunk: int = 128
    rope_base: float = 10000.0
    s5_group: int = 16
    s5_state: int = 64
    s5_chunk: int = 8
    s5_rows: int = 512
    na_heads: int = 16
    na_kh: int = 8
    na_kw: int = 16
    peer_heads: int = 8
    peer_nkeys: int = 128
    peer_topk: int = 16
    tm: int = 512
    peer_tm: int = 512
    peer_unroll: int = 128
    na_unroll: int = 8
    ret_group: int = 8
    peer_te: int = 1024


def _vmem_limit(nbytes):
    return int(min(VMEM_LIMIT_CAP, max(32 * 2**20, nbytes)))


def _gelu(x):
    return 0.5 * x * (1.0 + lax.erf(x * (1.0 / math.sqrt(2.0))))


def _sigmoid(x):
    return 1.0 / (1.0 + jnp.exp(-x))


def _norm_mm_body(x_ref, g_ref, w_ref, *out_refs, splits, emit_hn, eps):
    x = x_ref[...]
    y = x * lax.rsqrt(jnp.mean(x * x, axis=-1, keepdims=True) + eps) * g_ref[...]
    yb = y.astype(BF16)
    z = jnp.dot(yb, w_ref[...], preferred_element_type=F32)
    for r, (s, e) in zip(out_refs, splits):
        r[...] = z[:, s:e].astype(r.dtype)
    if emit_hn:
        out_refs[len(splits)][...] = yb


def _norm_matmul(x2, gain, w_bf16, splits, dtypes, cfg, emit_hn=False):
    n, d = x2.shape
    nout = w_bf16.shape[1]
    tm = min(cfg.tm, n)
    out_shape = [jax.ShapeDtypeStruct((n, e - s), dt) for (s, e), dt in zip(splits, dtypes)]
    out_specs = [pl.BlockSpec((tm, e - s), lambda i: (i, 0)) for (s, e) in splits]
    if emit_hn:
        out_shape.append(jax.ShapeDtypeStruct((n, d), BF16))
        out_specs.append(pl.BlockSpec((tm, d), lambda i: (i, 0)))
    est = 2 * (tm * d * 4 + d * nout * 2 + tm * nout * 4 + tm * d * 2) + 2 * tm * nout * 4
    return pl.pallas_call(
        functools.partial(_norm_mm_body, splits=tuple(splits), emit_hn=emit_hn, eps=cfg.eps),
        out_shape=out_shape,
        grid=(n // tm,),
        in_specs=[pl.BlockSpec((tm, d), lambda i: (i, 0)),
                  pl.BlockSpec((1, d), lambda i: (0, 0)),
                  pl.BlockSpec((d, nout), lambda i: (0, 0))],
        out_specs=out_specs,
        compiler_params=pltpu.CompilerParams(dimension_semantics=("parallel",),
                                             vmem_limit_bytes=_vmem_limit(est)),
        name="norm_matmul",
    )(x2, gain.reshape(1, d).astype(F32), w_bf16)


def _mm_res_body(*refs, n_in):
    acc = refs[2 * n_in][...]
    for a, w in zip(refs[:n_in], refs[n_in:2 * n_in]):
        acc = acc + jnp.dot(a[...].astype(BF16), w[...], preferred_element_type=F32)
    refs[-1][...] = acc


def _matmul_residual(a_list, w_list, res, cfg):
    n, d = res.shape
    tm = min(cfg.tm, n)
    n_in = len(a_list)
    in_specs = ([pl.BlockSpec((tm, a.shape[1]), lambda i: (i, 0)) for a in a_list]
                + [pl.BlockSpec(w.shape, lambda i: (0, 0)) for w in w_list]
                + [pl.BlockSpec((tm, d), lambda i: (i, 0))])
    est = 2 * (sum(tm * a.shape[1] * a.dtype.itemsize for a in a_list) + sum(w.size * 2 for w in w_list)
               + 2 * tm * d * 4) + 2 * tm * d * 4
    return pl.pallas_call(
        functools.partial(_mm_res_body, n_in=n_in),
        out_shape=jax.ShapeDtypeStruct((n, d), F32),
        grid=(n // tm,),
        in_specs=in_specs,
        out_specs=pl.BlockSpec((tm, d), lambda i: (i, 0)),
        compiler_params=pltpu.CompilerParams(dimension_semantics=("parallel",),
                                             vmem_limit_bytes=_vmem_limit(est)),
        name="matmul_residual",
    )(*a_list, *w_list, res)


def _ret_tables(ret_decay, chunk, width):
    lg = -jax.nn.softplus(-ret_decay.astype(F32))
    pos = jnp.arange(chunk, dtype=F32)
    diff = pos[:, None] - pos[None, :]
    d_f = jnp.where(diff >= 0, jnp.exp(lg[0][:, None, None] * jnp.maximum(diff, 0.0)[None]), 0.0)
    d_b = jnp.where(diff < 0, jnp.exp(lg[1][:, None, None] * jnp.maximum(-diff, 0.0)[None]), 0.0)
    cols = [jnp.exp(lg[0][:, None] * (chunk - 1.0 - pos)[None]),
            jnp.exp(lg[1][:, None] * pos[None]),
            jnp.exp(lg[0][:, None] * (pos + 1.0)[None]),
            jnp.exp(lg[1][:, None] * (chunk - pos)[None]),
            jnp.broadcast_to(jnp.exp(lg[0] * chunk)[:, None], (lg.shape[1], chunk)),
            jnp.broadcast_to(jnp.exp(lg[1] * chunk)[:, None], (lg.shape[1], chunk))]
    tab = jnp.stack(cols, axis=1)
    return d_f + d_b, jnp.broadcast_to(tab[..., None], tab.shape + (width,))


def _rope_tables(t, half, base):
    inv = base ** (-jnp.arange(half, dtype=F32) / half)
    ang = jnp.arange(t, dtype=F32)[:, None] * inv[None, :]
    cos, sin = jnp.cos(ang), jnp.sin(ang)
    return jnp.concatenate([cos, cos], axis=1), jnp.concatenate([-sin, sin], axis=1)


def _ret_body(q_ref, k_ref, v_ref, g_ref, cos_ref, sin_ref, d_ref, tab_ref, gn_ref, o_ref,
              qs, ks, sf, sb, *, chunk, nc, kscale, eps, group):
    dk = q_ref.shape[1]
    half = dk // 2
    cos, sin = cos_ref[...], sin_ref[...]
    q = q_ref[...]
    qs[...] = q * cos + pltpu.roll(q, half, 1) * sin
    k = k_ref[...]
    ks[...] = (k * cos + pltpu.roll(k, half, 1) * sin) * kscale
    contract0 = (((0,), (0,)), ((), ()))
    contract1 = (((1,), (1,)), ((), ()))

    k_f, k_b, q_f, q_b = tab_ref[0, 0], tab_ref[0, 1], tab_ref[0, 2], tab_ref[0, 3]
    cd_f, cd_b = tab_ref[0, 4], tab_ref[0, 5]

    def increments(n, carry):
        r0 = pl.multiple_of(n * chunk, chunk)
        kc = ks[pl.ds(r0, chunk), :]
        vc = v_ref[pl.ds(r0, chunk), :].astype(BF16)
        sf[n] = lax.dot_general((kc * k_f).astype(BF16), vc, contract0, preferred_element_type=F32)
        sb[n] = lax.dot_general((kc * k_b).astype(BF16), vc, contract0, preferred_element_type=F32)
        return carry

    lax.fori_loop(0, nc, increments, 0, unroll=group)

    def sweep_f(n, state):
        inc = sf[n]
        sf[n] = state
        return state * cd_f + inc

    def sweep_b(i, state):
        n = nc - 1 - i
        inc = sb[n]
        sb[n] = state
        return state * cd_b + inc

    zero = jnp.zeros((dk, v_ref.shape[1]), F32)
    lax.fori_loop(0, nc, sweep_f, zero)
    lax.fori_loop(0, nc, sweep_b, zero)

    def outputs(gi, carry):
        first = []
        for u in range(group):
            n = gi * group + u
            r0 = pl.multiple_of(n * chunk, chunk)
            qc = qs[pl.ds(r0, chunk), :]
            s = lax.dot_general(qc.astype(BF16), ks[pl.ds(r0, chunk), :].astype(BF16), contract1,
                                preferred_element_type=F32)
            cross = jnp.dot((qc * q_f).astype(BF16), sf[n].astype(BF16), preferred_element_type=F32)
            cross = cross + jnp.dot((qc * q_b).astype(BF16), sb[n].astype(BF16), preferred_element_type=F32)
            first.append((r0, s, cross))
        for r0, s, cross in first:
            vc = v_ref[pl.ds(r0, chunk), :].astype(BF16)
            o = jnp.dot((s * d_ref[0]).astype(BF16), vc, preferred_element_type=F32) + cross
            oc = o - jnp.mean(o, axis=-1, keepdims=True)
            o = oc * lax.rsqrt(jnp.mean(oc * oc, axis=-1, keepdims=True) + eps)
            g = g_ref[pl.ds(r0, chunk), :]
            o_ref[pl.ds(r0, chunk), :] = (o * gn_ref[...] * (g * _sigmoid(g))).astype(o_ref.dtype)
        return carry

    lax.fori_loop(0, nc // group, outputs, 0)


def _retention(zq, ret_decay, ret_gn, b, t, cfg):
    h, dk, c = cfg.ret_heads, cfg.ret_dk, cfg.ret_chunk
    assert t % c == 0 and zq.shape[1] == 4 * h * dk
    nc = t // c
    dmat, tab = _ret_tables(ret_decay, c, dk)
    cos2, sin2 = _rope_tables(t, dk // 2, cfg.rope_base)
    blk = lambda off: pl.BlockSpec((t, dk), lambda bi, hi, off=off: (bi, off + hi))
    est = 2 * (4 * t * dk * 4 + 2 * t * dk * 4 + c * c * 4 + 6 * c * dk * 4 + t * dk * 2) \
        + 2 * t * dk * 4 + 2 * nc * dk * dk * 4 + 8 * t * dk * 4
    return pl.pallas_call(
        functools.partial(_ret_body, chunk=c, nc=nc, kscale=dk ** -0.5, eps=cfg.eps,
                          group=math.gcd(cfg.ret_group, nc)),
        out_shape=jax.ShapeDtypeStruct((b * t, h * dk), BF16),
        grid=(b, h),
        in_specs=[blk(0), blk(h), blk(2 * h), blk(3 * h),
                  pl.BlockSpec((t, dk), lambda bi, hi: (0, 0)),
                  pl.BlockSpec((t, dk), lambda bi, hi: (0, 0)),
                  pl.BlockSpec((1, c, c), lambda bi, hi: (hi, 0, 0)),
                  pl.BlockSpec((1, 6, c, dk), lambda bi, hi: (hi, 0, 0, 0)),
                  pl.BlockSpec((1, dk), lambda bi, hi: (0, hi))],
        out_specs=pl.BlockSpec((t, dk), lambda bi, hi: (bi, hi)),
        scratch_shapes=[pltpu.VMEM((t, dk), F32), pltpu.VMEM((t, dk), F32),
                        pltpu.VMEM((nc, dk, dk), F32), pltpu.VMEM((nc, dk, dk), F32)],
        compiler_params=pltpu.CompilerParams(dimension_semantics=("parallel", "parallel"),
                                             vmem_limit_bytes=_vmem_limit(est)),
        name="retention",
    )(zq, zq, zq, zq, cos2, sin2, dmat, tab, ret_gn.reshape(1, h * dk).astype(F32))


def _s5_tables(lam_re, lam_im, log_dt, b_re, b_im, c_re, c_im, d_skip, glu_w, glu_b, L, nsteps):
    lam = lax.complex(lam_re.astype(F32), lam_im.astype(F32))
    ldt = lam * jnp.exp(log_dt.astype(F32))[..., None]
    lam_bar = jnp.exp(ldt)
    bmat = lax.complex(b_re.astype(F32), b_im.astype(F32))
    b_bar = ((lam_bar - 1.0) / lam)[..., None] * bmat[None]
    cmat = lax.complex(c_re.astype(F32), c_im.astype(F32))
    g, p, cg = bmat.shape
    tau = jnp.arange(L + 1, dtype=F32)
    pw = jnp.exp(ldt[:, :, None, :] * tau[None, None, :, None])
    kern = jnp.real(jnp.einsum('dgop,dgtp,dgpi->dgtoi', cmat, pw[:, :, :L], b_bar))
    li = np.arange(L)
    lag = li[None, :] - li[:, None]
    k_f = jnp.where((lag >= 0)[None, :, :, None, None], kern[0][:, np.clip(lag, 0, L - 1)], 0.0)
    k_b = jnp.where((lag <= 0)[None, :, :, None, None], kern[1][:, np.clip(-lag, 0, L - 1)], 0.0)
    m = (k_f + k_b).transpose(0, 1, 4, 2, 3).reshape(g, L * cg, L * cg)

    def cat(z):
        return jnp.concatenate([jnp.real(z), jnp.imag(z)], axis=-1)

    inc_f = cat(jnp.einsum('glp,gpi->glip', pw[0][:, L - 1 - li], b_bar[0])).reshape(g, L * cg, 2 * p)
    inc_b = cat(jnp.einsum('glp,gpi->glip', pw[1][:, li], b_bar[1])).reshape(g, L * cg, 2 * p)

    def out_mat(z):
        return jnp.concatenate([jnp.real(z), -jnp.imag(z)], axis=1).reshape(g, 2 * p, L * cg)

    out_f = out_mat(jnp.einsum('gop,glp->gplo', cmat[0], pw[0][:, li + 1]))
    out_b = out_mat(jnp.einsum('gop,glp->gplo', cmat[1], pw[1][:, L - li]))
    eye = jnp.eye(L, dtype=F32)
    glu = jnp.einsum('lm,gce->glcme', eye, glu_w.astype(F32)).reshape(g, L * cg, L * cg)
    vecs = jnp.stack([jnp.tile(d_skip.astype(F32), (1, L)), jnp.tile(glu_b.astype(F32), (1, L))], axis=1)
    steps = (2.0 ** jnp.arange(nsteps, dtype=F32)) * L
    a = jnp.exp(ldt[:, :, None, :] * steps[None, None, :, None])
    scan = jnp.stack([jnp.concatenate([jnp.real(a), jnp.real(a)], -1),
                      jnp.concatenate([-jnp.imag(a), jnp.imag(a)], -1)], axis=3)
    scan = scan.transpose(1, 0, 2, 3, 4).reshape(g, 2 * nsteps * 2, 2 * p)
    return (m.astype(BF16), inc_f.astype(BF16), inc_b.astype(BF16), out_f.astype(BF16),
            out_b.astype(BF16), glu.astype(BF16), vecs, scan)


def _s5_body(u_ref, m_ref, incf_ref, incb_ref, outf_ref, outb_ref, glu_ref, vec_ref, scan_ref, o_ref,
             *, ncs, nsteps, L):
    rows = u_ref.shape[0] // L
    lanes = u_ref.shape[1]
    u = jnp.concatenate([u_ref[pl.ds(l, rows, stride=L), :] for l in range(L)], axis=1)
    ub = u.astype(BF16)

    y = jnp.dot(ub, m_ref[0], preferred_element_type=F32)
    xf = jnp.dot(ub, incf_ref[0], preferred_element_type=F32)
    xb = jnp.dot(ub, incb_ref[0], preferred_element_type=F32)
    cidx = lax.rem(lax.broadcasted_iota(I32, (rows, 1), 0), ncs)
    half = xf.shape[1] // 2

    def scan(x, base, forward):
        re_out, im_out = [], []
        for c0 in range(0, half, V7X_LANES):
            re, im = x[:, c0:c0 + V7X_LANES], x[:, half + c0:half + c0 + V7X_LANES]
            for kk in range(nsteps):
                s = 1 << kk
                a_re = scan_ref[0, pl.ds(base + 2 * kk, 1), c0:c0 + V7X_LANES]
                a_im = scan_ref[0, pl.ds(base + 2 * kk + 1, 1), half + c0:half + c0 + V7X_LANES]
                shift = s if forward else rows - s
                ok = (cidx >= s) if forward else (cidx < ncs - s)
                p_re, p_im = pltpu.roll(re, shift, 0), pltpu.roll(im, shift, 0)
                re = re + jnp.where(ok, a_re * p_re - a_im * p_im, 0.0)
                im = im + jnp.where(ok, a_re * p_im + a_im * p_re, 0.0)
            re_out.append(re)
            im_out.append(im)
        return jnp.concatenate(re_out + im_out, axis=1)

    xf = scan(xf, 0, True)
    xb = scan(xb, 2 * nsteps, False)
    x_prev = jnp.where(cidx >= 1, pltpu.roll(xf, 1, 0), 0.0)
    x_next = jnp.where(cidx < ncs - 1, pltpu.roll(xb, rows - 1, 0), 0.0)
    y = y + jnp.dot(x_prev.astype(BF16), outf_ref[0], preferred_element_type=F32)
    y = y + jnp.dot(x_next.astype(BF16), outb_ref[0], preferred_element_type=F32)
    y = y + u * vec_ref[0, pl.ds(0, 1), :]
    yg = _gelu(y)
    z = jnp.dot(yg.astype(BF16), glu_ref[0], preferred_element_type=F32) + vec_ref[0, pl.ds(1, 1), :]
    out = yg * _sigmoid(z)
    for l in range(L):
        o_ref[pl.ds(l, rows, stride=L), :] = out[:, l * lanes:(l + 1) * lanes]


def _s5_slab_tables(tabs, gs, L, cg):
    m, inc_f, inc_b, out_f, out_b, glu, vecs, scan = tabs
    g = m.shape[0]
    s = g // gs
    p2 = inc_f.shape[2]
    eye = jnp.eye(gs, dtype=m.dtype)

    def tok_tok(x):
        x6 = x.reshape(s, gs, L, cg, L, cg)
        return jnp.einsum('sglcmd,gh->slgcmhd', x6, eye).reshape(s, L * gs * cg, L * gs * cg)

    ps = p2 // 2

    def tok_state(x):
        x6 = x.reshape(s, gs, L, cg, 2, ps)
        return jnp.einsum('sglcrp,gh->slgcrhp', x6, eye).reshape(s, L * gs * cg, gs * p2)

    def state_tok(x):
        x6 = x.reshape(s, gs, 2, ps, L, cg)
        return jnp.einsum('sgrplc,gh->srgplhc', x6, eye).reshape(s, gs * p2, L * gs * cg)

    vecs_s = vecs.reshape(s, gs, 2, L, cg).transpose(0, 2, 3, 1, 4).reshape(s, 2, L * gs * cg)
    nrow = scan.shape[1]
    scan_s = scan.reshape(s, gs, nrow, 2, ps).transpose(0, 2, 3, 1, 4).reshape(s, nrow, gs * p2)
    return (tok_tok(m), tok_state(inc_f), tok_state(inc_b), state_tok(out_f), state_tok(out_b), tok_tok(glu),
            vecs_s, scan_s)


def _s5(u, params, b, t, cfg):
    lam_re, lam_im, log_dt, b_re, b_im, c_re, c_im, d_skip, glu_w, glu_b = params
    L, cg = cfg.s5_chunk, cfg.s5_group
    n, ch = u.shape
    gs = V7X_LANES // cg
    assert t % L == 0 and ch % V7X_LANES == 0 and V7X_LANES % cg == 0
    ncs = t // L
    nsteps = max(1, (ncs - 1).bit_length())
    p2 = 2 * cfg.s5_state
    tabs = _s5_slab_tables(
        _s5_tables(lam_re, lam_im, log_dt, b_re, b_im, c_re, c_im, d_skip, glu_w, glu_b, L, nsteps), gs, L, cg)
    seqs = max(1, min(b, cfg.s5_rows // ncs))
    while b % seqs:
        seqs -= 1
    rows = seqs * ncs
    w = L * V7X_LANES
    ws = gs * p2
    per_s = lambda shape: pl.BlockSpec((1,) + shape, lambda si, ri: (si, 0, 0))
    est = 2 * 2 * rows * w * 4 + 2 * 2 * (2 * w * w + 4 * w * ws) + 10 * rows * w * 4 + 8 * rows * ws * 4
    tok = pl.BlockSpec((rows * L, V7X_LANES), lambda si, ri: (ri, si))
    return pl.pallas_call(
        functools.partial(_s5_body, ncs=ncs, nsteps=nsteps, L=L),
        out_shape=jax.ShapeDtypeStruct((n, ch), F32),
        grid=(ch // V7X_LANES, n // (rows * L)),
        in_specs=[tok, per_s((w, w)), per_s((w, ws)), per_s((w, ws)), per_s((ws, w)), per_s((ws, w)),
                  per_s((w, w)), per_s((2, w)), per_s((4 * nsteps, ws))],
        out_specs=tok,
        compiler_params=pltpu.CompilerParams(dimension_semantics=("parallel", "parallel"),
                                             vmem_limit_bytes=_vmem_limit(est)),
        name="s5",
    )(u, *tabs)


def _na_bias_tables(rpb, cfg, rows):
    kh, kw, w = cfg.na_kh, cfg.na_kw, cfg.grid_w
    j = np.arange(w)
    c_start = np.clip(j - kw // 2, 0, w - kw)
    c = np.arange(w)
    inside = (c[None, :] >= c_start[:, None]) & (c[None, :] < c_start[:, None] + kw)
    by_row = jnp.stack([rpb.astype(F32)[:, kh - 1 - dl:2 * kh - 1 - dl] for dl in range(kh)], axis=1)
    padded = jnp.pad(by_row, ((0, 0), (0, 0), (0, 0), (w - kw, w - kw)))
    bias = jnp.stack([padded[..., w - 1 - jq:2 * w - 1 - jq] for jq in range(w)], axis=2)
    bias = jnp.where(inside[None, None, :, None, :], bias, NEG_BIG)
    return bias.reshape(rpb.shape[0], kh, w, kh * w)


def _na_body(q_ref, kn, vb, tab_ref, o_ref, qn, *, w, rows, kh, dh, unroll):
    lo = lax.broadcasted_iota(I32, (1, 2 * dh), 1) < dh
    qh = q_ref[...]
    qn[0] = jnp.where(lo, qh, jnp.zeros_like(qh))
    qn[1] = jnp.where(lo, jnp.zeros_like(qh), qh)
    contract1 = (((1,), (1,)), ((), ()))

    def group(gi, carry):
        chains = []
        for u in range(unroll):
            r = gi * unroll + u
            rs = jnp.clip(r - kh // 2, 0, rows - kh)
            q0 = pl.multiple_of(r * w, w)
            k0 = pl.multiple_of(rs * w, w)
            for hh in range(2):
                s = lax.dot_general(qn[hh, pl.ds(q0, w), :], kn[pl.ds(k0, kh * w), :], contract1,
                                    preferred_element_type=F32)
                chains.append((s + tab_ref[hh, r - rs], k0, q0))
        outs = []
        for s, k0, q0 in chains:
            p = jnp.exp(s - jnp.max(s, axis=-1, keepdims=True))
            p = p / jnp.sum(p, axis=-1, keepdims=True)
            outs.append(jnp.dot(p.astype(BF16), vb[pl.ds(k0, kh * w), :], preferred_element_type=F32))
        for u in range(unroll):
            q0 = chains[2 * u][2]
            o_ref[pl.ds(q0, w), :] = jnp.where(lo, outs[2 * u], outs[2 * u + 1]).astype(o_ref.dtype)
        return carry

    lax.fori_loop(0, rows // unroll, group, 0)


def _qkv_body(x_ref, g_ref, w_ref, qg_ref, kg_ref, o_ref, *, eps, dh, width):
    x = x_ref[...]
    y = x * lax.rsqrt(jnp.mean(x * x, axis=-1, keepdims=True) + eps) * g_ref[...]
    z = jnp.dot(y.astype(BF16), w_ref[...], preferred_element_type=F32)
    lanes = V7X_MXU_WIDTH
    shift = dh.bit_length() - 1
    same_head = (lax.shift_right_logical(lax.broadcasted_iota(I32, (lanes, lanes), 0), shift)
                 == lax.shift_right_logical(lax.broadcasted_iota(I32, (lanes, lanes), 1), shift))
    head_mean = jnp.where(same_head, 1.0 / dh, 0.0).astype(BF16)

    def head_norm(c0, gain_ref, scale):
        for c in range(0, width, lanes):
            xc = z[:, c0 + c:c0 + c + lanes]
            x2 = xc * xc
            hi = x2.astype(BF16)
            rest = (x2 - hi.astype(F32)).astype(BF16)
            ms = (jnp.dot(hi, head_mean, preferred_element_type=F32)
                  + jnp.dot(rest, head_mean, preferred_element_type=F32))
            out = xc * lax.rsqrt(ms + eps) * gain_ref[:, c:c + lanes]
            o_ref[:, c0 + c:c0 + c + lanes] = (out * scale if scale != 1.0 else out).astype(o_ref.dtype)

    head_norm(0, qg_ref, dh ** -0.5)
    head_norm(width, kg_ref, 1.0)
    o_ref[:, 2 * width:] = z[:, 2 * width:].astype(o_ref.dtype)


def _qkv_projection(x2, gain, w_bf16, q_gain, k_gain, cfg):
    n, d = x2.shape
    nout = w_bf16.shape[1]
    width = nout // 3
    dh = width // cfg.na_heads
    tm = min(cfg.tm, n)
    assert dh & (dh - 1) == 0 and V7X_MXU_WIDTH % dh == 0 and width % V7X_MXU_WIDTH == 0
    est = 2 * (tm * d * 4 + d * nout * 2 + tm * nout * 2) + 3 * tm * nout * 4
    row = lambda a: a.reshape(1, width).astype(F32)
    return pl.pallas_call(
        functools.partial(_qkv_body, eps=cfg.eps, dh=dh, width=width),
        out_shape=jax.ShapeDtypeStruct((n, nout), BF16),
        grid=(n // tm,),
        in_specs=[pl.BlockSpec((tm, d), lambda i: (i, 0)),
                  pl.BlockSpec((1, d), lambda i: (0, 0)),
                  pl.BlockSpec((d, nout), lambda i: (0, 0)),
                  pl.BlockSpec((1, width), lambda i: (0, 0)),
                  pl.BlockSpec((1, width), lambda i: (0, 0))],
        out_specs=pl.BlockSpec((tm, nout), lambda i: (i, 0)),
        compiler_params=pltpu.CompilerParams(dimension_semantics=("parallel",),
                                             vmem_limit_bytes=_vmem_limit(est)),
        name="qkv_projection",
    )(x2, gain.reshape(1, d).astype(F32), w_bf16, row(q_gain), row(k_gain))


def _neighbourhood_attention(z, rpb, b, t, cfg):
    h, w, kh = cfg.na_heads, cfg.grid_w, cfg.na_kh
    dh = z.shape[1] // (3 * h)
    rows = t // w
    assert t % w == 0 and rows >= kh and h % 2 == 0 and 2 * dh == V7X_LANES
    tab = _na_bias_tables(rpb, cfg, rows)
    hp = h // 2
    blk = lambda off: pl.BlockSpec((t, 2 * dh), lambda bi, pi, off=off: (bi, off + pi))
    est = 2 * (3 * t * 2 * dh * 2 + 2 * kh * w * kh * w * 4 + t * 2 * dh * 2) + 2 * t * 2 * dh * 2 \
        + 6 * t * 2 * dh * 4
    return pl.pallas_call(
        functools.partial(_na_body, w=w, rows=rows, kh=kh, dh=dh, unroll=math.gcd(cfg.na_unroll, rows)),
        out_shape=jax.ShapeDtypeStruct((b * t, h * dh), BF16),
        grid=(b, hp),
        in_specs=[blk(0), blk(hp), blk(2 * hp),
                  pl.BlockSpec((2, kh, w, kh * w), lambda bi, pi: (pi, 0, 0, 0))],
        out_specs=pl.BlockSpec((t, 2 * dh), lambda bi, pi: (bi, pi)),
        scratch_shapes=[pltpu.VMEM((2, t, 2 * dh), BF16)],
        compiler_params=pltpu.CompilerParams(dimension_semantics=("parallel", "parallel"),
                                             vmem_limit_bytes=_vmem_limit(est)),
        name="neighbourhood_attention",
    )(z, z, z, tab)


def _peer_cells(topk):
    return [(k1, k2) for k1 in range(topk) for k2 in range(topk) if (k1 + 1) * (k2 + 1) <= topk]


W_PITCH_PAD = 8
HIGH_HALF = np.uint32(0xFFFF0000)
HALF_ULP_BF16 = np.uint32(0x8000)


class _RouteScratch(NamedTuple):
    s: object
    ts: object
    ti: object
    c_s: object
    c_e: object
    best: object
    e_t: object
    g_t: object


def _route_scores(qp_ref, keys_ref, rs):
    dq = keys_ref.shape[2]
    contract1 = (((1,), (1,)), ((), ()))
    for p in range(2):
        rs.s[p] = lax.dot_general(keys_ref[p], qp_ref[:, p * dq:(p + 1) * dq], contract1,
                                  preferred_element_type=F32)


def _route_first_stage(rs, it0, n_it):
    nkeys, th = rs.s.shape[1:]
    iota = lax.broadcasted_iota(I32, (nkeys, th), 0).astype(F32)
    for p in range(2):
        s = rs.s[p]
        for k in range(n_it):
            m = jnp.max(s, axis=0, keepdims=True)
            idx = jnp.min(jnp.where(s == m, iota, float(nkeys)), axis=0, keepdims=True)
            rs.ts[p, pl.ds(it0 + k, 1), :] = m
            rs.ti[p, pl.ds(it0 + k, 1), :] = idx
            s = jnp.where(iota == idx, -jnp.inf, s)
        rs.s[p] = s


def _route_cells(rs, cells):
    crow, th = rs.c_s.shape
    nkeys = rs.s.shape[1]
    ncell = len(cells)
    rs.c_s[pl.ds(ncell, crow - ncell), :] = jnp.full((crow - ncell, th), -jnp.inf, F32)
    rs.c_e[pl.ds(ncell, crow - ncell), :] = jnp.zeros((crow - ncell, th), F32)
    for ci, (k1, k2) in enumerate(cells):
        rs.c_s[pl.ds(ci, 1), :] = rs.ts[0, pl.ds(k1, 1), :] + rs.ts[1, pl.ds(k2, 1), :]
        rs.c_e[pl.ds(ci, 1), :] = rs.ti[0, pl.ds(k1, 1), :] * float(nkeys) + rs.ti[1, pl.ds(k2, 1), :]


def _route_second_stage(rs, it0, n_it, blk0, row0):
    crow, th = rs.c_s.shape
    iota = lax.broadcasted_iota(I32, (crow, th), 0).astype(F32)
    cs, ce = rs.c_s[...], rs.c_e[...]
    for k in range(n_it):
        m = jnp.max(cs, axis=0, keepdims=True)
        pos = jnp.min(jnp.where(cs == m, iota, float(crow)), axis=0, keepdims=True)
        hit = iota == pos
        e_row = jnp.sum(jnp.where(hit, ce, 0.0), axis=0, keepdims=True)
        for c in range(th // V7X_LANES):
            rs.e_t[blk0 + c, pl.ds(row0 + it0 + k, 1), :] = e_row[:, c * V7X_LANES:(c + 1) * V7X_LANES]
        rs.best[pl.ds(it0 + k, 1), :] = m
        cs = jnp.where(hit, -jnp.inf, cs)
    rs.c_s[...] = cs


def _route_gates(rs, blk0, row0):
    topk, th = rs.best.shape
    best = rs.best[...]
    e = jnp.exp(best - best[0:1, :])
    gates = e / jnp.sum(e, axis=0, keepdims=True)
    for c in range(th // V7X_LANES):
        rs.g_t[blk0 + c, pl.ds(row0, topk), :] = gates[:, c * V7X_LANES:(c + 1) * V7X_LANES]


PEER_SUB = 2 * V7X_LANES


def _peer_pipe_body(hn_ref, qp_ref, keys_ref, ut_ref, v_ref, x_ref, o_ref,
                    wmap, acc, pmat, e_cur, g_cur, s_scr, ts, ti, c_s, c_e, best, e_t, g_t,
                    *, nk, pitch, unroll, heads, topk, cells, nhalf, nsteps):
    r = pl.program_id(0)
    j = pl.program_id(1)
    tm = hn_ref.shape[0]
    th = qp_ref.shape[0]
    nsub, _, sub = ut_ref.shape
    nout = acc.shape[0]
    half = nk // 2
    ncol = th // V7X_LANES
    it_per = topk // nsub
    nj = heads * topk
    rs = _RouteScratch(s_scr, ts, ti, c_s, c_e, best, e_t, g_t)
    contract1 = (((1,), (1,)), ((), ()))

    @pl.when(j == 0)
    def _():
        @pl.when(r == 0)
        def _():
            for ref in (acc, pmat, e_cur, g_cur, ts, ti):
                ref[...] = jnp.zeros_like(ref)

        @pl.when(r > 0)
        def _():
            blk0 = ((nsteps - 1) % nhalf) * ncol
            row0 = ((nsteps - 1) // nhalf) * topk
            _route_cells(rs, cells)
            _route_second_stage(rs, 0, topk, blk0, row0)
            _route_gates(rs, blk0, row0)
            for cb in range(tm // V7X_LANES):
                rows = slice(cb * V7X_LANES, (cb + 1) * V7X_LANES)
                e_cur[rows, :] = e_t[cb, 0:nj, :].T
                g_cur[rows, :] = g_t[cb, 0:nj, :].T

        irow = lax.broadcasted_iota(I32, (nk, nj), 0)
        row = irow.astype(F32)
        a_of_row = jnp.where(irow < half, 2 * irow, 2 * (irow - half) + 1).astype(F32)

        def build(n, carry):
            er = e_cur[pl.ds(n, 1), :]
            ar = jnp.floor(er * (1.0 / nk))
            br = er - ar * nk
            gr = g_cur[pl.ds(n, 1), :] * 0.5
            one_a = jnp.where(a_of_row == ar, 1.0, 0.0).astype(BF16)
            gate_b = jnp.where(row == br, gr, 0.0).astype(BF16)
            wn = lax.dot_general(one_a, gate_b, contract1, preferred_element_type=F32)
            lo = pltpu.bitcast(wn[:half], jnp.uint32)
            hi = pltpu.bitcast(wn[half:], jnp.uint32)
            packed = ((hi + HALF_ULP_BF16) & HIGH_HALF) | ((lo + HALF_ULP_BF16) >> 16)
            wmap[pl.ds(pl.multiple_of(n * pitch, V7X_SUBLANES), half), :] = packed
            return carry

        lax.fori_loop(0, tm, build, 0, unroll=unroll)

    ju = jnp.maximum(j - 1, 0)
    c_blk0 = lax.rem(ju, nhalf) * ncol
    c_row0 = pl.multiple_of(jnp.where(j == 0, nj, lax.div(ju, nhalf) * topk), topk)
    _route_scores(qp_ref, keys_ref, rs)
    _route_cells(rs, cells)
    cur = lax.rem(j, 2)
    prev = 1 - cur

    def trip(sb, carry):
        ob = lax.rem(sb, nout)
        k0 = lax.div(sb, nout) * nout
        part = acc[ob]
        for k in range(nout):
            rows = pl.ds(pl.multiple_of((k0 + k) * sub, sub), sub)
            part = part + jnp.dot(pmat[prev * nsub + k0 + k], v_ref[ob, rows, :], preferred_element_type=F32)
        acc[ob] = part
        xv = jnp.dot(hn_ref[...], ut_ref[sb], preferred_element_type=F32)
        act = xv * (1.0 + lax.erf(xv * (1.0 / math.sqrt(2.0))))
        parts = []
        for i in range(sub // (2 * nk)):
            w32 = wmap[pl.ds((j * nsub + sb) * (sub // (2 * nk)) + i, tm, stride=pitch), :]
            parts.append(act[:, 2 * i * nk:(2 * i + 1) * nk] * pltpu.bitcast(w32 << 16, F32))
            parts.append(act[:, (2 * i + 1) * nk:(2 * i + 2) * nk] * pltpu.bitcast(w32 & HIGH_HALF, F32))
        pmat[cur * nsub + sb] = jnp.concatenate(parts, axis=1).astype(BF16)
        _route_first_stage(rs, sb * it_per, it_per)
        _route_second_stage(rs, sb * it_per, it_per, c_blk0, c_row0)
        return carry

    lax.fori_loop(0, nsub, trip, 0)
    _route_gates(rs, c_blk0, c_row0)

    @pl.when(j == 0)
    def _():
        for k in range(nout):
            o_ref[:, k * sub:(k + 1) * sub] = x_ref[:, k * sub:(k + 1) * sub] + acc[k]
        acc[...] = jnp.zeros_like(acc)


def _peer_fused(hn, qp, sub_keys, u3, v4, x2, cfg):
    n, d = x2.shape
    nexp = u3.shape[0] * u3.shape[2]
    nk, heads, topk = cfg.peer_nkeys, cfg.peer_heads, cfg.peer_topk
    dq = sub_keys.shape[2]
    nj = heads * topk
    sub = PEER_SUB
    nout = d // sub
    te = max(d, min(cfg.peer_te, nexp) // d * d)
    nsub = te // sub
    nsteps = nexp // te
    assert nk == V7X_LANES and nexp == nk * nk and sub % (2 * nk) == 0 and d % sub == 0 and nexp % te == 0
    assert te % d == 0 and nj == V7X_LANES and qp.shape[1] == heads * 2 * dq and topk % nsub == 0
    assert nsteps % heads == 0 and nsteps % 2 == 0
    nhalf = nsteps // heads
    tm = min(cfg.peer_tm, n)
    tm = max(tm, nhalf * V7X_LANES)
    assert n % tm == 0 and tm % nhalf == 0
    th = tm // nhalf
    assert th % V7X_LANES == 0
    ntiles = n // tm
    pitch = nk // 2 + W_PITCH_PAD
    cells = _peer_cells(topk)
    crow = -(-(len(cells) + 1) // V7X_SUBLANES) * V7X_SUBLANES
    nblk = tm // V7X_LANES
    est = 2 * (tm * d * 2 + th * 2 * dq * 2 + 2 * d * te * 2 + 2 * tm * d * 4) \
        + tm * pitch * nk * 4 + tm * d * 4 + 2 * tm * te * 2 + 3 * tm * nj * 4 \
        + 3 * nblk * (nj + topk) * V7X_LANES * 4 + (3 * crow + 2 * nk + 5 * topk) * th * 4 + 6 * tm * sub * 4
    tile_in = pl.BlockSpec((tm, d), lambda r, j: (jnp.clip(r - 1, 0, ntiles - 1), 0))
    tile_out = pl.BlockSpec((tm, d), lambda r, j: (jnp.clip(jnp.where(j == 0, r - 2, r - 1), 0, ntiles - 1), 0))
    return pl.pallas_call(
        functools.partial(_peer_pipe_body, nk=nk, pitch=pitch, unroll=min(cfg.peer_unroll, tm), heads=heads,
                          topk=topk, cells=tuple(cells), nhalf=nhalf, nsteps=nsteps),
        out_shape=jax.ShapeDtypeStruct((n, d), F32),
        grid=(ntiles + 2, nsteps),
        in_specs=[tile_in,
                  pl.BlockSpec((th, 2 * dq),
                               lambda r, j: (nhalf * jnp.minimum(r, ntiles - 1) + j % nhalf, j // nhalf)),
                  pl.BlockSpec(sub_keys.shape, lambda r, j: (0, 0, 0)),
                  pl.BlockSpec((nsub, d, sub), lambda r, j: (j, 0, 0)),
                  pl.BlockSpec((nout, te, sub), lambda r, j: (0, (j + nsteps - 1) % nsteps, 0)),
                  tile_out],
        out_specs=tile_out,
        scratch_shapes=[pltpu.VMEM((tm * pitch, nk), jnp.uint32),
                        pltpu.VMEM((nout, tm, sub), F32), pltpu.VMEM((2 * nsub, tm, sub), BF16),
                        pltpu.VMEM((tm, nj), F32), pltpu.VMEM((tm, nj), F32),
                        pltpu.VMEM((2, nk, th), F32), pltpu.VMEM((2, topk, th), F32), pltpu.VMEM((2, topk, th), F32),
                        pltpu.VMEM((crow, th), F32), pltpu.VMEM((crow, th), F32),
                        pltpu.VMEM((topk, th), F32),
                        pltpu.VMEM((nblk, nj + topk, V7X_LANES), F32), pltpu.VMEM((nblk, nj + topk, V7X_LANES), F32)],
        compiler_params=pltpu.CompilerParams(dimension_semantics=("arbitrary", "arbitrary"),
                                             vmem_limit_bytes=_vmem_limit(est)),
        name="peer_fused",
    )(hn, qp, sub_keys.astype(BF16), u3, v4, x2)


def _peer(x2, gain, w_q_bf16, sub_keys, u3, v4, cfg):
    nq = w_q_bf16.shape[1]
    qp, hn = _norm_matmul(x2, gain, w_q_bf16, [(0, nq)], [BF16], cfg, emit_hn=True)
    return _peer_fused(hn, qp, sub_keys, u3, v4, x2, cfg)


def _trunk(x, p, cfg):
    b, t, d = x.shape
    x2 = x.reshape(b * t, d)
    depth = p["norm_mix"].shape[0]
    for l in range(depth):
        i = l // 2
        if l % 2 == 0:
            rw = 4 * cfg.ret_heads * cfg.ret_dk
            w_in = p["ab_w_in"][i]
            zq, u = _norm_matmul(x2, p["norm_mix"][l], w_in, [(0, rw), (rw, w_in.shape[1])], [F32, F32], cfg)
            ret = _retention(zq, p["ab_ret_decay"][i], p["ab_ret_gn"][i], b, t, cfg)
            ssm = _s5(u, tuple(p[k][i] for k in ("ab_s5_lam_re", "ab_s5_lam_im", "ab_s5_log_dt", "ab_s5_b_re",
                                                  "ab_s5_b_im", "ab_s5_c_re", "ab_s5_c_im", "ab_s5_d",
                                                  "ab_s5_glu_w", "ab_s5_glu_b")), b, t, cfg)
            w_out = p["ab_w_out"][i]
            nr = ret.shape[1]
            x2 = _matmul_residual([ret, ssm], [w_out[:nr], w_out[nr:]], x2, cfg)
        else:
            z = _qkv_projection(x2, p["norm_mix"][l], p["na_w_qkv"][i], p["na_q_gain"][i], p["na_k_gain"][i], cfg)
            att = _neighbourhood_attention(z, p["na_rpb"][i], b, t, cfg)
            x2 = _matmul_residual([att], [p["na_w_o"][i]], x2, cfg)
        x2 = _peer(x2, p["norm_ffn"][l], p["peer_w_q"][l], p["peer_sub_keys"][l], p["peer_u3"][l],
                   p["peer_v4"][l], cfg)
    return x2.reshape(b, t, d)


def _prepare(params):
    p = dict(params)
    for k in ("ab_w_in", "ab_w_out", "na_w_qkv", "na_w_o", "peer_w_q"):
        p[k] = params[k].astype(BF16)
    u, v = params["peer_u"].astype(BF16), params["peer_v"].astype(BF16)
    layers, nexp, d = u.shape
    p["peer_u3"] = u.reshape(layers, nexp // PEER_SUB, PEER_SUB, d).transpose(0, 1, 3, 2)
    p["peer_v4"] = v.reshape(layers, nexp, d // PEER_SUB, PEER_SUB).transpose(0, 2, 1, 3)
    return p


def _forward(x_prompt, x_sample, params, cfg=Cfg()):
    p = _prepare(params)
    return _trunk(x_prompt, p, cfg), _trunk(x_sample, p, cfg)


def kernel(x_prompt, x_sample, norm_mix, norm_ffn, ab_w_in, ab_ret_decay, ab_ret_gn, ab_s5_lam_re, ab_s5_lam_im, ab_s5_log_dt, ab_s5_b_re, ab_s5_b_im, ab_s5_c_re, ab_s5_c_im, ab_s5_d, ab_s5_glu_w, ab_s5_glu_b, ab_w_out, na_w_qkv, na_q_gain, na_k_gain, na_rpb, na_w_o, peer_w_q, peer_sub_keys, peer_u, peer_v):
    params = dict(norm_mix=norm_mix, norm_ffn=norm_ffn, ab_w_in=ab_w_in, ab_ret_decay=ab_ret_decay,
                  ab_ret_gn=ab_ret_gn, ab_s5_lam_re=ab_s5_lam_re, ab_s5_lam_im=ab_s5_lam_im,
                  ab_s5_log_dt=ab_s5_log_dt, ab_s5_b_re=ab_s5_b_re, ab_s5_b_im=ab_s5_b_im,
                  ab_s5_c_re=ab_s5_c_re, ab_s5_c_im=ab_s5_c_im, ab_s5_d=ab_s5_d, ab_s5_glu_w=ab_s5_glu_w,
                  ab_s5_glu_b=ab_s5_glu_b, ab_w_out=ab_w_out, na_w_qkv=na_w_qkv, na_q_gain=na_q_gain,
                  na_k_gain=na_k_gain, na_rpb=na_rpb, na_w_o=na_w_o, peer_w_q=peer_w_q,
                  peer_sub_keys=peer_sub_keys, peer_u=peer_u, peer_v=peer_v)
    return _forward(x_prompt, x_sample, params)
```
